```python
import math
import jax
import jax.numpy as jnp
from jax import lax
import numpy as np

D_MODEL = 2048
BATCH = 1
SEQ = 16384
DEPTH = 2

GRID_W = 64
CTX_LEN = 256
N_DIR = 2

HEAD_DIM = 128
ATTN_WIDTH = D_MODEL // 2
ATTN_Q_HEADS = ATTN_WIDTH // HEAD_DIM
ATTN_KV_HEADS = ATTN_Q_HEADS // 4
KV_WIDTH = ATTN_KV_HEADS * HEAD_DIM
Q_BLOCK = 128
ROPE_THETA = 10000.0

DN_WIDTH = D_MODEL // 4
DN_HEADS = DN_WIDTH // HEAD_DIM
DN_CONV = 5
DN_CHUNK = 64

S5_WIDTH = D_MODEL // 4
S5_GROUP_CH = 16
S5_GROUPS = S5_WIDTH // S5_GROUP_CH
S5_STATE = 64

MIX_WIDTH = ATTN_WIDTH + DN_WIDTH + S5_WIDTH
IN_SPLITS = (ATTN_WIDTH, KV_WIDTH, KV_WIDTH, 3 * DN_WIDTH, DN_WIDTH, N_DIR * DN_HEADS, N_DIR * DN_HEADS, S5_WIDTH)
IN_COLS = ATTN_WIDTH + 2 * KV_WIDTH + 4 * DN_WIDTH + 2 * N_DIR * DN_HEADS + S5_WIDTH

MOE_GROUPS = 4
MOE_PER_GROUP = 4
MOE_EXPERTS = MOE_GROUPS * MOE_PER_GROUP
MOE_TOPK = 2
MOE_HIDDEN = D_MODEL // 4

EPS = 1e-6

kernel_name = 'hybrid_attn_deltanet_s5_hmoe_dit'


def _rmsnorm(x, w):
    xf = x.astype(jnp.float32)
    y = xf * lax.rsqrt(jnp.mean(xf * xf, axis=-1, keepdims=True) + EPS)
    return (y * w.astype(jnp.float32)).astype(x.dtype)


def _l2norm(x):
    return x * lax.rsqrt(jnp.sum(x * x, axis=-1, keepdims=True) + EPS)


def _modulate(x, w, shift, scale):
    return _rmsnorm(x, w) * (1 + scale[:, None, :]) + shift[:, None, :]


def _split_cols(p, sizes):
    out, start = [], 0
    for s in sizes:
        out.append(p[..., start:start + s])
        start += s
    return out


def _axial_rope_tables(n_tokens):
    rows = n_tokens // GRID_W
    t = jnp.arange(rows * GRID_W)
    row = (t // GRID_W).astype(jnp.float32)
    col = (t % GRID_W).astype(jnp.float32)
    half = HEAD_DIM // 2
    inv = ROPE_THETA ** (-jnp.arange(0, half, 2, dtype=jnp.float32) / half)
    ang_r = row[:, None] * inv[None, :]
    ang_c = col[:, None] * inv[None, :]
    ang = jnp.concatenate([ang_r, ang_r, ang_c, ang_c], axis=-1)
    return jnp.cos(ang), jnp.sin(ang)


def _apply_axial_rope(x, cos, sin):
    quarter = HEAD_DIM // 4
    xr = x.reshape(*x.shape[:-1], 2, 2, quarter)
    rot = jnp.stack([-xr[..., 1, :], xr[..., 0, :]], axis=-2).reshape(x.shape)
    return (x * cos[None, :, None, :] + rot * sin[None, :, None, :]).astype(x.dtype)


def _attn_heads(pq, pk, pv, q_norm_w, k_norm_w):
    bsz, t, _ = pq.shape
    q = _rmsnorm(pq.reshape(bsz, t, ATTN_Q_HEADS, HEAD_DIM), q_norm_w)
    k = _rmsnorm(pk.reshape(bsz, t, ATTN_KV_HEADS, HEAD_DIM), k_norm_w)
    v = pv.reshape(bsz, t, ATTN_KV_HEADS, HEAD_DIM)
    return q, k, v


def _block_attention(q, k, v):
    bsz, s, hq, hd = q.shape
    hkv = k.shape[2]
    grp = hq // hkv
    nb = s // Q_BLOCK
    qb = q.reshape(bsz, nb, Q_BLOCK, hkv, grp, hd).transpose(1, 0, 2, 3, 4, 5)
    scale = hd ** -0.5

    def one_block(q_blk):
        sc = jnp.einsum('bqhgd,bkhd->bhgqk', q_blk, k, preferred_element_type=jnp.float32) * scale
        p = jax.nn.softmax(sc, axis=-1).astype(v.dtype)
        return jnp.einsum('bhgqk,bkhd->bqhgd', p, v)

    o = lax.map(one_block, qb)
    return o.transpose(1, 0, 2, 3, 4, 5).reshape(bsz, s, hq * hd)


def _short_conv(x, w):
    ch = x.shape[-1]
    pad = (w.shape[0] - 1) // 2
    y = lax.conv_general_dilated(x, w[:, None, :].astype(x.dtype), window_strides=(1,), padding=[(pad, pad)],
                                 dimension_numbers=('NWC', 'WIO', 'NWC'), feature_group_count=ch)
    return jax.nn.silu(y)


def _gdn_prepare(qkv, dbeta, dalpha, conv_w, a_log, dt_bias):
    bsz, t, _ = qkv.shape
    qkv = _short_conv(qkv, conv_w).astype(jnp.float32)
    q, k, v = jnp.split(qkv, 3, axis=-1)
    q = _l2norm(q.reshape(bsz, t, DN_HEADS, HEAD_DIM)) * HEAD_DIM ** -0.5
    k = _l2norm(k.reshape(bsz, t, DN_HEADS, HEAD_DIM))
    v = v.reshape(bsz, t, DN_HEADS, HEAD_DIM)
    beta = jax.nn.sigmoid(dbeta.astype(jnp.float32)).reshape(bsz, t, N_DIR, DN_HEADS)
    g = -jnp.exp(a_log.astype(jnp.float32)) * jax.nn.softplus(
        dalpha.astype(jnp.float32).reshape(bsz, t, N_DIR, DN_HEADS) + dt_bias.astype(jnp.float32))
    return q, k, v, g, beta


def _gated_delta_chunked(q, k, v, g, beta, s0):
    bsz, t, h, kd_ = q.shape
    vd = v.shape[-1]
    c = DN_CHUNK
    n = t // c

    def ch(a):
        return jnp.moveaxis(a.reshape(bsz, n, c, h, *a.shape[3:]), 3, 1)

    q, k, v, g, beta = ch(q), ch(k), ch(v), ch(g), ch(beta)
    gc = jnp.cumsum(g, axis=-1)
    idx = jnp.arange(c)
    incl = idx[:, None] >= idx[None, :]
    strict = idx[:, None] > idx[None, :]
    decay = jnp.exp(jnp.where(incl, gc[..., :, None] - gc[..., None, :], -jnp.inf))
    kb = k * beta[..., None]
    low = jnp.where(strict, jnp.einsum('bhncd,bhnsd->bhncs', kb, k) * decay, 0.0)
    eye = jnp.eye(c, dtype=q.dtype)
    rhs = jnp.concatenate([v * beta[..., None], kb * jnp.exp(gc)[..., None]], axis=-1)
    sol = lax.linalg.triangular_solve(eye + low, rhs, left_side=True, lower=True, unit_diagonal=True)
    u, w = sol[..., :vd], sol[..., vd:]
    qk = jnp.einsum('bhncd,bhnsd->bhncs', q, k) * decay
    qg = q * jnp.exp(gc)[..., None]
    glast = gc[..., -1]
    kdec = k * jnp.exp(glast[..., None] - gc)[..., None]

    def step(s, xs):
        qg_n, qk_n, u_n, w_n, kd_n, gl_n = xs
        v_new = u_n - jnp.einsum('bhck,bhkv->bhcv', w_n, s)
        o_n = jnp.einsum('bhck,bhkv->bhcv', qg_n, s) + jnp.einsum('bhcs,bhsv->bhcv', qk_n, v_new)
        s = s * jnp.exp(gl_n)[..., None, None] + jnp.einsum('bhck,bhcv->bhkv', kd_n, v_new)
        return s, o_n

    xs = tuple(jnp.moveaxis(a, 2, 0) for a in (qg, qk, u, w, kdec, glast))
    s_final, o = lax.scan(step, s0, xs)
    o = jnp.moveaxis(jnp.moveaxis(o, 0, 2), 1, 3).reshape(bsz, t, h, vd)
    return o, s_final


def _gdn_bidir(q, k, v, g, beta, s0_f, s0_b):
    o_f, s_f = _gated_delta_chunked(q, k, v, g[:, :, 0], beta[:, :, 0], s0_f)
    fl = lambda a: jnp.flip(a, axis=1)
    o_b, s_b = _gated_delta_chunked(fl(q), fl(k), fl(v), fl(g[:, :, 1]), fl(beta[:, :, 1]), s0_b)
    return o_f + fl(o_b), s_f, s_b


def _gdn_output(o, gate, norm_w):
    bsz, t, _ = gate.shape
    gated = _rmsnorm(o, norm_w) * jax.nn.silu(gate.astype(jnp.float32).reshape(bsz, t, DN_HEADS, HEAD_DIM))
    return gated.reshape(bsz, t, DN_WIDTH).astype(gate.dtype)


def _s5_discretize(a_re, a_im, log_dt, b_c):
    lam = lax.complex(a_re.astype(jnp.float32), a_im.astype(jnp.float32))
    dt = jnp.exp(log_dt.astype(jnp.float32))[:, None]
    lam_bar = jnp.exp(lam * dt)
    b_bar = ((lam_bar - 1) / lam)[..., None] * b_c
    return lam_bar, b_bar


def _lin_rec(e1, e2):
    a1, b1 = e1
    a2, b2 = e2
    return a1 * a2, a2 * b1 + b2


def _s5_scan(u, lam_bar, b_bar, x0):
    bu = jnp.einsum('btgh,gph->btgp', u.astype(jnp.complex64), b_bar)
    if x0 is not None:
        bu = bu.at[:, 0].add(lam_bar * x0)
    a = jnp.broadcast_to(lam_bar, bu.shape)
    _, xs = lax.associative_scan(_lin_rec, (a, bu), axis=1)
    return xs


def _s5_stream(u, lam_f, bb_f, lam_b, bb_b, c_f, c_b, d, x0_f, x0_b):
    bsz, t, _ = u.shape
    ug = u.astype(jnp.float32).reshape(bsz, t, S5_GROUPS, S5_GROUP_CH)
    xf = _s5_scan(ug, lam_f, bb_f, x0_f)
    xb = jnp.flip(_s5_scan(jnp.flip(ug, 1), lam_b, bb_b, x0_b), 1)
    y = jnp.real(jnp.einsum('btgp,ghp->btgh', xf, c_f) + jnp.einsum('btgp,ghp->btgh', xb, c_b)) + d * ug
    return y.reshape(bsz, t, S5_WIDTH), xf[:, -1], xb[:, 0]


def _s5_glu(y, w_glu, b_glu, dtype):
    z = jax.nn.gelu(y)
    return (z * jax.nn.sigmoid(z @ w_glu.astype(jnp.float32) + b_glu.astype(jnp.float32))).astype(dtype)


def _hier_moe(h, router_g_w, router_g_b, router_e_w, router_e_b, w_gate, w_up, w_down):
    bsz, t, _ = h.shape
    pg = jax.nn.softmax((h @ router_g_w).astype(jnp.float32) + router_g_b.astype(jnp.float32), axis=-1)
    pg_top, g_idx = lax.top_k(pg, 1)
    le = ((h @ router_e_w).astype(jnp.float32) + router_e_b.astype(jnp.float32)).reshape(
        bsz, t, MOE_GROUPS, MOE_PER_GROUP)
    le_sel = jnp.take_along_axis(le, g_idx[..., None], axis=2)[:, :, 0]
    pe_top, e_idx = lax.top_k(jax.nn.softmax(le_sel, axis=-1), MOE_TOPK)
    pe_top = pe_top / jnp.sum(pe_top, axis=-1, keepdims=True)
    expert = g_idx * MOE_PER_GROUP + e_idx
    weight = pg_top * pe_top
    combine = jnp.einsum('btk,btke->bte', weight, jax.nn.one_hot(expert, MOE_EXPERTS, dtype=jnp.float32)).astype(h.dtype)
    y = jnp.zeros_like(h)
    for e in range(MOE_EXPERTS):
        he = jax.nn.silu(h @ w_gate[e]) * (h @ w_up[e])
        y = y + combine[..., e:e + 1] * (he @ w_down[e])
    return y


def _layer(xc, xl, mod_c, mod_l, cos, sin, norm1_w, norm2_w, w_in, q_norm_w, k_norm_w,
           dn_conv_w, dn_a_log, dn_dt_bias, dn_out_norm_w,
           s5_a_re, s5_a_im, s5_log_dt, s5_b_re, s5_b_im, s5_c_re, s5_c_im, s5_d, s5_w_glu, s5_b_glu,
           w_out, router_g_w, router_g_b, router_e_w, router_e_b, w_gate, w_up, w_down, need_ctx):
    sh1_c, sc1_c, g1_c, sh2_c, sc2_c, g2_c = jnp.split(mod_c, 6, axis=-1)
    sh1_l, sc1_l, g1_l, sh2_l, sc2_l, g2_l = jnp.split(mod_l, 6, axis=-1)
    dt_ = xl.dtype

    pc = _split_cols(_modulate(xc, norm1_w, sh1_c, sc1_c) @ w_in, IN_SPLITS)
    pl = _split_cols(_modulate(xl, norm1_w, sh1_l, sc1_l) @ w_in, IN_SPLITS)

    qc, kc, vc = _attn_heads(pc[0], pc[1], pc[2], q_norm_w, k_norm_w)
    ql, kl, vl = _attn_heads(pl[0], pl[1], pl[2], q_norm_w, k_norm_w)
    ql = _apply_axial_rope(ql, cos, sin)
    kl = _apply_axial_rope(kl, cos, sin)
    att_l = _block_attention(ql, jnp.concatenate([kc, kl], axis=1), jnp.concatenate([vc, vl], axis=1))

    prep_c = _gdn_prepare(pc[3], pc[5], pc[6], dn_conv_w, dn_a_log, dn_dt_bias)
    prep_l = _gdn_prepare(pl[3], pl[5], pl[6], dn_conv_w, dn_a_log, dn_dt_bias)
    s0 = jnp.zeros((xc.shape[0], DN_HEADS, HEAD_DIM, HEAD_DIM), jnp.float32)
    o_c, s_f, s_b = _gdn_bidir(*prep_c, s0, s0)
    o_l, _, _ = _gdn_bidir(*prep_l, s_f, s_b)
    dn_l = _gdn_output(o_l, pl[4], dn_out_norm_w)

    b_c = lax.complex(s5_b_re.astype(jnp.float32), s5_b_im.astype(jnp.float32))
    lam_f, bb_f = _s5_discretize(s5_a_re[0], s5_a_im[0], s5_log_dt[0], b_c)
    lam_b, bb_b = _s5_discretize(s5_a_re[1], s5_a_im[1], s5_log_dt[1], b_c)
    c_f = lax.complex(s5_c_re[0].astype(jnp.float32), s5_c_im[0].astype(jnp.float32))
    c_b = lax.complex(s5_c_re[1].astype(jnp.float32), s5_c_im[1].astype(jnp.float32))
    d = s5_d.astype(jnp.float32).reshape(S5_GROUPS, S5_GROUP_CH)
    y_c, xf_c, xb_c = _s5_stream(pc[7], lam_f, bb_f, lam_b, bb_b, c_f, c_b, d, None, None)
    y_l, _, _ = _s5_stream(pl[7], lam_f, bb_f, lam_b, bb_b, c_f, c_b, d, xf_c, xb_c)
    s5_l = _s5_glu(y_l, s5_w_glu, s5_b_glu, dt_)

    mix_l = jnp.concatenate([att_l.astype(dt_), dn_l.astype(dt_), s5_l], axis=-1) @ w_out
    xl = xl + g1_l[:, None, :] * mix_l
    xl = xl + g2_l[:, None, :] * _hier_moe(_modulate(xl, norm2_w, sh2_l, sc2_l), router_g_w, router_g_b,
                                           router_e_w, router_e_b, w_gate, w_up, w_down)
    if need_ctx:
        att_c = _block_attention(qc, kc, vc)
        dn_c = _gdn_output(o_c, pc[4], dn_out_norm_w)
        s5_c = _s5_glu(y_c, s5_w_glu, s5_b_glu, dt_)
        mix_c = jnp.concatenate([att_c.astype(dt_), dn_c.astype(dt_), s5_c], axis=-1) @ w_out
        xc = xc + g1_c[:, None, :] * mix_c
        xc = xc + g2_c[:, None, :] * _hier_moe(_modulate(xc, norm2_w, sh2_c, sc2_c), router_g_w, router_g_b,
                                               router_e_w, router_e_b, w_gate, w_up, w_down)
    return xc, xl


def setup_inputs(seed: int = 0) -> dict:
    key = jax.random.key(seed)
    keys = iter(jax.random.split(key, 48))
    f32 = jnp.float32
    L, D = DEPTH, D_MODEL

    def normal(shape, std):
        return jax.random.normal(next(keys), shape, f32) * std

    def uniform(shape, lo, hi):
        return jax.random.uniform(next(keys), shape, f32, lo, hi)

    n_idx = jnp.arange(S5_STATE, dtype=f32)
    dn_dt = jnp.exp(uniform((L, N_DIR, DN_HEADS), math.log(1e-3), math.log(1e-1)))
    return {
        'x': normal((BATCH, SEQ, D), 1.0),
        'c': normal((BATCH, D), 1.0),
        'ctx': normal((BATCH, CTX_LEN, D), 1.0),
        'c_ctx': normal((D,), 1.0),
        'w_ada': normal((L, D, 6 * D), D ** -0.5),
        'b_ada': normal((L, 6 * D), 0.02),
        'norm1_w': 1.0 + normal((L, D), 0.02),
        'norm2_w': 1.0 + normal((L, D), 0.02),
        'w_in': normal((L, D, IN_COLS), D ** -0.5),
        'q_norm_w': 1.0 + normal((L, HEAD_DIM), 0.02),
        'k_norm_w': 1.0 + normal((L, HEAD_DIM), 0.02),
        'dn_conv_w': normal((L, DN_CONV, 3 * DN_WIDTH), DN_CONV ** -0.5),
        'dn_a_log': jnp.log(uniform((L, N_DIR, DN_HEADS), 1.0, 16.0)),
        'dn_dt_bias': dn_dt + jnp.log(-jnp.expm1(-dn_dt)),
        'dn_out_norm_w': 1.0 + normal((L, HEAD_DIM), 0.02),
        's5_a_re': -0.5 * jnp.exp(normal((L, N_DIR, S5_GROUPS, S5_STATE), 0.05)),
        's5_a_im': jnp.broadcast_to(math.pi * n_idx, (L, N_DIR, S5_GROUPS, S5_STATE)),
        's5_log_dt': uniform((L, N_DIR, S5_GROUPS), math.log(1e-3), math.log(1e-1)),
        's5_b_re': normal((L, S5_GROUPS, S5_STATE, S5_GROUP_CH), (2 * S5_GROUP_CH) ** -0.5),
        's5_b_im': normal((L, S5_GROUPS, S5_STATE, S5_GROUP_CH), (2 * S5_GROUP_CH) ** -0.5),
        's5_c_re': normal((L, N_DIR, S5_GROUPS, S5_GROUP_CH, S5_STATE), S5_STATE ** -0.5),
        's5_c_im': normal((L, N_DIR, S5_GROUPS, S5_GROUP_CH, S5_STATE), S5_STATE ** -0.5),
        's5_d': normal((L, S5_WIDTH), 1.0),
        's5_w_glu': normal((L, S5_WIDTH, S5_WIDTH), S5_WIDTH ** -0.5),
        's5_b_glu': normal((L, S5_WIDTH), 0.02),
        'w_out': normal((L, MIX_WIDTH, D), MIX_WIDTH ** -0.5),
        'router_g_w': normal((L, D, MOE_GROUPS), D ** -0.5),
        'router_g_b': normal((L, MOE_GROUPS), 0.01),
        'router_e_w': normal((L, D, MOE_EXPERTS), D ** -0.5),
        'router_e_b': normal((L, MOE_EXPERTS), 0.01),
        'w_gate': normal((L, MOE_EXPERTS, D, MOE_HIDDEN), D ** -0.5),
        'w_up': normal((L, MOE_EXPERTS, D, MOE_HIDDEN), D ** -0.5),
        'w_down': normal((L, MOE_EXPERTS, MOE_HIDDEN, D), MOE_HIDDEN ** -0.5),
    }


def reference(x, c, ctx, c_ctx, w_ada, b_ada, norm1_w, norm2_w, w_in, q_norm_w, k_norm_w,
              dn_conv_w, dn_a_log, dn_dt_bias, dn_out_norm_w,
              s5_a_re, s5_a_im, s5_log_dt, s5_b_re, s5_b_im, s5_c_re, s5_c_im, s5_d, s5_w_glu, s5_b_glu,
              w_out, router_g_w, router_g_b, router_e_w, router_e_b, w_gate, w_up, w_down):
    cos, sin = _axial_rope_tables(x.shape[1])
    xc, xl = ctx, x
    for l in range(DEPTH):
        mod_l = jax.nn.silu(c) @ w_ada[l] + b_ada[l]
        mod_c = (jax.nn.silu(c_ctx) @ w_ada[l] + b_ada[l])[None, :]
        xc, xl = _layer(xc, xl, mod_c, mod_l, cos, sin, norm1_w[l], norm2_w[l], w_in[l], q_norm_w[l], k_norm_w[l],
                        dn_conv_w[l], dn_a_log[l], dn_dt_bias[l], dn_out_norm_w[l],
                        s5_a_re[l], s5_a_im[l], s5_log_dt[l], s5_b_re[l], s5_b_im[l], s5_c_re[l], s5_c_im[l],
                        s5_d[l], s5_w_glu[l], s5_b_glu[l], w_out[l], router_g_w[l], router_g_b[l],
                        router_e_w[l], router_e_b[l], w_gate[l], w_up[l], w_down[l],
                        need_ctx=(l < DEPTH - 1))
    return xl
```

```python
import functools
import math

import jax
import jax.numpy as jnp
from jax import lax
from jax.experimental import pallas as pl
from jax.experimental.pallas import tpu as pltpu

F32 = jnp.float32
BF16 = jnp.bfloat16

D_MODEL = 2048
DEPTH = 2
GRID_W = 64
N_DIR = 2
HEAD_DIM = 128
ATTN_WIDTH = D_MODEL // 2
ATTN_Q_HEADS = ATTN_WIDTH // HEAD_DIM
ATTN_KV_HEADS = ATTN_Q_HEADS // 4
KV_WIDTH = ATTN_KV_HEADS * HEAD_DIM
ROPE_THETA = 10000.0
DN_WIDTH = D_MODEL // 4
DN_HEADS = DN_WIDTH // HEAD_DIM
DN_CONV = 5
DN_CHUNK = 64
S5_WIDTH = D_MODEL // 4
S5_GROUP_CH = 16
S5_GROUPS = S5_WIDTH // S5_GROUP_CH
S5_STATE = 64
MIX_WIDTH = ATTN_WIDTH + DN_WIDTH + S5_WIDTH
IN_SPLITS = (ATTN_WIDTH, KV_WIDTH, KV_WIDTH, 3 * DN_WIDTH, DN_WIDTH, N_DIR * DN_HEADS, N_DIR * DN_HEADS, S5_WIDTH)
MOE_GROUPS = 4
MOE_PER_GROUP = 4
MOE_EXPERTS = MOE_GROUPS * MOE_PER_GROUP
MOE_HIDDEN = D_MODEL // 4
EPS = 1e-6

LANES = 128
SUBLANES = 8
VMEM_LIMIT_BYTES = 56 * 1024 * 1024

N_GATES = N_DIR * DN_HEADS
S5_CHUNK = 16
S5_CW = S5_CHUNK * S5_GROUP_CH


def _cparams(*sem):
    return pltpu.CompilerParams(dimension_semantics=sem, vmem_limit_bytes=VMEM_LIMIT_BYTES)


def _const_spec(shape):
    nd = len(shape)
    return pl.BlockSpec(shape, lambda *_: (0,) * nd, pipeline_mode=pl.Buffered(1))


def _silu(x):
    return x * jax.nn.sigmoid(x)


MOD_TN = 1024


def _mod_kernel(cv_ref, w_ref, b_ref, o_ref):
    s = _silu(cv_ref[...]).astype(BF16)
    o_ref[...] = jnp.dot(s, w_ref[...].astype(BF16), preferred_element_type=F32) + b_ref[...]


def _adaln_mod(cvec, w_ada, b_ada):
    n6 = w_ada.shape[-1]
    return pl.pallas_call(
        _mod_kernel,
        grid=(DEPTH, n6 // MOD_TN),
        in_specs=[
            pl.BlockSpec((SUBLANES, D_MODEL), lambda l, j: (0, 0)),
            pl.BlockSpec((None, D_MODEL, MOD_TN), lambda l, j: (l, 0, j)),
            pl.BlockSpec((None, 1, MOD_TN), lambda l, j: (l, 0, j)),
        ],
        out_specs=pl.BlockSpec((None, SUBLANES, MOD_TN), lambda l, j: (l, 0, j)),
        out_shape=jax.ShapeDtypeStruct((DEPTH, SUBLANES, n6), F32),
        compiler_params=_cparams("parallel", "parallel"),
        name="adaln_mod",
    )(cvec, w_ada, b_ada.reshape(DEPTH, 1, n6))


def _rms_rows(x):
    return x * lax.rsqrt(jnp.mean(x * x, axis=-1, keepdims=True) + EPS)


def _in_kernel(rope, x_ref, nw_ref, sh_ref, sc_ref, wqt_ref, wk_ref, wvt_ref, wdn_ref, wg_ref, wba_ref, wbat_ref,
               ws5_ref, qn_ref, kn_ref, cos_ref, sin_ref, cost_ref, sint_ref,
               qt_ref, k_ref, vt_ref, dn_ref, g_ref, ba_ref, bat_ref, u_ref):
    x = x_ref[...]
    h = (_rms_rows(x) * nw_ref[...] * (1.0 + sc_ref[...]) + sh_ref[...]).astype(BF16)
    nt = (((1,), (1,)), ((), ()))

    qt = lax.dot_general(wqt_ref[...], h, nt, preferred_element_type=F32)
    q_scale = HEAD_DIM ** -0.5
    for hd in range(ATTN_Q_HEADS):
        qh = qt[hd * HEAD_DIM:(hd + 1) * HEAD_DIM, :]
        qh = qh * lax.rsqrt(jnp.mean(qh * qh, axis=0, keepdims=True) + EPS) * qn_ref[...]
        if rope:
            a, b, c, d = (qh[i * 32:(i + 1) * 32, :] for i in range(4))
            rot = jnp.concatenate([-b, a, -d, c], axis=0)
            qh = qh * cost_ref[...] + rot * sint_ref[...]
        qt_ref[hd * HEAD_DIM:(hd + 1) * HEAD_DIM, :] = (qh * q_scale).astype(BF16)

    k = jnp.dot(h, wk_ref[...], preferred_element_type=F32)
    lane = lax.broadcasted_iota(jnp.int32, (1, HEAD_DIM), 1)
    first_half = (lane % 64) < 32
    for hd in range(ATTN_KV_HEADS):
        kh = _rms_rows(k[:, hd * HEAD_DIM:(hd + 1) * HEAD_DIM]) * kn_ref[...]
        if rope:
            rot = jnp.where(first_half, -pltpu.roll(kh, 96, 1), pltpu.roll(kh, 32, 1))
            kh = kh * cos_ref[...] + rot * sin_ref[...]
        k_ref[:, hd * HEAD_DIM:(hd + 1) * HEAD_DIM] = kh.astype(BF16)

    vt_ref[...] = lax.dot_general(wvt_ref[...], h, nt, preferred_element_type=F32).astype(BF16)
    dn_ref[...] = jnp.dot(h, wdn_ref[...], preferred_element_type=F32)
    g_ref[...] = jnp.dot(h, wg_ref[...], preferred_element_type=F32)
    ba_ref[...] = jnp.dot(h, wba_ref[...], preferred_element_type=F32)
    bat_ref[...] = lax.dot_general(wbat_ref[...], h, nt, preferred_element_type=F32)
    u_ref[...] = jnp.dot(h, ws5_ref[...], preferred_element_type=F32).astype(BF16)


def _in_proj(x, mod, lw, rope_tabs, rope, tm):
    t = x.shape[0]
    cos, sin, cost, sint = rope_tabs
    row = lambda w: pl.BlockSpec((tm, w), lambda i: (i, 0))
    col = lambda w: pl.BlockSpec((w, tm), lambda i: (0, i))
    vec = lambda a: _const_spec(a.shape)
    ws = [lw["wqt"], lw["wk"], lw["wvt"], lw["wdn"], lw["wg"], lw["wba"], lw["wbat"], lw["ws5"]]
    in_specs = ([row(D_MODEL), vec(lw["norm1_w"]), vec(mod["sh1"]), vec(mod["sc1"])] + [vec(w) for w in ws]
                + [vec(lw["qn_col"]), vec(lw["kn_row"]), row(HEAD_DIM), row(HEAD_DIM), col(HEAD_DIM), col(HEAD_DIM)])
    out_specs = [col(ATTN_WIDTH), row(KV_WIDTH), pl.BlockSpec((None, KV_WIDTH, tm), lambda i: (i, 0, 0)), row(3 * DN_WIDTH), row(DN_WIDTH), row(LANES),
                 col(2 * N_GATES), row(S5_WIDTH)]
    sds = jax.ShapeDtypeStruct
    out_shape = [sds((ATTN_WIDTH, t), BF16), sds((t, KV_WIDTH), BF16), sds((t // tm, KV_WIDTH, tm), BF16),
                 sds((t, 3 * DN_WIDTH), F32), sds((t, DN_WIDTH), F32), sds((t, LANES), F32),
                 sds((2 * N_GATES, t), F32), sds((t, S5_WIDTH), BF16)]
    return pl.pallas_call(
        functools.partial(_in_kernel, rope),
        grid=(t // tm,),
        in_specs=in_specs, out_specs=out_specs, out_shape=out_shape,
        compiler_params=_cparams("parallel"),
        name="in_proj",
    )(x, lw["norm1_w"], mod["sh1"], mod["sc1"], *ws, lw["qn_col"], lw["kn_row"], cos, sin, cost, sint)


ATT_TQ = 256
ATT_TK = 256
GQA = ATTN_Q_HEADS // ATTN_KV_HEADS


def _attn_kernel(nj, qt_ref, k_ref, vt_ref, ot_ref, m_ref, l_ref, acc_ref):
    tq = qt_ref.shape[1]
    q = jnp.concatenate([qt_ref[h * HEAD_DIM:(h + 1) * HEAD_DIM, :] for h in range(GQA)], axis=1)
    m_ref[...] = jnp.full(m_ref.shape, -jnp.inf, F32)
    l_ref[...] = jnp.zeros(l_ref.shape, F32)
    acc_ref[...] = jnp.zeros(acc_ref.shape, F32)

    def body(j, carry):
        kb = k_ref[pl.ds(pl.multiple_of(j * ATT_TK, ATT_TK), ATT_TK), :]
        s = jnp.dot(kb, q, preferred_element_type=F32)
        m_old = m_ref[...]
        m_new = jnp.maximum(m_old, jnp.max(s, axis=0, keepdims=True))
        alpha = jnp.exp(m_old - m_new)
        p = jnp.exp(s - m_new)
        l_ref[...] = alpha * l_ref[...] + jnp.sum(p, axis=0, keepdims=True)
        acc_ref[...] = alpha * acc_ref[...] + jnp.dot(vt_ref[j], p.astype(BF16), preferred_element_type=F32)
        m_ref[...] = m_new
        return carry

    lax.fori_loop(0, nj, body, 0)
    o = acc_ref[...] * (1.0 / l_ref[...])
    for h in range(GQA):
        ot_ref[h * HEAD_DIM:(h + 1) * HEAD_DIM, :] = o[:, h * tq:(h + 1) * tq].astype(BF16)


def _attention(qt, k_all, vt_all):
    t = qt.shape[1]
    tk_total = k_all.shape[0]
    nj = tk_total // ATT_TK
    tq = min(ATT_TQ, t)
    return pl.pallas_call(
        functools.partial(_attn_kernel, nj),
        grid=(ATTN_KV_HEADS, t // tq),
        in_specs=[
            pl.BlockSpec((GQA * HEAD_DIM, tq), lambda g, i: (g, i)),
            pl.BlockSpec((tk_total, HEAD_DIM), lambda g, i: (0, g)),
            pl.BlockSpec((nj, HEAD_DIM, ATT_TK), lambda g, i: (0, g, 0)),
        ],
        out_specs=pl.BlockSpec((GQA * HEAD_DIM, tq), lambda g, i: (g, i)),
        out_shape=jax.ShapeDtypeStruct((ATTN_WIDTH, t), BF16),
        scratch_shapes=[pltpu.VMEM((1, GQA * tq), F32), pltpu.VMEM((1, GQA * tq), F32),
                        pltpu.VMEM((HEAD_DIM, GQA * tq), F32)],
        compiler_params=_cparams("parallel", "parallel"),
        name="attention",
    )(qt, k_all, vt_all)


def _softplus(x):
    return jnp.maximum(x, 0.0) + jnp.log(1.0 + jnp.exp(-jnp.abs(x)))


def _dot_f32(a, b):
    return jnp.dot(a, b, precision=lax.Precision.HIGHEST, preferred_element_type=F32)


def _gdn_prep_kernel(nblk, cur_ref, prev_ref, next_ref, cw_ref, ba_ref, bat_ref, alog_r, dtb_r, alog_c, dtb_c,
                     q_ref, k_ref, v_ref, gcol_ref, grow_ref, xx_ref):
    i = pl.program_id(0)
    tm = cur_ref.shape[0]
    pad = (DN_CONV - 1) // 2
    xx_ref[0:SUBLANES, :] = jnp.where(i > 0, prev_ref[...], 0.0)
    xx_ref[SUBLANES:SUBLANES + tm, :] = cur_ref[...]
    xx_ref[SUBLANES + tm:2 * SUBLANES + tm, :] = jnp.where(i < nblk - 1, next_ref[...], 0.0)
    acc = None
    for j in range(DN_CONV):
        term = xx_ref[SUBLANES - pad + j:SUBLANES - pad + j + tm, :] * cw_ref[j:j + 1, :]
        acc = term if acc is None else acc + term
    y = _silu(acc)
    for hd in range(DN_HEADS):
        sl = slice(hd * HEAD_DIM, (hd + 1) * HEAD_DIM)
        qh = y[:, sl]
        q_ref[:, sl] = qh * (lax.rsqrt(jnp.sum(qh * qh, axis=-1, keepdims=True) + EPS) * HEAD_DIM ** -0.5)
        kh = y[:, DN_WIDTH + hd * HEAD_DIM:DN_WIDTH + (hd + 1) * HEAD_DIM]
        k_ref[:, sl] = kh * lax.rsqrt(jnp.sum(kh * kh, axis=-1, keepdims=True) + EPS)
    v_ref[...] = y[:, 2 * DN_WIDTH:]

    r = lax.broadcasted_iota(jnp.int32, (tm, tm), 0)
    c = lax.broadcasted_iota(jnp.int32, (tm, tm), 1)
    same = (r // DN_CHUNK) == (c // DN_CHUNK)
    tri_le = jnp.where(same & (c <= r), 1.0, 0.0)
    tri_ge = jnp.where(same & (c >= r), 1.0, 0.0)

    ba = ba_ref[...]
    lane = lax.broadcasted_iota(jnp.int32, (1, LANES), 1)
    g = -jnp.exp(alog_r[...]) * _softplus(ba + dtb_r[...])
    gc = jnp.where(lane < N_GATES + DN_HEADS, _dot_f32(tri_le, g), _dot_f32(tri_ge, g))
    gcol_ref[...] = jnp.where(lane < N_GATES, jax.nn.sigmoid(ba), gc)

    gt = -jnp.exp(alog_c[...]) * _softplus(bat_ref[...] + dtb_c[...])
    row = lax.broadcasted_iota(jnp.int32, (2 * N_GATES, 1), 0)
    gct = jnp.where(row < N_GATES + DN_HEADS, _dot_f32(gt, tri_ge), _dot_f32(gt, tri_le))
    for ch in range(tm // DN_CHUNK):
        grow_ref[ch] = gct[N_GATES:, ch * DN_CHUNK:(ch + 1) * DN_CHUNK]


def _gdn_prep(dn, ba, bat, lw, tm):
    t = dn.shape[0]
    nblk = t // tm
    nsub = tm // SUBLANES
    w3 = 3 * DN_WIDTH
    row = lambda w: pl.BlockSpec((tm, w), lambda i: (i, 0))
    vec = lambda a: _const_spec(a.shape)
    sds = jax.ShapeDtypeStruct
    return pl.pallas_call(
        functools.partial(_gdn_prep_kernel, nblk),
        grid=(nblk,),
        in_specs=[row(w3),
                  pl.BlockSpec((SUBLANES, w3), lambda i: (jnp.maximum(i * nsub - 1, 0), 0)),
                  pl.BlockSpec((SUBLANES, w3), lambda i: (jnp.minimum((i + 1) * nsub, t // SUBLANES - 1), 0)),
                  vec(lw["conv_w"]), row(LANES), pl.BlockSpec((2 * N_GATES, tm), lambda i: (0, i)),
                  vec(lw["alog_row"]), vec(lw["dtb_row"]), vec(lw["alog_col"]), vec(lw["dtb_col"])],
        out_specs=[row(DN_WIDTH), row(DN_WIDTH), row(DN_WIDTH), row(LANES),
                   pl.BlockSpec((tm // DN_CHUNK, N_GATES, DN_CHUNK), lambda i: (i, 0, 0))],
        out_shape=[sds((t, DN_WIDTH), F32), sds((t, DN_WIDTH), F32), sds((t, DN_WIDTH), F32), sds((t, LANES), F32),
                   sds((t // DN_CHUNK, N_GATES, DN_CHUNK), F32)],
        scratch_shapes=[pltpu.VMEM((tm + 2 * SUBLANES, w3), F32)],
        compiler_params=_cparams("parallel"),
        name="gdn_prep",
    )(dn, dn, dn, lw["conv_w"], ba, bat, lw["alog_row"], lw["dtb_row"], lw["alog_col"], lw["dtb_col"])


def _dot_bf(a, b, dims=(((1,), (0,)), ((), ()))):
    return lax.dot_general(a.astype(BF16), b.astype(BF16), dims, preferred_element_type=F32)


_NT = (((1,), (1,)), ((), ()))
_TN = (((0,), (0,)), ((), ()))
TRI_BASE = 8


def _gdn_chunk(q, k, v, beta, gc, gr, s, fwd):
    c = DN_CHUNK
    ri = lax.broadcasted_iota(jnp.int32, (c, c), 0)
    ci = lax.broadcasted_iota(jnp.int32, (c, c), 1)
    incl = (ri >= ci) if fwd else (ri <= ci)
    strict = (ri > ci) if fwd else (ri < ci)
    decay = jnp.where(incl, jnp.exp(jnp.where(incl, gc - gr, 0.0)), 0.0)
    glast = gr[:, c - 1:c] if fwd else gr[:, 0:1]
    eg = jnp.exp(gc)
    kb = k * beta
    a = jnp.where(strict, _dot_bf(kb, k, _NT) * decay, 0.0)
    blk = lambda b: (ri // b) == (ci // b)
    a0 = jnp.where(blk(TRI_BASE), a, 0.0)
    x = jnp.where(ri == ci, 1.0, 0.0) - a0
    p = a0
    for _ in range(int(math.log2(TRI_BASE)) - 1):
        p = _dot_bf(p, p)
        x = x + _dot_bf(x, p)
    b = 2 * TRI_BASE
    while b <= c:
        l = jnp.where(blk(b) & jnp.logical_not(blk(b // 2)), a, 0.0)
        x = x - _dot_bf(x, _dot_bf(l, x))
        b *= 2
    sol = _dot_bf(x, jnp.concatenate([v * beta, kb * eg], axis=1))
    u, w = sol[:, :HEAD_DIM], sol[:, HEAD_DIM:]
    qk = _dot_bf(q, k, _NT) * decay
    ws = _dot_bf(jnp.concatenate([w, q * eg], axis=0), s)
    v_new = u - ws[:c]
    o = ws[c:] + _dot_bf(qk, v_new)
    s_new = s * jnp.exp(glast) + _dot_bf(k * jnp.exp(glast - gc), v_new, _TN)
    return o, s_new


def _gdn_kernel(nblk, qf, kf, vf, gcf, grf, qb, kb, vb, gcb, grb, s0_ref, of_ref, ob_ref, sout_ref, s_ref):
    i = pl.program_id(0)

    @pl.when(i == 0)
    def _():
        s_ref[...] = s0_ref[...]

    nb = qf.shape[0] // DN_CHUNK

    def body(j, carry):
        for d, (q_ref, k_ref, v_ref, gc_ref, gr_ref, o_ref, jj) in enumerate(
                ((qf, kf, vf, gcf, grf, of_ref, j), (qb, kb, vb, gcb, grb, ob_ref, nb - 1 - j))):
            rows = pl.ds(pl.multiple_of(jj * DN_CHUNK, DN_CHUNK), DN_CHUNK)
            gcol = gc_ref[rows, :]
            grow = gr_ref[jj]
            for h in range(DN_HEADS):
                gi = d * DN_HEADS + h
                sl = slice(h * HEAD_DIM, (h + 1) * HEAD_DIM)
                o, s_new = _gdn_chunk(q_ref[rows, sl], k_ref[rows, sl], v_ref[rows, sl],
                                      gcol[:, gi:gi + 1], gcol[:, N_GATES + gi:N_GATES + gi + 1],
                                      grow[gi:gi + 1, :], s_ref[gi], d == 0)
                o_ref[rows, sl] = o
                s_ref[gi] = s_new
        return carry

    lax.fori_loop(0, nb, body, 0)

    @pl.when(i == nblk - 1)
    def _():
        sout_ref[...] = s_ref[...]


def _gdn_scan(q, k, v, gcol, grow, s0, tm):
    t = q.shape[0]
    nblk = t // tm
    nch = tm // DN_CHUNK
    fwd = lambda w: pl.BlockSpec((tm, w), lambda i: (i, 0))
    bwd = lambda w: pl.BlockSpec((tm, w), lambda i: (nblk - 1 - i, 0))
    sds = jax.ShapeDtypeStruct
    s_shape = (N_GATES, HEAD_DIM, HEAD_DIM)
    return pl.pallas_call(
        functools.partial(_gdn_kernel, nblk),
        grid=(nblk,),
        in_specs=[fwd(DN_WIDTH), fwd(DN_WIDTH), fwd(DN_WIDTH), fwd(LANES),
                  pl.BlockSpec((nch, N_GATES, DN_CHUNK), lambda i: (i, 0, 0)),
                  bwd(DN_WIDTH), bwd(DN_WIDTH), bwd(DN_WIDTH), bwd(LANES),
                  pl.BlockSpec((nch, N_GATES, DN_CHUNK), lambda i: (nblk - 1 - i, 0, 0)),
                  _const_spec(s_shape)],
        out_specs=[fwd(DN_WIDTH), bwd(DN_WIDTH), pl.BlockSpec(s_shape, lambda i: (0, 0, 0))],
        out_shape=[sds((t, DN_WIDTH), F32), sds((t, DN_WIDTH), F32), sds(s_shape, F32)],
        scratch_shapes=[pltpu.VMEM(s_shape, F32)],
        compiler_params=_cparams("arbitrary"),
        name="gdn_scan",
    )(q, k, v, gcol, grow, q, k, v, gcol, grow, s0)


S5_SW = 4 * S5_STATE


def _s5_matrices(a_re, a_im, log_dt, b_re, b_im, c_re, c_im, d):
    L, G, P, H = S5_CHUNK, S5_GROUPS, S5_STATE, S5_GROUP_CH
    hi = lax.Precision.HIGHEST
    a_re, a_im = a_re.astype(F32), a_im.astype(F32)
    dt = jnp.exp(log_dt.astype(F32))[..., None]
    lam = lax.complex(a_re, a_im)
    coef = (jnp.exp(lam * dt) - 1) / lam
    b_c = lax.complex(b_re.astype(F32), b_im.astype(F32))
    bb = coef[..., None] * b_c[None]
    bb_re, bb_im = jnp.real(bb), jnp.imag(bb)
    tau = jnp.arange(L + 1, dtype=F32)[:, None, None, None]
    mag = jnp.exp(a_re[None] * dt[None] * tau)
    ang = a_im[None] * dt[None] * tau
    pw_re, pw_im = mag * jnp.cos(ang), mag * jnp.sin(ang)
    cr, ci = c_re.astype(F32), c_im.astype(F32)
    e_re = cr[None] * pw_re[:, :, :, None, :] - ci[None] * pw_im[:, :, :, None, :]
    e_im = cr[None] * pw_im[:, :, :, None, :] + ci[None] * pw_re[:, :, :, None, :]
    kk = (jnp.einsum('tdghp,dgpi->tdghi', e_re[:L], bb_re, precision=hi)
          - jnp.einsum('tdghp,dgpi->tdghi', e_im[:L], bb_im, precision=hi))
    s_i = jnp.arange(L)[:, None]
    t_i = jnp.arange(L)[None, :]
    lag = t_i - s_i
    kf = jnp.where((lag >= 0)[..., None, None, None], kk[jnp.clip(lag, 0, L - 1), 0], 0.0)
    kb = jnp.where((lag <= 0)[..., None, None, None], kk[jnp.clip(-lag, 0, L - 1), 1], 0.0)
    dskip = d.astype(F32).reshape(G, H)
    eye_t = (lag == 0).astype(F32)
    skip = eye_t[:, :, None, None, None] * (dskip[:, :, None] * jnp.eye(H, dtype=F32)[None])[None, None]
    toep = (kf + kb + skip).transpose(2, 0, 4, 1, 3).reshape(G, L * H, L * H)

    def loc(pr, pi, dr):
        w_re = pr[..., None] * bb_re[dr][None] - pi[..., None] * bb_im[dr][None]
        w_im = pr[..., None] * bb_im[dr][None] + pi[..., None] * bb_re[dr][None]
        f = lambda w: w.transpose(1, 0, 3, 2).reshape(G, L * H, P)
        return f(w_re), f(w_im)
    wf_re, wf_im = loc(pw_re[:L, 0][::-1], pw_im[:L, 0][::-1], 0)
    wb_re, wb_im = loc(pw_re[:L, 1], pw_im[:L, 1], 1)
    wcat = jnp.concatenate([toep, wf_re, wb_re, wf_im, wb_im], axis=-1).astype(BF16)

    carry = lambda e: e.transpose(1, 3, 0, 2).reshape(G, P, L * H)
    mf_re, mf_im = carry(e_re[1:, 0]), carry(-e_im[1:, 0])
    mb_re, mb_im = carry(e_re[1:, 1][::-1]), carry(-e_im[1:, 1][::-1])
    mcat = jnp.concatenate([mf_re, mb_re, mf_im, mb_im], axis=1).astype(BF16)
    al_re = jnp.concatenate([pw_re[L, 0], pw_re[L, 1]], axis=-1)
    al_im = jnp.concatenate([pw_im[L, 0], pw_im[L, 1]], axis=-1)
    al = jnp.stack([al_re.reshape(-1), al_im.reshape(-1)])
    return wcat, mcat, al


S5_HW = 2 * S5_STATE


def _s5_local_kernel(u_ref, w_ref, y_ref, xre_ref, xim_ref):
    r = jnp.dot(u_ref[...], w_ref[...], preferred_element_type=F32)
    y_ref[...] = r[:, :S5_CW]
    xre_ref[...] = r[:, S5_CW:S5_CW + S5_HW]
    xim_ref[...] = r[:, S5_CW + S5_HW:]


def _s5_local(ug, wcat):
    g, n, _ = ug.shape
    out = lambda w: pl.BlockSpec((n, w), lambda i: (0, i))
    sds = jax.ShapeDtypeStruct
    return pl.pallas_call(
        _s5_local_kernel,
        grid=(g,),
        in_specs=[pl.BlockSpec((None, n, S5_CW), lambda i: (i, 0, 0)),
                  pl.BlockSpec((None, S5_CW, S5_CW + S5_SW), lambda i: (i, 0, 0))],
        out_specs=[out(S5_CW), out(S5_HW), out(S5_HW)],
        out_shape=[sds((n, g * S5_CW), F32), sds((n, g * S5_HW), F32), sds((n, g * S5_HW), F32)],
        compiler_params=_cparams("parallel"),
        name="s5_local",
    )(ug, wcat)


def _s5_scan_kernel(nblk, xfr_ref, xfi_ref, xbr_ref, xbi_ref, al_ref, s0_ref,
                    cfr_ref, cfi_ref, cbr_ref, cbi_ref, sout_ref, s_ref):
    i = pl.program_id(0)

    @pl.when(i == 0)
    def _():
        s_ref[...] = s0_ref[...]

    nb = xfr_ref.shape[0]
    al_re, al_im = al_ref[0:1, :], al_ref[1:2, :]
    is_fwd = (lax.broadcasted_iota(jnp.int32, (1, al_ref.shape[1]), 1) % S5_HW) < S5_STATE

    def body(j, carry):
        re, im = carry
        rf, rb = pl.ds(j, 1), pl.ds(nb - 1 - j, 1)
        cfr_ref[rf, :] = re
        cfi_ref[rf, :] = im
        cbr_ref[rb, :] = re
        cbi_ref[rb, :] = im
        in_re = jnp.where(is_fwd, xfr_ref[rf, :], xbr_ref[rb, :])
        in_im = jnp.where(is_fwd, xfi_ref[rf, :], xbi_ref[rb, :])
        return al_re * re - al_im * im + in_re, al_re * im + al_im * re + in_im

    re, im = lax.fori_loop(0, nb, body, (s_ref[0:1, :], s_ref[1:2, :]))
    s_ref[0:1, :] = re
    s_ref[1:2, :] = im

    @pl.when(i == nblk - 1)
    def _():
        sout_ref[...] = s_ref[...]


def _s5_scan(x_re, x_im, al, s0, nb):
    n, w = x_re.shape
    nblk = n // nb
    fwd = pl.BlockSpec((nb, w), lambda i: (i, 0))
    bwd = pl.BlockSpec((nb, w), lambda i: (nblk - 1 - i, 0))
    sds = jax.ShapeDtypeStruct
    return pl.pallas_call(
        functools.partial(_s5_scan_kernel, nblk),
        grid=(nblk,),
        in_specs=[fwd, fwd, bwd, bwd, _const_spec(al.shape), _const_spec(s0.shape)],
        out_specs=[fwd, fwd, bwd, bwd, pl.BlockSpec(s0.shape, lambda i: (0, 0))],
        out_shape=[sds((n, w), F32)] * 4 + [sds(s0.shape, F32)],
        scratch_shapes=[pltpu.VMEM(s0.shape, F32)],
        compiler_params=_cparams("arbitrary"),
        name="s5_scan",
    )(x_re, x_im, x_re, x_im, al, s0)


def _s5_carry_kernel(y_ref, cfr_ref, cfi_ref, cbr_ref, cbi_ref, m_ref, o_ref):
    is_fwd = lax.broadcasted_iota(jnp.int32, (1, S5_HW), 1) < S5_STATE
    cin = jnp.concatenate([jnp.where(is_fwd, cfr_ref[...], cbr_ref[...]),
                           jnp.where(is_fwd, cfi_ref[...], cbi_ref[...])], axis=1).astype(BF16)
    o_ref[...] = y_ref[...] + jnp.dot(cin, m_ref[...], preferred_element_type=F32)


def _s5_carry(y, cins, mcat):
    n = y.shape[0]
    g = mcat.shape[0]
    blk = lambda w: pl.BlockSpec((n, w), lambda i: (0, i))
    return pl.pallas_call(
        _s5_carry_kernel,
        grid=(g,),
        in_specs=[blk(S5_CW)] + [blk(S5_HW)] * 4 + [pl.BlockSpec((None, S5_SW, S5_CW), lambda i: (i, 0, 0))],
        out_specs=blk(S5_CW),
        out_shape=jax.ShapeDtypeStruct(y.shape, F32),
        compiler_params=_cparams("parallel"),
        name="s5_carry",
    )(y, *cins, mcat)


def _s5_mixer(u, mats, s0):
    wcat, mcat, al = mats
    t = u.shape[0]
    n = t // S5_CHUNK
    ug = u.reshape(n, S5_CHUNK, S5_GROUPS, S5_GROUP_CH).transpose(2, 0, 1, 3).reshape(S5_GROUPS, n, S5_CW)
    y_loc, x_re, x_im = _s5_local(ug, wcat)
    *cins, s_fin = _s5_scan(x_re, x_im, al, s0, min(n, 128))
    y = _s5_carry(y_loc, cins, mcat)
    y = y.reshape(n, S5_GROUPS, S5_CHUNK, S5_GROUP_CH).transpose(0, 2, 1, 3).reshape(t, S5_WIDTH)
    return y, s_fin


ROUTER_E0 = MOE_GROUPS


def _gelu_tanh(x):
    return 0.5 * x * (1.0 + jnp.tanh(math.sqrt(2.0 / math.pi) * (x + 0.044715 * (x * x * x))))


def _route(logits):
    ninf = float("-inf")
    lane = lax.broadcasted_iota(jnp.int32, (1, LANES), 1)
    lanef = lane.astype(F32)
    first = lambda hit: jnp.min(jnp.where(hit, lanef, float(LANES)), axis=-1, keepdims=True)
    gl = jnp.where(lane < MOE_GROUPS, logits, ninf)
    gmax = jnp.max(gl, axis=-1, keepdims=True)
    pg_top = 1.0 / jnp.sum(jnp.exp(gl - gmax), axis=-1, keepdims=True)
    base = ROUTER_E0 + MOE_PER_GROUP * first(gl == gmax)
    el = jnp.where((lanef >= base) & (lanef < base + MOE_PER_GROUP), logits, ninf)
    emax = jnp.max(el, axis=-1, keepdims=True)
    esum = jnp.sum(jnp.exp(el - emax), axis=-1, keepdims=True)
    i1 = first(el == emax)
    el2 = jnp.where(lanef == i1, ninf, el)
    emax2 = jnp.max(el2, axis=-1, keepdims=True)
    i2 = first(el2 == emax2)
    p1 = 1.0 / esum
    p2 = jnp.exp(emax2 - emax) / esum
    w1 = p1 / (p1 + p2)
    w2 = p2 / (p1 + p2)
    return pg_top * (jnp.where(lanef == i1, w1, 0.0) + jnp.where(lanef == i2, w2, 0.0))


def _out_kernel(att_ref, of_ref, ob_ref, gate_ref, y_ref, x_ref, g1_ref, sh2_ref, sc2_ref, n2_ref, dnw_ref, bglu_ref,
                wglu_ref, woa_ref, wob_ref, woc_ref, wr_ref, br_ref, xo_ref, h_ref, comb_ref):
    o = of_ref[...] + ob_ref[...]
    gate = _silu(gate_ref[...])
    dn = jnp.concatenate(
        [_rms_rows(o[:, h * HEAD_DIM:(h + 1) * HEAD_DIM]) * dnw_ref[...] for h in range(DN_HEADS)], axis=1) * gate
    z = _gelu_tanh(y_ref[...])
    s5 = z * jax.nn.sigmoid(jnp.dot(z.astype(BF16), wglu_ref[...], preferred_element_type=F32) + bglu_ref[...])
    mix = (lax.dot_general(att_ref[...], woa_ref[...], _TN, preferred_element_type=F32)
           + jnp.dot(dn.astype(BF16), wob_ref[...], preferred_element_type=F32)
           + jnp.dot(s5.astype(BF16), woc_ref[...], preferred_element_type=F32))
    x = x_ref[...] + g1_ref[...] * mix
    xo_ref[...] = x
    h = _rms_rows(x) * n2_ref[...] * (1.0 + sc2_ref[...]) + sh2_ref[...]
    h_ref[...] = h.astype(BF16)
    comb_ref[...] = _route(_dot_f32(h, wr_ref[...]) + br_ref[...])


def _mix_out(att_t, o_f, o_b, gate, y, x, mod, lw, tm):
    t = x.shape[0]
    row = lambda w: pl.BlockSpec((tm, w), lambda i: (i, 0))
    vec = lambda a: _const_spec(a.shape)
    consts = [mod["g1"], mod["sh2"], mod["sc2"], lw["norm2_w"], lw["dnw"], lw["bglu"], lw["wglu"], lw["wo_a"],
              lw["wo_b"], lw["wo_c"], lw["wr"], lw["br"]]
    sds = jax.ShapeDtypeStruct
    return pl.pallas_call(
        _out_kernel,
        grid=(t // tm,),
        in_specs=[pl.BlockSpec((ATTN_WIDTH, tm), lambda i: (0, i)), row(DN_WIDTH), row(DN_WIDTH), row(DN_WIDTH),
                  row(S5_WIDTH), row(D_MODEL)] + [vec(a) for a in consts],
        out_specs=[row(D_MODEL), row(D_MODEL), row(LANES)],
        out_shape=[sds((t, D_MODEL), F32), sds((t, D_MODEL), BF16), sds((t, LANES), F32)],
        compiler_params=_cparams("parallel"),
        name="mix_out",
    )(att_t, o_f, o_b, gate, y, x, *consts)


def _moe_kernel(h_ref, comb_ref, x_ref, g2_ref, wg_ref, wu_ref, wd_ref, o_ref, acc_ref):
    e = pl.program_id(1)

    @pl.when(e == 0)
    def _():
        acc_ref[...] = jnp.zeros(acc_ref.shape, F32)

    h = h_ref[...]
    mid = _silu(jnp.dot(h, wg_ref[...], preferred_element_type=F32)) * jnp.dot(h, wu_ref[...], preferred_element_type=F32)
    lane = lax.broadcasted_iota(jnp.int32, (1, LANES), 1)
    cw = jnp.sum(jnp.where(lane == e + ROUTER_E0, comb_ref[...], 0.0), axis=-1, keepdims=True)
    acc_ref[...] += cw * jnp.dot(mid.astype(BF16), wd_ref[...], preferred_element_type=F32)

    @pl.when(e == MOE_EXPERTS - 1)
    def _():
        o_ref[...] = x_ref[...] + g2_ref[...] * acc_ref[...]


def _moe(h, comb, x, g2, lw, tm):
    t = x.shape[0]
    row = lambda w: pl.BlockSpec((tm, w), lambda i, e: (i, 0))
    return pl.pallas_call(
        _moe_kernel,
        grid=(t // tm, MOE_EXPERTS),
        in_specs=[row(D_MODEL), row(LANES), row(D_MODEL), _const_spec(g2.shape),
                  pl.BlockSpec((None, D_MODEL, MOE_HIDDEN), lambda i, e: (e, 0, 0)),
                  pl.BlockSpec((None, D_MODEL, MOE_HIDDEN), lambda i, e: (e, 0, 0)),
                  pl.BlockSpec((None, MOE_HIDDEN, D_MODEL), lambda i, e: (e, 0, 0))],
        out_specs=row(D_MODEL),
        out_shape=jax.ShapeDtypeStruct((t, D_MODEL), F32),
        scratch_shapes=[pltpu.VMEM((tm, D_MODEL), F32)],
        compiler_params=_cparams("parallel", "arbitrary"),
        name="moe",
    )(h, comb, x, g2, lw["w_gate"], lw["w_up"], lw["w_down"])


def _rope_tables(n_tokens):
    t = jnp.arange(n_tokens)
    row = (t // GRID_W).astype(F32)
    col = (t % GRID_W).astype(F32)
    half = HEAD_DIM // 2
    inv = ROPE_THETA ** (-jnp.arange(0, half, 2, dtype=F32) / half)
    ang_r = row[:, None] * inv[None, :]
    ang_c = col[:, None] * inv[None, :]
    ang = jnp.concatenate([ang_r, ang_r, ang_c, ang_c], axis=-1)
    cos, sin = jnp.cos(ang), jnp.sin(ang)
    return cos, sin, cos.T, sin.T


def _layer_weights(l, p):
    o = [0]
    for s in IN_SPLITS:
        o.append(o[-1] + s)
    w_in = p["w_in"][l]
    seg = lambda i: w_in[:, o[i]:o[i + 1]]
    ba = jnp.concatenate([seg(5), seg(6)], axis=1)
    lanes16 = lambda v: jnp.zeros((LANES,), F32).at[N_GATES:2 * N_GATES].set(v.reshape(N_GATES).astype(F32))
    alog, dtb = lanes16(p["dn_a_log"][l]), lanes16(p["dn_dt_bias"][l])
    w_out = p["w_out"][l]
    wr = jnp.zeros((D_MODEL, LANES), F32)
    wr = wr.at[:, :MOE_GROUPS].set(p["router_g_w"][l]).at[:, ROUTER_E0:ROUTER_E0 + MOE_EXPERTS].set(p["router_e_w"][l])
    br = jnp.zeros((1, LANES), F32)
    br = br.at[0, :MOE_GROUPS].set(p["router_g_b"][l]).at[0, ROUTER_E0:ROUTER_E0 + MOE_EXPERTS].set(p["router_e_b"][l])
    return dict(
        norm1_w=p["norm1_w"][l][None], norm2_w=p["norm2_w"][l][None],
        wqt=seg(0).T.astype(BF16), wk=seg(1).astype(BF16), wvt=seg(2).T.astype(BF16), wdn=seg(3).astype(BF16),
        wg=seg(4).astype(BF16), wba=jnp.pad(ba, ((0, 0), (0, LANES - 2 * N_GATES))).astype(BF16),
        wbat=ba.T.astype(BF16), ws5=seg(7).astype(BF16),
        qn_col=p["q_norm_w"][l][:, None], kn_row=p["k_norm_w"][l][None],
        conv_w=jnp.pad(p["dn_conv_w"][l], ((0, SUBLANES - DN_CONV), (0, 0))),
        alog_row=alog[None], dtb_row=dtb[None], alog_col=alog[:2 * N_GATES, None], dtb_col=dtb[:2 * N_GATES, None],
        dnw=p["dn_out_norm_w"][l][None], wglu=p["s5_w_glu"][l].astype(BF16), bglu=p["s5_b_glu"][l][None],
        wo_a=w_out[:ATTN_WIDTH].astype(BF16), wo_b=w_out[ATTN_WIDTH:ATTN_WIDTH + DN_WIDTH].astype(BF16),
        wo_c=w_out[ATTN_WIDTH + DN_WIDTH:].astype(BF16), wr=wr, br=br,
        w_gate=p["w_gate"][l].astype(BF16), w_up=p["w_up"][l].astype(BF16), w_down=p["w_down"][l].astype(BF16),
        s5=_s5_matrices(p["s5_a_re"][l], p["s5_a_im"][l], p["s5_log_dt"][l], p["s5_b_re"][l], p["s5_b_im"][l],
                        p["s5_c_re"][l], p["s5_c_im"][l], p["s5_d"][l]),
    )


TM_CTX = 256
TM_IN = 256
TM_SEQ = 512


def _mixers(x, mod, lw, tabs, rope, s_dn, s_s5, tm_in, tm):
    qt, k, vt, dn, gate, ba, bat, u = _in_proj(x, mod, lw, tabs, rope, tm_in)
    q_dn, k_dn, v_dn, gcol, grow = _gdn_prep(dn, ba, bat, lw, tm)
    o_f, o_b, s_dn = _gdn_scan(q_dn, k_dn, v_dn, gcol, grow, s_dn, tm)
    y, s_s5 = _s5_mixer(u, lw["s5"], s_s5)
    return dict(qt=qt, k=k, vt=vt, gate=gate, o_f=o_f, o_b=o_b, y=y), s_dn, s_s5


def kernel(x, c, ctx, c_ctx, w_ada, b_ada, norm1_w, norm2_w, w_in, q_norm_w, k_norm_w, dn_conv_w, dn_a_log, dn_dt_bias,
           dn_out_norm_w, s5_a_re, s5_a_im, s5_log_dt, s5_b_re, s5_b_im, s5_c_re, s5_c_im, s5_d, s5_w_glu, s5_b_glu,
           w_out, router_g_w, router_g_b, router_e_w, router_e_b, w_gate, w_up, w_down):
    p = dict(norm1_w=norm1_w, norm2_w=norm2_w, w_in=w_in, q_norm_w=q_norm_w, k_norm_w=k_norm_w, dn_conv_w=dn_conv_w,
             dn_a_log=dn_a_log, dn_dt_bias=dn_dt_bias, dn_out_norm_w=dn_out_norm_w, s5_a_re=s5_a_re, s5_a_im=s5_a_im,
             s5_log_dt=s5_log_dt, s5_b_re=s5_b_re, s5_b_im=s5_b_im, s5_c_re=s5_c_re, s5_c_im=s5_c_im, s5_d=s5_d,
             s5_w_glu=s5_w_glu, s5_b_glu=s5_b_glu, w_out=w_out, router_g_w=router_g_w, router_g_b=router_g_b,
             router_e_w=router_e_w, router_e_b=router_e_b, w_gate=w_gate, w_up=w_up, w_down=w_down)
    assert x.shape[0] == 1 and ctx.shape[0] == 1
    xl, xc = x[0], ctx[0]
    n_ctx = xc.shape[0]
    cvec = jnp.zeros((SUBLANES, D_MODEL), F32).at[0].set(c[0]).at[1].set(c_ctx)
    mods = _adaln_mod(cvec, w_ada, b_ada)
    tabs = _rope_tables(xl.shape[0])
    tabs_c = (tabs[0][:n_ctx], tabs[1][:n_ctx], tabs[2][:, :n_ctx], tabs[3][:, :n_ctx])
    names = ("sh1", "sc1", "g1", "sh2", "sc2", "g2")
    for l in range(DEPTH):
        lw = _layer_weights(l, p)
        mod_l = {n: mods[l, 0:1, i * D_MODEL:(i + 1) * D_MODEL] for i, n in enumerate(names)}
        mod_c = {n: mods[l, 1:2, i * D_MODEL:(i + 1) * D_MODEL] for i, n in enumerate(names)}
        s_dn0 = jnp.zeros((N_GATES, HEAD_DIM, HEAD_DIM), F32)
        s_s50 = jnp.zeros((2, S5_GROUPS * S5_HW), F32)
        mc, s_dn, s_s5 = _mixers(xc, mod_c, lw, tabs_c, False, s_dn0, s_s50, TM_CTX, TM_CTX)
        ml, _, _ = _mixers(xl, mod_l, lw, tabs, True, s_dn, s_s5, TM_IN, TM_SEQ)
        att_l = _attention(ml["qt"], jnp.concatenate([mc["k"], ml["k"]], axis=0),
                           jnp.concatenate([mc["vt"], ml["vt"]], axis=0))
        xl, h2, comb = _mix_out(att_l, ml["o_f"], ml["o_b"], ml["gate"], ml["y"], xl, mod_l, lw, TM_SEQ)
        xl = _moe(h2, comb, xl, mod_l["g2"], lw, TM_SEQ)
        if l < DEPTH - 1:
            att_c = _attention(mc["qt"], mc["k"], mc["vt"])
            xc, h2, comb = _mix_out(att_c, mc["o_f"], mc["o_b"], mc["gate"], mc["y"], xc, mod_c, lw, TM_CTX)
            xc = _moe(h2, comb, xc, mod_c["g2"], lw, TM_CTX)
    return xl[None]
```

```python
import functools
import math

import jax
import jax.numpy as jnp
from jax import lax
from jax.experimental import pallas as pl
from jax.experimental.pallas import tpu as pltpu

F32 = jnp.float32
BF16 = jnp.bfloat16

D_MODEL = 2048
DEPTH = 2
GRID_W = 64
N_DIR = 2
HEAD_DIM = 128
ATTN_WIDTH = D_MODEL // 2
ATTN_Q_HEADS = ATTN_WIDTH // HEAD_DIM
ATTN_KV_HEADS = ATTN_Q_HEADS // 4
KV_WIDTH = ATTN_KV_HEADS * HEAD_DIM
ROPE_THETA = 10000.0
DN_WIDTH = D_MODEL // 4
DN_HEADS = DN_WIDTH // HEAD_DIM
DN_CONV = 5
DN_CHUNK = 64
S5_WIDTH = D_MODEL // 4
S5_GROUP_CH = 16
S5_GROUPS = S5_WIDTH // S5_GROUP_CH
S5_STATE = 64
MIX_WIDTH = ATTN_WIDTH + DN_WIDTH + S5_WIDTH
IN_SPLITS = (ATTN_WIDTH, KV_WIDTH, KV_WIDTH, 3 * DN_WIDTH, DN_WIDTH, N_DIR * DN_HEADS, N_DIR * DN_HEADS, S5_WIDTH)
MOE_GROUPS = 4
MOE_PER_GROUP = 4
MOE_EXPERTS = MOE_GROUPS * MOE_PER_GROUP
MOE_HIDDEN = D_MODEL // 4
EPS = 1e-6

LANES = 128
SUBLANES = 8
VMEM_LIMIT_BYTES = 56 * 1024 * 1024

N_GATES = N_DIR * DN_HEADS
S5_CHUNK = 16
S5_CW = S5_CHUNK * S5_GROUP_CH


def _cparams(*sem):
    return pltpu.CompilerParams(dimension_semantics=sem, vmem_limit_bytes=VMEM_LIMIT_BYTES)


def _const_spec(shape):
    nd = len(shape)
    return pl.BlockSpec(shape, lambda *_: (0,) * nd, pipeline_mode=pl.Buffered(1))


def _silu(x):
    return x * jax.nn.sigmoid(x)


MOD_TN = 1024


def _mod_kernel(cv_ref, w_ref, b_ref, o_ref):
    s = _silu(cv_ref[...]).astype(BF16)
    o_ref[...] = jnp.dot(s, w_ref[...].astype(BF16), preferred_element_type=F32) + b_ref[...]


def _adaln_mod(cvec, w_ada, b_ada):
    n6 = w_ada.shape[-1]
    return pl.pallas_call(
        _mod_kernel,
        grid=(DEPTH, n6 // MOD_TN),
        in_specs=[
            pl.BlockSpec((SUBLANES, D_MODEL), lambda l, j: (0, 0)),
            pl.BlockSpec((None, D_MODEL, MOD_TN), lambda l, j: (l, 0, j)),
            pl.BlockSpec((None, 1, MOD_TN), lambda l, j: (l, 0, j)),
        ],
        out_specs=pl.BlockSpec((None, SUBLANES, MOD_TN), lambda l, j: (l, 0, j)),
        out_shape=jax.ShapeDtypeStruct((DEPTH, SUBLANES, n6), F32),
        compiler_params=_cparams("parallel", "parallel"),
        name="adaln_mod",
    )(cvec, w_ada, b_ada.reshape(DEPTH, 1, n6))


def _rms_rows(x):
    return x * lax.rsqrt(jnp.mean(x * x, axis=-1, keepdims=True) + EPS)


def _in_kernel(rope, x_ref, nw_ref, sh_ref, sc_ref, wqt_ref, wk_ref, wvt_ref, wdn_ref, wg_ref, wba_ref, wbat_ref,
               ws5_ref, qn_ref, kn_ref, cos_ref, sin_ref, cost_ref, sint_ref,
               qt_ref, k_ref, vt_ref, dn_ref, g_ref, ba_ref, bat_ref, u_ref):
    x = x_ref[...]
    h = (_rms_rows(x) * nw_ref[...] * (1.0 + sc_ref[...]) + sh_ref[...]).astype(BF16)
    nt = (((1,), (1,)), ((), ()))

    qt = lax.dot_general(wqt_ref[...], h, nt, preferred_element_type=F32)
    q_scale = HEAD_DIM ** -0.5 * math.log2(math.e)
    for hd in range(ATTN_Q_HEADS):
        qh = qt[hd * HEAD_DIM:(hd + 1) * HEAD_DIM, :]
        qh = qh * lax.rsqrt(jnp.mean(qh * qh, axis=0, keepdims=True) + EPS) * qn_ref[...]
        if rope:
            a, b, c, d = (qh[i * 32:(i + 1) * 32, :] for i in range(4))
            rot = jnp.concatenate([-b, a, -d, c], axis=0)
            qh = qh * cost_ref[...] + rot * sint_ref[...]
        qt_ref[hd * HEAD_DIM:(hd + 1) * HEAD_DIM, :] = (qh * q_scale).astype(BF16)

    k = jnp.dot(h, wk_ref[...], preferred_element_type=F32)
    lane = lax.broadcasted_iota(jnp.int32, (1, HEAD_DIM), 1)
    first_half = (lane % 64) < 32
    for hd in range(ATTN_KV_HEADS):
        kh = _rms_rows(k[:, hd * HEAD_DIM:(hd + 1) * HEAD_DIM]) * kn_ref[...]
        if rope:
            rot = jnp.where(first_half, -pltpu.roll(kh, 96, 1), pltpu.roll(kh, 32, 1))
            kh = kh * cos_ref[...] + rot * sin_ref[...]
        k_ref[:, hd * HEAD_DIM:(hd + 1) * HEAD_DIM] = kh.astype(BF16)

    vt_ref[...] = lax.dot_general(wvt_ref[...], h, nt, preferred_element_type=F32).astype(BF16)
    dn_ref[...] = jnp.dot(h, wdn_ref[...], preferred_element_type=F32)
    g_ref[...] = jnp.dot(h, wg_ref[...], preferred_element_type=F32)
    ba_ref[...] = jnp.dot(h, wba_ref[...], preferred_element_type=F32)
    bat_ref[...] = lax.dot_general(wbat_ref[...], h, nt, preferred_element_type=F32)
    u_ref[...] = jnp.dot(h, ws5_ref[...], preferred_element_type=F32).astype(BF16)


def _in_proj(x, mod, lw, rope_tabs, rope, tm):
    t = x.shape[0]
    cos, sin, cost, sint = rope_tabs
    row = lambda w: pl.BlockSpec((tm, w), lambda i: (i, 0))
    col = lambda w: pl.BlockSpec((w, tm), lambda i: (0, i))
    vec = lambda a: _const_spec(a.shape)
    ws = [lw["wqt"], lw["wk"], lw["wvt"], lw["wdn"], lw["wg"], lw["wba"], lw["wbat"], lw["ws5"]]
    in_specs = ([row(D_MODEL), vec(lw["norm1_w"]), vec(mod["sh1"]), vec(mod["sc1"])] + [vec(w) for w in ws]
                + [vec(lw["qn_col"]), vec(lw["kn_row"]), row(HEAD_DIM), row(HEAD_DIM), col(HEAD_DIM), col(HEAD_DIM)])
    out_specs = [col(ATTN_WIDTH), row(KV_WIDTH), pl.BlockSpec((None, KV_WIDTH, tm), lambda i: (i, 0, 0)), row(3 * DN_WIDTH), row(DN_WIDTH), row(LANES),
                 col(2 * N_GATES), row(S5_WIDTH)]
    sds = jax.ShapeDtypeStruct
    out_shape = [sds((ATTN_WIDTH, t), BF16), sds((t, KV_WIDTH), BF16), sds((t // tm, KV_WIDTH, tm), BF16),
                 sds((t, 3 * DN_WIDTH), F32), sds((t, DN_WIDTH), F32), sds((t, LANES), F32),
                 sds((2 * N_GATES, t), F32), sds((t, S5_WIDTH), BF16)]
    return pl.pallas_call(
        functools.partial(_in_kernel, rope),
        grid=(t // tm,),
        in_specs=in_specs, out_specs=out_specs, out_shape=out_shape,
        compiler_params=_cparams("parallel"),
        name="in_proj",
    )(x, lw["norm1_w"], mod["sh1"], mod["sc1"], *ws, lw["qn_col"], lw["kn_row"], cos, sin, cost, sint)


ATT_TQ = 256
ATT_TK = 256
GQA = ATTN_Q_HEADS // ATTN_KV_HEADS


ATT_ONES = 16


def _attn_kernel(nj, qt_ref, k_ref, vt_ref, ot_ref, sa_ref, sb_ref, m_ref, acc_ref):
    tq = qt_ref.shape[1]
    q = jnp.concatenate([qt_ref[h * HEAD_DIM:(h + 1) * HEAD_DIM, :] for h in range(GQA)], axis=1)
    ones = jnp.ones((ATT_ONES, ATT_TK), BF16)

    def scores(j, dst_ref):
        kb = k_ref[pl.ds(pl.multiple_of(j * ATT_TK, ATT_TK), ATT_TK), :]
        dst_ref[...] = jnp.dot(kb, q, preferred_element_type=F32)

    def consume(j, src_ref):
        vt = jnp.concatenate([vt_ref[j], ones], axis=0)
        for h in range(GQA):
            cols = slice(h * tq, (h + 1) * tq)
            s = src_ref[:, cols]
            m_old = m_ref[:, cols]
            m_new = jnp.maximum(m_old, jnp.max(s, axis=0, keepdims=True))
            p = jnp.exp2(s - m_new).astype(BF16)
            acc_ref[:, cols] = (jnp.exp2(m_old - m_new) * acc_ref[:, cols]
                                + jnp.dot(vt, p, preferred_element_type=F32))
            m_ref[:, cols] = m_new

    m_ref[...] = jnp.full(m_ref.shape, -jnp.inf, F32)
    acc_ref[...] = jnp.zeros(acc_ref.shape, F32)
    scores(0, sa_ref)

    def pair(i, carry):
        j = 2 * i
        scores(j + 1, sb_ref)
        consume(j, sa_ref)
        scores(jnp.minimum(j + 2, nj - 1), sa_ref)
        consume(j + 1, sb_ref)
        return carry

    lax.fori_loop(0, nj // 2, pair, 0)
    if nj % 2:
        consume(nj - 1, sa_ref)
    acc = acc_ref[...]
    o = acc[:HEAD_DIM] * (1.0 / acc[HEAD_DIM:HEAD_DIM + 1])
    for h in range(GQA):
        ot_ref[h * HEAD_DIM:(h + 1) * HEAD_DIM, :] = o[:, h * tq:(h + 1) * tq].astype(BF16)


def _attention(qt, k_all, vt_all):
    t = qt.shape[1]
    tk_total = k_all.shape[0]
    nj = tk_total // ATT_TK
    tq = min(ATT_TQ, t)
    return pl.pallas_call(
        functools.partial(_attn_kernel, nj),
        grid=(ATTN_KV_HEADS, t // tq),
        in_specs=[
            pl.BlockSpec((GQA * HEAD_DIM, tq), lambda g, i: (g, i)),
            pl.BlockSpec((tk_total, HEAD_DIM), lambda g, i: (0, g)),
            pl.BlockSpec((nj, HEAD_DIM, ATT_TK), lambda g, i: (0, g, 0)),
        ],
        out_specs=pl.BlockSpec((GQA * HEAD_DIM, tq), lambda g, i: (g, i)),
        out_shape=jax.ShapeDtypeStruct((ATTN_WIDTH, t), BF16),
        scratch_shapes=[pltpu.VMEM((ATT_TK, GQA * tq), F32), pltpu.VMEM((ATT_TK, GQA * tq), F32),
                        pltpu.VMEM((1, GQA * tq), F32), pltpu.VMEM((HEAD_DIM + ATT_ONES, GQA * tq), F32)],
        compiler_params=_cparams("parallel", "parallel"),
        name="attention",
    )(qt, k_all, vt_all)


def _softplus(x):
    return jnp.maximum(x, 0.0) + jnp.log(1.0 + jnp.exp(-jnp.abs(x)))


def _dot_f32(a, b):
    return jnp.dot(a, b, precision=lax.Precision.HIGHEST, preferred_element_type=F32)


def _gdn_prep_kernel(nblk, cur_ref, prev_ref, next_ref, cw_ref, ba_ref, bat_ref, alog_r, dtb_r, alog_c, dtb_c,
                     q_ref, k_ref, v_ref, gcol_ref, grow_ref, xx_ref):
    i = pl.program_id(0)
    tm = cur_ref.shape[0]
    pad = (DN_CONV - 1) // 2
    xx_ref[0:SUBLANES, :] = jnp.where(i > 0, prev_ref[...], 0.0)
    xx_ref[SUBLANES:SUBLANES + tm, :] = cur_ref[...]
    xx_ref[SUBLANES + tm:2 * SUBLANES + tm, :] = jnp.where(i < nblk - 1, next_ref[...], 0.0)
    acc = None
    for j in range(DN_CONV):
        term = xx_ref[SUBLANES - pad + j:SUBLANES - pad + j + tm, :] * cw_ref[j:j + 1, :]
        acc = term if acc is None else acc + term
    y = _silu(acc)
    for hd in range(DN_HEADS):
        sl = slice(hd * HEAD_DIM, (hd + 1) * HEAD_DIM)
        qh = y[:, sl]
        q_ref[:, sl] = qh * (lax.rsqrt(jnp.sum(qh * qh, axis=-1, keepdims=True) + EPS) * HEAD_DIM ** -0.5)
        kh = y[:, DN_WIDTH + hd * HEAD_DIM:DN_WIDTH + (hd + 1) * HEAD_DIM]
        k_ref[:, sl] = kh * lax.rsqrt(jnp.sum(kh * kh, axis=-1, keepdims=True) + EPS)
    v_ref[...] = y[:, 2 * DN_WIDTH:]

    r = lax.broadcasted_iota(jnp.int32, (tm, tm), 0)
    c = lax.broadcasted_iota(jnp.int32, (tm, tm), 1)
    same = (r // DN_CHUNK) == (c // DN_CHUNK)
    tri_le = jnp.where(same & (c <= r), 1.0, 0.0)
    tri_ge = jnp.where(same & (c >= r), 1.0, 0.0)

    ba = ba_ref[...]
    lane = lax.broadcasted_iota(jnp.int32, (1, LANES), 1)
    g = -jnp.exp(alog_r[...]) * _softplus(ba + dtb_r[...])
    gc = jnp.where(lane < N_GATES + DN_HEADS, _dot_f32(tri_le, g), _dot_f32(tri_ge, g))
    gcol_ref[...] = jnp.where(lane < N_GATES, jax.nn.sigmoid(ba), gc)

    gt = -jnp.exp(alog_c[...]) * _softplus(bat_ref[...] + dtb_c[...])
    row = lax.broadcasted_iota(jnp.int32, (2 * N_GATES, 1), 0)
    gct = jnp.where(row < N_GATES + DN_HEADS, _dot_f32(gt, tri_ge), _dot_f32(gt, tri_le))
    for ch in range(tm // DN_CHUNK):
        grow_ref[ch] = gct[N_GATES:, ch * DN_CHUNK:(ch + 1) * DN_CHUNK]


def _gdn_prep(dn, ba, bat, lw, tm):
    t = dn.shape[0]
    nblk = t // tm
    nsub = tm // SUBLANES
    w3 = 3 * DN_WIDTH
    row = lambda w: pl.BlockSpec((tm, w), lambda i: (i, 0))
    vec = lambda a: _const_spec(a.shape)
    sds = jax.ShapeDtypeStruct
    return pl.pallas_call(
        functools.partial(_gdn_prep_kernel, nblk),
        grid=(nblk,),
        in_specs=[row(w3),
                  pl.BlockSpec((SUBLANES, w3), lambda i: (jnp.maximum(i * nsub - 1, 0), 0)),
                  pl.BlockSpec((SUBLANES, w3), lambda i: (jnp.minimum((i + 1) * nsub, t // SUBLANES - 1), 0)),
                  vec(lw["conv_w"]), row(LANES), pl.BlockSpec((2 * N_GATES, tm), lambda i: (0, i)),
                  vec(lw["alog_row"]), vec(lw["dtb_row"]), vec(lw["alog_col"]), vec(lw["dtb_col"])],
        out_specs=[row(DN_WIDTH), row(DN_WIDTH), row(DN_WIDTH), row(LANES),
                   pl.BlockSpec((tm // DN_CHUNK, N_GATES, DN_CHUNK), lambda i: (i, 0, 0))],
        out_shape=[sds((t, DN_WIDTH), F32), sds((t, DN_WIDTH), F32), sds((t, DN_WIDTH), F32), sds((t, LANES), F32),
                   sds((t // DN_CHUNK, N_GATES, DN_CHUNK), F32)],
        scratch_shapes=[pltpu.VMEM((tm + 2 * SUBLANES, w3), F32)],
        compiler_params=_cparams("parallel"),
        name="gdn_prep",
    )(dn, dn, dn, lw["conv_w"], ba, bat, lw["alog_row"], lw["dtb_row"], lw["alog_col"], lw["dtb_col"])


def _dot_bf(a, b, dims=(((1,), (0,)), ((), ()))):
    return lax.dot_general(a.astype(BF16), b.astype(BF16), dims, preferred_element_type=F32)


_NT = (((1,), (1,)), ((), ()))
_TN = (((0,), (0,)), ((), ()))
TRI_BASE = 8


def _gdn_local(chains):
    c = DN_CHUNK
    ri = lax.broadcasted_iota(jnp.int32, (c, c), 0)
    ci = lax.broadcasted_iota(jnp.int32, (c, c), 1)
    blk = lambda b: (ri // b) == (ci // b)
    eye = jnp.where(ri == ci, 1.0, 0.0)
    n = range(len(chains))
    qs, ks, vs, betas, gcs, grs, fwds = zip(*chains)
    incl = [(ri >= ci) if f else (ri <= ci) for f in fwds]
    strict = [(ri > ci) if f else (ri < ci) for f in fwds]
    decay = [jnp.where(incl[i], jnp.exp(jnp.where(incl[i], gcs[i] - grs[i], 0.0)), 0.0) for i in n]
    glast = [grs[i][:, c - 1:c] if fwds[i] else grs[i][:, 0:1] for i in n]
    eg = [jnp.exp(g) for g in gcs]
    kb = [ks[i] * betas[i] for i in n]
    kbf = [k.astype(BF16) for k in ks]
    a = [jnp.where(strict[i], _dot_bf(kb[i], kbf[i], _NT) * decay[i], 0.0) for i in n]
    qk = [(_dot_bf(qs[i], kbf[i], _NT) * decay[i]).astype(BF16) for i in n]
    p = [jnp.where(blk(TRI_BASE), a[i], 0.0) for i in n]
    x = [eye - p[i] for i in n]
    for _ in range(int(math.log2(TRI_BASE)) - 1):
        p = [_dot_bf(p[i], p[i]) for i in n]
        x = [x[i] + _dot_bf(x[i], p[i]) for i in n]
    b = 2 * TRI_BASE
    while b <= c:
        off = blk(b) & jnp.logical_not(blk(b // 2))
        lx = [_dot_bf(jnp.where(off, a[i], 0.0), x[i]) for i in n]
        x = [x[i] - _dot_bf(x[i], lx[i]) for i in n]
        b *= 2
    sol = [_dot_bf(x[i], jnp.concatenate([vs[i] * betas[i], kb[i] * eg[i]], axis=1)) for i in n]
    return [(sol[i][:, :HEAD_DIM],
             jnp.concatenate([sol[i][:, HEAD_DIM:], qs[i] * eg[i]], axis=0).astype(BF16),
             (ks[i] * jnp.exp(glast[i] - gcs[i])).astype(BF16),
             qk[i]) for i in n]


GDN_LOCAL_CHUNKS = 2


def _gdn_kernel(nblk, qf, kf, vf, gcf, grf, qb, kb, vb, gcb, grb, s0_ref, of_ref, ob_ref, sout_ref,
                s_ref, u_ref, wq_ref, kd_ref, qk_ref):
    i = pl.program_id(0)
    c = DN_CHUNK

    @pl.when(i == 0)
    def _():
        s_ref[...] = s0_ref[...]

    nb = qf.shape[0] // c
    dirs = ((qf, kf, vf, gcf, grf), (qb, kb, vb, gcb, grb))

    def local_body(jp, carry):
        work = []
        for cc in range(GDN_LOCAL_CHUNKS):
            ch = jp * GDN_LOCAL_CHUNKS + cc
            rows = pl.ds(pl.multiple_of(ch * c, c), c)
            for d, (q_ref, k_ref, v_ref, gc_ref, gr_ref) in enumerate(dirs):
                gcol = gc_ref[rows, :]
                grow = gr_ref[ch]
                for h in range(DN_HEADS):
                    gi = d * DN_HEADS + h
                    sl = slice(h * HEAD_DIM, (h + 1) * HEAD_DIM)
                    work.append((ch, gi, (q_ref[rows, sl], k_ref[rows, sl], v_ref[rows, sl], gcol[:, gi:gi + 1],
                                          gcol[:, N_GATES + gi:N_GATES + gi + 1], grow[gi:gi + 1, :], d == 0)))
        done = _gdn_local([args for _, _, args in work])
        for (ch, gi, _), (u, wq, kd, qk) in zip(work, done):
            u_ref[ch, gi] = u
            wq_ref[ch, gi] = wq
            kd_ref[ch, gi] = kd
            qk_ref[ch, gi] = qk
        return carry

    lax.fori_loop(0, nb // GDN_LOCAL_CHUNKS, local_body, 0)

    def seq_body(j, carry):
        work = []
        for d, (gr_ref, o_ref) in enumerate(((grf, of_ref), (grb, ob_ref))):
            jj = j if d == 0 else nb - 1 - j
            grow = gr_ref[jj]
            for h in range(DN_HEADS):
                gi = d * DN_HEADS + h
                glast = grow[gi:gi + 1, c - 1:c] if d == 0 else grow[gi:gi + 1, 0:1]
                work.append((o_ref, jj, h, gi, s_ref[gi], u_ref[jj, gi], wq_ref[jj, gi], kd_ref[jj, gi], qk_ref[jj, gi],
                             jnp.exp(glast)))
        ws = [jnp.dot(w[6], w[4].astype(BF16), preferred_element_type=F32) for w in work]
        v_new = [(w[5] - ws_[:c]).astype(BF16) for w, ws_ in zip(work, ws)]
        o = [ws_[c:] + jnp.dot(w[8], vn, preferred_element_type=F32) for w, ws_, vn in zip(work, ws, v_new)]
        s_new = [w[4] * w[9] + lax.dot_general(w[7], vn, _TN, preferred_element_type=F32) for w, vn in zip(work, v_new)]
        for (o_ref, jj, h, gi, *_), o_, s_ in zip(work, o, s_new):
            o_ref[pl.ds(pl.multiple_of(jj * c, c), c), h * HEAD_DIM:(h + 1) * HEAD_DIM] = o_
            s_ref[gi] = s_
        return carry

    lax.fori_loop(0, nb, seq_body, 0)

    @pl.when(i == nblk - 1)
    def _():
        sout_ref[...] = s_ref[...]


def _gdn_scan(q, k, v, gcol, grow, s0, tm):
    t = q.shape[0]
    nblk = t // tm
    nch = tm // DN_CHUNK
    fwd = lambda w: pl.BlockSpec((tm, w), lambda i: (i, 0))
    bwd = lambda w: pl.BlockSpec((tm, w), lambda i: (nblk - 1 - i, 0))
    sds = jax.ShapeDtypeStruct
    s_shape = (N_GATES, HEAD_DIM, HEAD_DIM)
    return pl.pallas_call(
        functools.partial(_gdn_kernel, nblk),
        grid=(nblk,),
        in_specs=[fwd(DN_WIDTH), fwd(DN_WIDTH), fwd(DN_WIDTH), fwd(LANES),
                  pl.BlockSpec((nch, N_GATES, DN_CHUNK), lambda i: (i, 0, 0)),
                  bwd(DN_WIDTH), bwd(DN_WIDTH), bwd(DN_WIDTH), bwd(LANES),
                  pl.BlockSpec((nch, N_GATES, DN_CHUNK), lambda i: (nblk - 1 - i, 0, 0)),
                  _const_spec(s_shape)],
        out_specs=[fwd(DN_WIDTH), bwd(DN_WIDTH), pl.BlockSpec(s_shape, lambda i: (0, 0, 0))],
        out_shape=[sds((t, DN_WIDTH), F32), sds((t, DN_WIDTH), F32), sds(s_shape, F32)],
        scratch_shapes=[pltpu.VMEM(s_shape, F32),
                        pltpu.VMEM((nch, N_GATES, DN_CHUNK, HEAD_DIM), F32),
                        pltpu.VMEM((nch, N_GATES, 2 * DN_CHUNK, HEAD_DIM), BF16),
                        pltpu.VMEM((nch, N_GATES, DN_CHUNK, HEAD_DIM), BF16),
                        pltpu.VMEM((nch, N_GATES, DN_CHUNK, DN_CHUNK), BF16)],
        compiler_params=_cparams("arbitrary"),
        name="gdn_scan",
    )(q, k, v, gcol, grow, q, k, v, gcol, grow, s0)


S5_SW = 4 * S5_STATE


def _s5_matrices(a_re, a_im, log_dt, b_re, b_im, c_re, c_im, d):
    L, G, P, H = S5_CHUNK, S5_GROUPS, S5_STATE, S5_GROUP_CH
    hi = lax.Precision.HIGHEST
    a_re, a_im = a_re.astype(F32), a_im.astype(F32)
    dt = jnp.exp(log_dt.astype(F32))[..., None]
    n_re = jnp.exp(a_re * dt) * jnp.cos(a_im * dt) - 1.0
    n_im = jnp.exp(a_re * dt) * jnp.sin(a_im * dt)
    den = a_re * a_re + a_im * a_im
    co_re = ((n_re * a_re + n_im * a_im) / den)[..., None]
    co_im = ((n_im * a_re - n_re * a_im) / den)[..., None]
    br, bi = b_re.astype(F32)[None], b_im.astype(F32)[None]
    bb_re, bb_im = co_re * br - co_im * bi, co_re * bi + co_im * br
    tau = jnp.arange(L + 1, dtype=F32)[:, None, None, None]
    mag = jnp.exp(a_re[None] * dt[None] * tau)
    ang = a_im[None] * dt[None] * tau
    pw_re, pw_im = mag * jnp.cos(ang), mag * jnp.sin(ang)
    cr, ci = c_re.astype(F32), c_im.astype(F32)
    e_re = cr[None] * pw_re[:, :, :, None, :] - ci[None] * pw_im[:, :, :, None, :]
    e_im = cr[None] * pw_im[:, :, :, None, :] + ci[None] * pw_re[:, :, :, None, :]
    kk = (jnp.einsum('tdghp,dgpi->tdghi', e_re[:L], bb_re, precision=hi)
          - jnp.einsum('tdghp,dgpi->tdghi', e_im[:L], bb_im, precision=hi))
    s_i = jnp.arange(L)[:, None]
    t_i = jnp.arange(L)[None, :]
    lag = t_i - s_i
    kf = jnp.where((lag >= 0)[..., None, None, None], kk[jnp.clip(lag, 0, L - 1), 0], 0.0)
    kb = jnp.where((lag <= 0)[..., None, None, None], kk[jnp.clip(-lag, 0, L - 1), 1], 0.0)
    dskip = d.astype(F32).reshape(G, H)
    eye_t = (lag == 0).astype(F32)
    skip = eye_t[:, :, None, None, None] * (dskip[:, :, None] * jnp.eye(H, dtype=F32)[None])[None, None]
    toep = (kf + kb + skip).transpose(2, 0, 4, 1, 3).reshape(G, L * H, L * H)

    def loc(pr, pi, dr):
        w_re = pr[..., None] * bb_re[dr][None] - pi[..., None] * bb_im[dr][None]
        w_im = pr[..., None] * bb_im[dr][None] + pi[..., None] * bb_re[dr][None]
        f = lambda w: w.transpose(1, 0, 3, 2).reshape(G, L * H, P)
        return f(w_re), f(w_im)
    wf_re, wf_im = loc(pw_re[:L, 0][::-1], pw_im[:L, 0][::-1], 0)
    wb_re, wb_im = loc(pw_re[:L, 1], pw_im[:L, 1], 1)
    wcat = jnp.concatenate([toep, wf_re, wb_re, wf_im, wb_im], axis=-1).astype(BF16)

    carry = lambda e: e.transpose(1, 3, 0, 2).reshape(G, P, L * H)
    mf_re, mf_im = carry(e_re[1:, 0]), carry(-e_im[1:, 0])
    mb_re, mb_im = carry(e_re[1:, 1][::-1]), carry(-e_im[1:, 1][::-1])
    mcat = jnp.concatenate([mf_re, mb_re, mf_im, mb_im], axis=1).astype(BF16)
    al_re = jnp.concatenate([pw_re[L, 0], pw_re[L, 1]], axis=-1)
    al_im = jnp.concatenate([pw_im[L, 0], pw_im[L, 1]], axis=-1)
    al = jnp.stack([al_re.reshape(-1), al_im.reshape(-1)])
    return wcat, mcat, al


S5_HW = 2 * S5_STATE


def _s5_local_kernel(u_ref, w_ref, y_ref, xre_ref, xim_ref):
    r = jnp.dot(u_ref[...], w_ref[...], preferred_element_type=F32)
    y_ref[...] = r[:, :S5_CW]
    xre_ref[...] = r[:, S5_CW:S5_CW + S5_HW]
    xim_ref[...] = r[:, S5_CW + S5_HW:]


def _s5_local(ug, wcat):
    g, n, _ = ug.shape
    out = lambda w: pl.BlockSpec((n, w), lambda i: (0, i))
    sds = jax.ShapeDtypeStruct
    return pl.pallas_call(
        _s5_local_kernel,
        grid=(g,),
        in_specs=[pl.BlockSpec((None, n, S5_CW), lambda i: (i, 0, 0)),
                  pl.BlockSpec((None, S5_CW, S5_CW + S5_SW), lambda i: (i, 0, 0))],
        out_specs=[out(S5_CW), out(S5_HW), out(S5_HW)],
        out_shape=[sds((n, g * S5_CW), F32), sds((n, g * S5_HW), F32), sds((n, g * S5_HW), F32)],
        compiler_params=_cparams("parallel"),
        name="s5_local",
    )(ug, wcat)


def _s5_scan_kernel(nblk, xfr_ref, xfi_ref, xbr_ref, xbi_ref, al_ref, s0_ref,
                    cfr_ref, cfi_ref, cbr_ref, cbi_ref, sout_ref, s_ref):
    i = pl.program_id(0)

    @pl.when(i == 0)
    def _():
        s_ref[...] = s0_ref[...]

    nb = xfr_ref.shape[0]
    al_re, al_im = al_ref[0:1, :], al_ref[1:2, :]
    is_fwd = (lax.broadcasted_iota(jnp.int32, (1, al_ref.shape[1]), 1) % S5_HW) < S5_STATE

    def body(j, carry):
        re, im = carry
        rf, rb = pl.ds(j, 1), pl.ds(nb - 1 - j, 1)
        cfr_ref[rf, :] = re
        cfi_ref[rf, :] = im
        cbr_ref[rb, :] = re
        cbi_ref[rb, :] = im
        in_re = jnp.where(is_fwd, xfr_ref[rf, :], xbr_ref[rb, :])
        in_im = jnp.where(is_fwd, xfi_ref[rf, :], xbi_ref[rb, :])
        return al_re * re - al_im * im + in_re, al_re * im + al_im * re + in_im

    re, im = lax.fori_loop(0, nb, body, (s_ref[0:1, :], s_ref[1:2, :]))
    s_ref[0:1, :] = re
    s_ref[1:2, :] = im

    @pl.when(i == nblk - 1)
    def _():
        sout_ref[...] = s_ref[...]


def _s5_scan(x_re, x_im, al, s0, nb):
    n, w = x_re.shape
    nblk = n // nb
    fwd = pl.BlockSpec((nb, w), lambda i: (i, 0))
    bwd = pl.BlockSpec((nb, w), lambda i: (nblk - 1 - i, 0))
    sds = jax.ShapeDtypeStruct
    return pl.pallas_call(
        functools.partial(_s5_scan_kernel, nblk),
        grid=(nblk,),
        in_specs=[fwd, fwd, bwd, bwd, _const_spec(al.shape), _const_spec(s0.shape)],
        out_specs=[fwd, fwd, bwd, bwd, pl.BlockSpec(s0.shape, lambda i: (0, 0))],
        out_shape=[sds((n, w), F32)] * 4 + [sds(s0.shape, F32)],
        scratch_shapes=[pltpu.VMEM(s0.shape, F32)],
        compiler_params=_cparams("arbitrary"),
        name="s5_scan",
    )(x_re, x_im, x_re, x_im, al, s0)


def _s5_carry_kernel(y_ref, cfr_ref, cfi_ref, cbr_ref, cbi_ref, m_ref, o_ref):
    is_fwd = lax.broadcasted_iota(jnp.int32, (1, S5_HW), 1) < S5_STATE
    cin = jnp.concatenate([jnp.where(is_fwd, cfr_ref[...], cbr_ref[...]),
                           jnp.where(is_fwd, cfi_ref[...], cbi_ref[...])], axis=1).astype(BF16)
    o_ref[...] = y_ref[...] + jnp.dot(cin, m_ref[...], preferred_element_type=F32)


def _s5_carry(y, cins, mcat):
    n = y.shape[0]
    g = mcat.shape[0]
    blk = lambda w: pl.BlockSpec((n, w), lambda i: (0, i))
    return pl.pallas_call(
        _s5_carry_kernel,
        grid=(g,),
        in_specs=[blk(S5_CW)] + [blk(S5_HW)] * 4 + [pl.BlockSpec((None, S5_SW, S5_CW), lambda i: (i, 0, 0))],
        out_specs=blk(S5_CW),
        out_shape=jax.ShapeDtypeStruct(y.shape, F32),
        compiler_params=_cparams("parallel"),
        name="s5_carry",
    )(y, *cins, mcat)


def _s5_mixer(u, mats, s0):
    wcat, mcat, al = mats
    t = u.shape[0]
    n = t // S5_CHUNK
    ug = u.reshape(n, S5_CHUNK, S5_GROUPS, S5_GROUP_CH).transpose(2, 0, 1, 3).reshape(S5_GROUPS, n, S5_CW)
    y_loc, x_re, x_im = _s5_local(ug, wcat)
    *cins, s_fin = _s5_scan(x_re, x_im, al, s0, min(n, 128))
    y = _s5_carry(y_loc, cins, mcat)
    y = y.reshape(n, S5_GROUPS, S5_CHUNK, S5_GROUP_CH).transpose(0, 2, 1, 3).reshape(t, S5_WIDTH)
    return y, s_fin


ROUTER_E0 = MOE_GROUPS


def _gelu_tanh(x):
    return 0.5 * x * (1.0 + jnp.tanh(math.sqrt(2.0 / math.pi) * (x + 0.044715 * (x * x * x))))


def _route(logits):
    ninf = float("-inf")
    lane = lax.broadcasted_iota(jnp.int32, (1, LANES), 1)
    lanef = lane.astype(F32)
    first = lambda hit: jnp.min(jnp.where(hit, lanef, float(LANES)), axis=-1, keepdims=True)
    gl = jnp.where(lane < MOE_GROUPS, logits, ninf)
    gmax = jnp.max(gl, axis=-1, keepdims=True)
    pg_top = 1.0 / jnp.sum(jnp.exp(gl - gmax), axis=-1, keepdims=True)
    base = ROUTER_E0 + MOE_PER_GROUP * first(gl == gmax)
    el = jnp.where((lanef >= base) & (lanef < base + MOE_PER_GROUP), logits, ninf)
    emax = jnp.max(el, axis=-1, keepdims=True)
    esum = jnp.sum(jnp.exp(el - emax), axis=-1, keepdims=True)
    i1 = first(el == emax)
    el2 = jnp.where(lanef == i1, ninf, el)
    emax2 = jnp.max(el2, axis=-1, keepdims=True)
    i2 = first(el2 == emax2)
    p1 = 1.0 / esum
    p2 = jnp.exp(emax2 - emax) / esum
    w1 = p1 / (p1 + p2)
    w2 = p2 / (p1 + p2)
    return pg_top * (jnp.where(lanef == i1, w1, 0.0) + jnp.where(lanef == i2, w2, 0.0))


def _out_kernel(att_ref, of_ref, ob_ref, gate_ref, y_ref, x_ref, g1_ref, sh2_ref, sc2_ref, n2_ref, dnw_ref, bglu_ref,
                wglu_ref, woa_ref, wob_ref, woc_ref, wr_ref, br_ref, xo_ref, h_ref, comb_ref):
    o = of_ref[...] + ob_ref[...]
    gate = _silu(gate_ref[...])
    dn = jnp.concatenate(
        [_rms_rows(o[:, h * HEAD_DIM:(h + 1) * HEAD_DIM]) * dnw_ref[...] for h in range(DN_HEADS)], axis=1) * gate
    z = _gelu_tanh(y_ref[...])
    s5 = z * jax.nn.sigmoid(jnp.dot(z.astype(BF16), wglu_ref[...], preferred_element_type=F32) + bglu_ref[...])
    mix = (lax.dot_general(att_ref[...], woa_ref[...], _TN, preferred_element_type=F32)
           + jnp.dot(dn.astype(BF16), wob_ref[...], preferred_element_type=F32)
           + jnp.dot(s5.astype(BF16), woc_ref[...], preferred_element_type=F32))
    x = x_ref[...] + g1_ref[...] * mix
    xo_ref[...] = x
    h = _rms_rows(x) * n2_ref[...] * (1.0 + sc2_ref[...]) + sh2_ref[...]
    h_ref[...] = h.astype(BF16)
    comb_ref[...] = _route(_dot_f32(h, wr_ref[...]) + br_ref[...])


def _mix_out(att_t, o_f, o_b, gate, y, x, mod, lw, tm):
    t = x.shape[0]
    row = lambda w: pl.BlockSpec((tm, w), lambda i: (i, 0))
    vec = lambda a: _const_spec(a.shape)
    consts = [mod["g1"], mod["sh2"], mod["sc2"], lw["norm2_w"], lw["dnw"], lw["bglu"], lw["wglu"], lw["wo_a"],
              lw["wo_b"], lw["wo_c"], lw["wr"], lw["br"]]
    sds = jax.ShapeDtypeStruct
    return pl.pallas_call(
        _out_kernel,
        grid=(t // tm,),
        in_specs=[pl.BlockSpec((ATTN_WIDTH, tm), lambda i: (0, i)), row(DN_WIDTH), row(DN_WIDTH), row(DN_WIDTH),
                  row(S5_WIDTH), row(D_MODEL)] + [vec(a) for a in consts],
        out_specs=[row(D_MODEL), row(D_MODEL), row(LANES)],
        out_shape=[sds((t, D_MODEL), F32), sds((t, D_MODEL), BF16), sds((t, LANES), F32)],
        compiler_params=_cparams("parallel"),
        name="mix_out",
    )(att_t, o_f, o_b, gate, y, x, *consts)


def _moe_kernel(h_ref, comb_ref, x_ref, g2_ref, wg_ref, wu_ref, wd_ref, o_ref, acc_ref):
    e = pl.program_id(1)

    @pl.when(e == 0)
    def _():
        acc_ref[...] = jnp.zeros(acc_ref.shape, F32)

    h = h_ref[...]
    mid = _silu(jnp.dot(h, wg_ref[...], preferred_element_type=F32)) * jnp.dot(h, wu_ref[...], preferred_element_type=F32)
    lane = lax.broadcasted_iota(jnp.int32, (1, LANES), 1)
    cw = jnp.sum(jnp.where(lane == e + ROUTER_E0, comb_ref[...], 0.0), axis=-1, keepdims=True)
    acc_ref[...] += cw * jnp.dot(mid.astype(BF16), wd_ref[...], preferred_element_type=F32)

    @pl.when(e == MOE_EXPERTS - 1)
    def _():
        o_ref[...] = x_ref[...] + g2_ref[...] * acc_ref[...]


def _moe(h, comb, x, g2, lw, tm):
    t = x.shape[0]
    row = lambda w: pl.BlockSpec((tm, w), lambda i, e: (i, 0))
    return pl.pallas_call(
        _moe_kernel,
        grid=(t // tm, MOE_EXPERTS),
        in_specs=[row(D_MODEL), row(LANES), row(D_MODEL), _const_spec(g2.shape),
                  pl.BlockSpec((None, D_MODEL, MOE_HIDDEN), lambda i, e: (e, 0, 0)),
                  pl.BlockSpec((None, D_MODEL, MOE_HIDDEN), lambda i, e: (e, 0, 0)),
                  pl.BlockSpec((None, MOE_HIDDEN, D_MODEL), lambda i, e: (e, 0, 0))],
        out_specs=row(D_MODEL),
        out_shape=jax.ShapeDtypeStruct((t, D_MODEL), F32),
        scratch_shapes=[pltpu.VMEM((tm, D_MODEL), F32)],
        compiler_params=_cparams("parallel", "arbitrary"),
        name="moe",
    )(h, comb, x, g2, lw["w_gate"], lw["w_up"], lw["w_down"])


def _rope_tables(n_tokens):
    t = jnp.arange(n_tokens)
    row = (t // GRID_W).astype(F32)
    col = (t % GRID_W).astype(F32)
    half = HEAD_DIM // 2
    inv = ROPE_THETA ** (-jnp.arange(0, half, 2, dtype=F32) / half)
    ang_r = row[:, None] * inv[None, :]
    ang_c = col[:, None] * inv[None, :]
    ang = jnp.concatenate([ang_r, ang_r, ang_c, ang_c], axis=-1)
    cos, sin = jnp.cos(ang), jnp.sin(ang)
    return cos, sin, cos.T, sin.T


def _layer_weights(l, p):
    o = [0]
    for s in IN_SPLITS:
        o.append(o[-1] + s)
    w_in = p["w_in"][l]
    seg = lambda i: w_in[:, o[i]:o[i + 1]]
    ba = jnp.concatenate([seg(5), seg(6)], axis=1)
    lanes16 = lambda v: jnp.zeros((LANES,), F32).at[N_GATES:2 * N_GATES].set(v.reshape(N_GATES).astype(F32))
    alog, dtb = lanes16(p["dn_a_log"][l]), lanes16(p["dn_dt_bias"][l])
    w_out = p["w_out"][l]
    wr = jnp.zeros((D_MODEL, LANES), F32)
    wr = wr.at[:, :MOE_GROUPS].set(p["router_g_w"][l]).at[:, ROUTER_E0:ROUTER_E0 + MOE_EXPERTS].set(p["router_e_w"][l])
    br = jnp.zeros((1, LANES), F32)
    br = br.at[0, :MOE_GROUPS].set(p["router_g_b"][l]).at[0, ROUTER_E0:ROUTER_E0 + MOE_EXPERTS].set(p["router_e_b"][l])
    return dict(
        norm1_w=p["norm1_w"][l][None], norm2_w=p["norm2_w"][l][None],
        wqt=seg(0).T.astype(BF16), wk=seg(1).astype(BF16), wvt=seg(2).T.astype(BF16), wdn=seg(3).astype(BF16),
        wg=seg(4).astype(BF16), wba=jnp.pad(ba, ((0, 0), (0, LANES - 2 * N_GATES))).astype(BF16),
        wbat=ba.T.astype(BF16), ws5=seg(7).astype(BF16),
        qn_col=p["q_norm_w"][l][:, None], kn_row=p["k_norm_w"][l][None],
        conv_w=jnp.pad(p["dn_conv_w"][l], ((0, SUBLANES - DN_CONV), (0, 0))),
        alog_row=alog[None], dtb_row=dtb[None], alog_col=alog[:2 * N_GATES, None], dtb_col=dtb[:2 * N_GATES, None],
        dnw=p["dn_out_norm_w"][l][None], wglu=p["s5_w_glu"][l].astype(BF16), bglu=p["s5_b_glu"][l][None],
        wo_a=w_out[:ATTN_WIDTH].astype(BF16), wo_b=w_out[ATTN_WIDTH:ATTN_WIDTH + DN_WIDTH].astype(BF16),
        wo_c=w_out[ATTN_WIDTH + DN_WIDTH:].astype(BF16), wr=wr, br=br,
        w_gate=p["w_gate"][l].astype(BF16), w_up=p["w_up"][l].astype(BF16), w_down=p["w_down"][l].astype(BF16),
        s5=_s5_matrices(p["s5_a_re"][l], p["s5_a_im"][l], p["s5_log_dt"][l], p["s5_b_re"][l], p["s5_b_im"][l],
                        p["s5_c_re"][l], p["s5_c_im"][l], p["s5_d"][l]),
    )


TM_CTX = 256
TM_IN = 256
TM_SEQ = 512


def _mixers(x, mod, lw, tabs, rope, s_dn, s_s5, tm_in, tm):
    qt, k, vt, dn, gate, ba, bat, u = _in_proj(x, mod, lw, tabs, rope, tm_in)
    q_dn, k_dn, v_dn, gcol, grow = _gdn_prep(dn, ba, bat, lw, tm)
    o_f, o_b, s_dn = _gdn_scan(q_dn, k_dn, v_dn, gcol, grow, s_dn, tm)
    y, s_s5 = _s5_mixer(u, lw["s5"], s_s5)
    return dict(qt=qt, k=k, vt=vt, gate=gate, o_f=o_f, o_b=o_b, y=y), s_dn, s_s5


def kernel(x, c, ctx, c_ctx, w_ada, b_ada, norm1_w, norm2_w, w_in, q_norm_w, k_norm_w, dn_conv_w, dn_a_log, dn_dt_bias,
           dn_out_norm_w, s5_a_re, s5_a_im, s5_log_dt, s5_b_re, s5_b_im, s5_c_re, s5_c_im, s5_d, s5_w_glu, s5_b_glu,
           w_out, router_g_w, router_g_b, router_e_w, router_e_b, w_gate, w_up, w_down):
    p = dict(norm1_w=norm1_w, norm2_w=norm2_w, w_in=w_in, q_norm_w=q_norm_w, k_norm_w=k_norm_w, dn_conv_w=dn_conv_w,
             dn_a_log=dn_a_log, dn_dt_bias=dn_dt_bias, dn_out_norm_w=dn_out_norm_w, s5_a_re=s5_a_re, s5_a_im=s5_a_im,
             s5_log_dt=s5_log_dt, s5_b_re=s5_b_re, s5_b_im=s5_b_im, s5_c_re=s5_c_re, s5_c_im=s5_c_im, s5_d=s5_d,
             s5_w_glu=s5_w_glu, s5_b_glu=s5_b_glu, w_out=w_out, router_g_w=router_g_w, router_g_b=router_g_b,
             router_e_w=router_e_w, router_e_b=router_e_b, w_gate=w_gate, w_up=w_up, w_down=w_down)
    assert x.shape[0] == 1 and ctx.shape[0] == 1
    xl, xc = x[0], ctx[0]
    n_ctx = xc.shape[0]
    cvec = jnp.zeros((SUBLANES, D_MODEL), F32).at[0].set(c[0]).at[1].set(c_ctx)
    mods = _adaln_mod(cvec, w_ada, b_ada)
    tabs = _rope_tables(xl.shape[0])
    tabs_c = (tabs[0][:n_ctx], tabs[1][:n_ctx], tabs[2][:, :n_ctx], tabs[3][:, :n_ctx])
    names = ("sh1", "sc1", "g1", "sh2", "sc2", "g2")
    for l in range(DEPTH):
        lw = _layer_weights(l, p)
        mod_l = {n: mods[l, 0:1, i * D_MODEL:(i + 1) * D_MODEL] for i, n in enumerate(names)}
        mod_c = {n: mods[l, 1:2, i * D_MODEL:(i + 1) * D_MODEL] for i, n in enumerate(names)}
        s_dn0 = jnp.zeros((N_GATES, HEAD_DIM, HEAD_DIM), F32)
        s_s50 = jnp.zeros((2, S5_GROUPS * S5_HW), F32)
        mc, s_dn, s_s5 = _mixers(xc, mod_c, lw, tabs_c, False, s_dn0, s_s50, TM_CTX, TM_CTX)
        ml, _, _ = _mixers(xl, mod_l, lw, tabs, True, s_dn, s_s5, TM_IN, TM_SEQ)
        att_l = _attention(ml["qt"], jnp.concatenate([mc["k"], ml["k"]], axis=0),
                           jnp.concatenate([mc["vt"], ml["vt"]], axis=0))
        xl, h2, comb = _mix_out(att_l, ml["o_f"], ml["o_b"], ml["gate"], ml["y"], xl, mod_l, lw, TM_SEQ)
        xl = _moe(h2, comb, xl, mod_l["g2"], lw, TM_SEQ)
        if l < DEPTH - 1:
            att_c = _attention(mc["qt"], mc["k"], mc["vt"])
            xc, h2, comb = _mix_out(att_c, mc["o_f"], mc["o_b"], mc["gate"], mc["y"], xc, mod_c, lw, TM_CTX)
            xc = _moe(h2, comb, xc, mod_c["g2"], lw, TM_CTX)
    return xl[None]
```

```python
import functools
import math

import jax
import jax.numpy as jnp
from jax import lax
from jax.experimental import pallas as pl
from jax.experimental.pallas import tpu as pltpu

F32 = jnp.float32
BF16 = jnp.bfloat16

D_MODEL = 2048
DEPTH = 2
GRID_W = 64
N_DIR = 2
HEAD_DIM = 128
ATTN_WIDTH = D_MODEL // 2
ATTN_Q_HEADS = ATTN_WIDTH // HEAD_DIM
ATTN_KV_HEADS = ATTN_Q_HEADS // 4
KV_WIDTH = ATTN_KV_HEADS * HEAD_DIM
ROPE_THETA = 10000.0
DN_WIDTH = D_MODEL // 4
DN_HEADS = DN_WIDTH // HEAD_DIM
DN_CONV = 5
DN_CHUNK = 64
S5_WIDTH = D_MODEL // 4
S5_GROUP_CH = 16
S5_GROUPS = S5_WIDTH // S5_GROUP_CH
S5_STATE = 64
MIX_WIDTH = ATTN_WIDTH + DN_WIDTH + S5_WIDTH
IN_SPLITS = (ATTN_WIDTH, KV_WIDTH, KV_WIDTH, 3 * DN_WIDTH, DN_WIDTH, N_DIR * DN_HEADS, N_DIR * DN_HEADS, S5_WIDTH)
MOE_GROUPS = 4
MOE_PER_GROUP = 4
MOE_EXPERTS = MOE_GROUPS * MOE_PER_GROUP
MOE_HIDDEN = D_MODEL // 4
EPS = 1e-6

LANES = 128
SUBLANES = 8
VMEM_LIMIT_BYTES = 56 * 1024 * 1024

N_GATES = N_DIR * DN_HEADS
S5_CHUNK = 16
S5_CW = S5_CHUNK * S5_GROUP_CH


def _cparams(*sem):
    return pltpu.CompilerParams(dimension_semantics=sem, vmem_limit_bytes=VMEM_LIMIT_BYTES)


def _const_spec(shape):
    nd = len(shape)
    return pl.BlockSpec(shape, lambda *_: (0,) * nd, pipeline_mode=pl.Buffered(1))


def _silu(x):
    return x * jax.nn.sigmoid(x)


MOD_TN = 1024


def _mod_kernel(cv_ref, w_ref, b_ref, o_ref):
    s = _silu(cv_ref[...]).astype(BF16)
    o_ref[...] = jnp.dot(s, w_ref[...].astype(BF16), preferred_element_type=F32) + b_ref[...]


def _adaln_mod(cvec, w_ada, b_ada):
    n6 = w_ada.shape[-1]
    return pl.pallas_call(
        _mod_kernel,
        grid=(DEPTH, n6 // MOD_TN),
        in_specs=[
            pl.BlockSpec((SUBLANES, D_MODEL), lambda l, j: (0, 0)),
            pl.BlockSpec((None, D_MODEL, MOD_TN), lambda l, j: (l, 0, j)),
            pl.BlockSpec((None, 1, MOD_TN), lambda l, j: (l, 0, j)),
        ],
        out_specs=pl.BlockSpec((None, SUBLANES, MOD_TN), lambda l, j: (l, 0, j)),
        out_shape=jax.ShapeDtypeStruct((DEPTH, SUBLANES, n6), F32),
        compiler_params=_cparams("parallel", "parallel"),
        name="adaln_mod",
    )(cvec, w_ada, b_ada.reshape(DEPTH, 1, n6))


def _rms_rows(x):
    return x * lax.rsqrt(jnp.mean(x * x, axis=-1, keepdims=True) + EPS)


def _in_kernel(rope, x_ref, nw_ref, sh_ref, sc_ref, wqt_ref, wk_ref, wvt_ref, wdn_ref, wg_ref, wba_ref, wbat_ref,
               ws5_ref, qn_ref, kn_ref, cos_ref, sin_ref, cost_ref, sint_ref,
               qt_ref, k_ref, vt_ref, dn_ref, g_ref, ba_ref, bat_ref, u_ref):
    x = x_ref[...]
    h = (_rms_rows(x) * nw_ref[...] * (1.0 + sc_ref[...]) + sh_ref[...]).astype(BF16)
    nt = (((1,), (1,)), ((), ()))

    qt = lax.dot_general(wqt_ref[...], h, nt, preferred_element_type=F32)
    q_scale = HEAD_DIM ** -0.5 * math.log2(math.e)
    for hd in range(ATTN_Q_HEADS):
        qh = qt[hd * HEAD_DIM:(hd + 1) * HEAD_DIM, :]
        qh = qh * lax.rsqrt(jnp.mean(qh * qh, axis=0, keepdims=True) + EPS) * qn_ref[...]
        if rope:
            a, b, c, d = (qh[i * 32:(i + 1) * 32, :] for i in range(4))
            rot = jnp.concatenate([-b, a, -d, c], axis=0)
            qh = qh * cost_ref[...] + rot * sint_ref[...]
        qt_ref[hd * HEAD_DIM:(hd + 1) * HEAD_DIM, :] = (qh * q_scale).astype(BF16)

    k = jnp.dot(h, wk_ref[...], preferred_element_type=F32)
    lane = lax.broadcasted_iota(jnp.int32, (1, HEAD_DIM), 1)
    first_half = (lane % 64) < 32
    for hd in range(ATTN_KV_HEADS):
        kh = _rms_rows(k[:, hd * HEAD_DIM:(hd + 1) * HEAD_DIM]) * kn_ref[...]
        if rope:
            rot = jnp.where(first_half, -pltpu.roll(kh, 96, 1), pltpu.roll(kh, 32, 1))
            kh = kh * cos_ref[...] + rot * sin_ref[...]
        k_ref[:, hd * HEAD_DIM:(hd + 1) * HEAD_DIM] = kh.astype(BF16)

    vt_ref[...] = lax.dot_general(wvt_ref[...], h, nt, preferred_element_type=F32).astype(BF16)
    dn_ref[...] = jnp.dot(h, wdn_ref[...], preferred_element_type=F32)
    g_ref[...] = jnp.dot(h, wg_ref[...], preferred_element_type=F32)
    ba_ref[...] = jnp.dot(h, wba_ref[...], preferred_element_type=F32)
    bat_ref[...] = lax.dot_general(wbat_ref[...], h, nt, preferred_element_type=F32)
    u_ref[...] = jnp.dot(h, ws5_ref[...], preferred_element_type=F32).astype(BF16)


def _in_proj(x, mod, lw, rope_tabs, rope, tm):
    t = x.shape[0]
    cos, sin, cost, sint = rope_tabs
    row = lambda w: pl.BlockSpec((tm, w), lambda i: (i, 0))
    col = lambda w: pl.BlockSpec((w, tm), lambda i: (0, i))
    vec = lambda a: _const_spec(a.shape)
    ws = [lw["wqt"], lw["wk"], lw["wvt"], lw["wdn"], lw["wg"], lw["wba"], lw["wbat"], lw["ws5"]]
    in_specs = ([row(D_MODEL), vec(lw["norm1_w"]), vec(mod["sh1"]), vec(mod["sc1"])] + [vec(w) for w in ws]
                + [vec(lw["qn_col"]), vec(lw["kn_row"]), row(HEAD_DIM), row(HEAD_DIM), col(HEAD_DIM), col(HEAD_DIM)])
    out_specs = [col(ATTN_WIDTH), row(KV_WIDTH), pl.BlockSpec((None, KV_WIDTH, tm), lambda i: (i, 0, 0)), row(3 * DN_WIDTH), row(DN_WIDTH), row(LANES),
                 col(2 * N_GATES), row(S5_WIDTH)]
    sds = jax.ShapeDtypeStruct
    out_shape = [sds((ATTN_WIDTH, t), BF16), sds((t, KV_WIDTH), BF16), sds((t // tm, KV_WIDTH, tm), BF16),
                 sds((t, 3 * DN_WIDTH), F32), sds((t, DN_WIDTH), F32), sds((t, LANES), F32),
                 sds((2 * N_GATES, t), F32), sds((t, S5_WIDTH), BF16)]
    return pl.pallas_call(
        functools.partial(_in_kernel, rope),
        grid=(t // tm,),
        in_specs=in_specs, out_specs=out_specs, out_shape=out_shape,
        compiler_params=_cparams("parallel"),
        name="in_proj",
    )(x, lw["norm1_w"], mod["sh1"], mod["sc1"], *ws, lw["qn_col"], lw["kn_row"], cos, sin, cost, sint)


ATT_TQ = 256
ATT_TK = 256
GQA = ATTN_Q_HEADS // ATTN_KV_HEADS


ATT_ONES = 16


def _attn_kernel(nj, qt_ref, k_ref, vt_ref, ot_ref, sa_ref, sb_ref, m_ref, acc_ref):
    tq = qt_ref.shape[1]
    q = jnp.concatenate([qt_ref[h * HEAD_DIM:(h + 1) * HEAD_DIM, :] for h in range(GQA)], axis=1)
    ones = jnp.ones((ATT_ONES, ATT_TK), BF16)

    def scores(j, dst_ref):
        kb = k_ref[pl.ds(pl.multiple_of(j * ATT_TK, ATT_TK), ATT_TK), :]
        dst_ref[...] = jnp.dot(kb, q, preferred_element_type=F32)

    def consume(j, src_ref):
        vt = jnp.concatenate([vt_ref[j], ones], axis=0)
        for h in range(GQA):
            cols = slice(h * tq, (h + 1) * tq)
            s = src_ref[:, cols]
            m_old = m_ref[:, cols]
            m_new = jnp.maximum(m_old, jnp.max(s, axis=0, keepdims=True))
            p = jnp.exp2(s - m_new).astype(BF16)
            acc_ref[:, cols] = (jnp.exp2(m_old - m_new) * acc_ref[:, cols]
                                + jnp.dot(vt, p, preferred_element_type=F32))
            m_ref[:, cols] = m_new

    m_ref[...] = jnp.full(m_ref.shape, -jnp.inf, F32)
    acc_ref[...] = jnp.zeros(acc_ref.shape, F32)
    scores(0, sa_ref)

    def pair(i, carry):
        j = 2 * i
        scores(j + 1, sb_ref)
        consume(j, sa_ref)
        scores(jnp.minimum(j + 2, nj - 1), sa_ref)
        consume(j + 1, sb_ref)
        return carry

    lax.fori_loop(0, nj // 2, pair, 0)
    if nj % 2:
        consume(nj - 1, sa_ref)
    acc = acc_ref[...]
    o = acc[:HEAD_DIM] * (1.0 / acc[HEAD_DIM:HEAD_DIM + 1])
    for h in range(GQA):
        ot_ref[h * HEAD_DIM:(h + 1) * HEAD_DIM, :] = o[:, h * tq:(h + 1) * tq].astype(BF16)


def _attention(qt, k_all, vt_all):
    t = qt.shape[1]
    tk_total = k_all.shape[0]
    nj = tk_total // ATT_TK
    tq = min(ATT_TQ, t)
    return pl.pallas_call(
        functools.partial(_attn_kernel, nj),
        grid=(ATTN_KV_HEADS, t // tq),
        in_specs=[
            pl.BlockSpec((GQA * HEAD_DIM, tq), lambda g, i: (g, i)),
            pl.BlockSpec((tk_total, HEAD_DIM), lambda g, i: (0, g)),
            pl.BlockSpec((nj, HEAD_DIM, ATT_TK), lambda g, i: (0, g, 0)),
        ],
        out_specs=pl.BlockSpec((GQA * HEAD_DIM, tq), lambda g, i: (g, i)),
        out_shape=jax.ShapeDtypeStruct((ATTN_WIDTH, t), BF16),
        scratch_shapes=[pltpu.VMEM((ATT_TK, GQA * tq), F32), pltpu.VMEM((ATT_TK, GQA * tq), F32),
                        pltpu.VMEM((1, GQA * tq), F32), pltpu.VMEM((HEAD_DIM + ATT_ONES, GQA * tq), F32)],
        compiler_params=_cparams("parallel", "parallel"),
        name="attention",
    )(qt, k_all, vt_all)


def _softplus(x):
    return jnp.maximum(x, 0.0) + jnp.log(1.0 + jnp.exp(-jnp.abs(x)))


def _dot_f32(a, b):
    return jnp.dot(a, b, precision=lax.Precision.HIGHEST, preferred_element_type=F32)


def _gdn_prep_kernel(nblk, cur_ref, prev_ref, next_ref, cw_ref, ba_ref, bat_ref, alog_r, dtb_r, alog_c, dtb_c,
                     q_ref, k_ref, v_ref, gcol_ref, grow_ref, xx_ref):
    i = pl.program_id(0)
    tm = cur_ref.shape[0]
    pad = (DN_CONV - 1) // 2
    xx_ref[0:SUBLANES, :] = jnp.where(i > 0, prev_ref[...], 0.0)
    xx_ref[SUBLANES:SUBLANES + tm, :] = cur_ref[...]
    xx_ref[SUBLANES + tm:2 * SUBLANES + tm, :] = jnp.where(i < nblk - 1, next_ref[...], 0.0)
    acc = None
    for j in range(DN_CONV):
        term = xx_ref[SUBLANES - pad + j:SUBLANES - pad + j + tm, :] * cw_ref[j:j + 1, :]
        acc = term if acc is None else acc + term
    y = _silu(acc)
    for hd in range(DN_HEADS):
        sl = slice(hd * HEAD_DIM, (hd + 1) * HEAD_DIM)
        qh = y[:, sl]
        q_ref[:, sl] = qh * (lax.rsqrt(jnp.sum(qh * qh, axis=-1, keepdims=True) + EPS) * HEAD_DIM ** -0.5)
        kh = y[:, DN_WIDTH + hd * HEAD_DIM:DN_WIDTH + (hd + 1) * HEAD_DIM]
        k_ref[:, sl] = kh * lax.rsqrt(jnp.sum(kh * kh, axis=-1, keepdims=True) + EPS)
    v_ref[...] = y[:, 2 * DN_WIDTH:]

    r = lax.broadcasted_iota(jnp.int32, (tm, tm), 0)
    c = lax.broadcasted_iota(jnp.int32, (tm, tm), 1)
    same = (r // DN_CHUNK) == (c // DN_CHUNK)
    tri_le = jnp.where(same & (c <= r), 1.0, 0.0)
    tri_ge = jnp.where(same & (c >= r), 1.0, 0.0)

    ba = ba_ref[...]
    lane = lax.broadcasted_iota(jnp.int32, (1, LANES), 1)
    g = -jnp.exp(alog_r[...]) * _softplus(ba + dtb_r[...])
    gc = jnp.where(lane < N_GATES + DN_HEADS, _dot_f32(tri_le, g), _dot_f32(tri_ge, g))
    gcol_ref[...] = jnp.where(lane < N_GATES, jax.nn.sigmoid(ba), gc)

    gt = -jnp.exp(alog_c[...]) * _softplus(bat_ref[...] + dtb_c[...])
    row = lax.broadcasted_iota(jnp.int32, (2 * N_GATES, 1), 0)
    gct = jnp.where(row < N_GATES + DN_HEADS, _dot_f32(gt, tri_ge), _dot_f32(gt, tri_le))
    for ch in range(tm // DN_CHUNK):
        grow_ref[ch] = gct[N_GATES:, ch * DN_CHUNK:(ch + 1) * DN_CHUNK]


def _gdn_prep(dn, ba, bat, lw, tm):
    t = dn.shape[0]
    nblk = t // tm
    nsub = tm // SUBLANES
    w3 = 3 * DN_WIDTH
    row = lambda w: pl.BlockSpec((tm, w), lambda i: (i, 0))
    vec = lambda a: _const_spec(a.shape)
    sds = jax.ShapeDtypeStruct
    return pl.pallas_call(
        functools.partial(_gdn_prep_kernel, nblk),
        grid=(nblk,),
        in_specs=[row(w3),
                  pl.BlockSpec((SUBLANES, w3), lambda i: (jnp.maximum(i * nsub - 1, 0), 0)),
                  pl.BlockSpec((SUBLANES, w3), lambda i: (jnp.minimum((i + 1) * nsub, t // SUBLANES - 1), 0)),
                  vec(lw["conv_w"]), row(LANES), pl.BlockSpec((2 * N_GATES, tm), lambda i: (0, i)),
                  vec(lw["alog_row"]), vec(lw["dtb_row"]), vec(lw["alog_col"]), vec(lw["dtb_col"])],
        out_specs=[row(DN_WIDTH), row(DN_WIDTH), row(DN_WIDTH), row(LANES),
                   pl.BlockSpec((tm // DN_CHUNK, N_GATES, DN_CHUNK), lambda i: (i, 0, 0))],
        out_shape=[sds((t, DN_WIDTH), F32), sds((t, DN_WIDTH), F32), sds((t, DN_WIDTH), F32), sds((t, LANES), F32),
                   sds((t // DN_CHUNK, N_GATES, DN_CHUNK), F32)],
        scratch_shapes=[pltpu.VMEM((tm + 2 * SUBLANES, w3), F32)],
        compiler_params=_cparams("parallel"),
        name="gdn_prep",
    )(dn, dn, dn, lw["conv_w"], ba, bat, lw["alog_row"], lw["dtb_row"], lw["alog_col"], lw["dtb_col"])


def _dot_bf(a, b, dims=(((1,), (0,)), ((), ()))):
    return lax.dot_general(a.astype(BF16), b.astype(BF16), dims, preferred_element_type=F32)


_NT = (((1,), (1,)), ((), ()))
_TN = (((0,), (0,)), ((), ()))
TRI_BASE = 8


def _gdn_local(chains):
    c = DN_CHUNK
    ri = lax.broadcasted_iota(jnp.int32, (c, c), 0)
    ci = lax.broadcasted_iota(jnp.int32, (c, c), 1)
    blk = lambda b: (ri // b) == (ci // b)
    eye = jnp.where(ri == ci, 1.0, 0.0)
    n = range(len(chains))
    qs, ks, vs, betas, gcs, grs, fwds = zip(*chains)
    incl = [(ri >= ci) if f else (ri <= ci) for f in fwds]
    strict = [(ri > ci) if f else (ri < ci) for f in fwds]
    decay = [jnp.where(incl[i], jnp.exp(jnp.where(incl[i], gcs[i] - grs[i], 0.0)), 0.0) for i in n]
    glast = [grs[i][:, c - 1:c] if fwds[i] else grs[i][:, 0:1] for i in n]
    eg = [jnp.exp(g) for g in gcs]
    kb = [ks[i] * betas[i] for i in n]
    kbf = [k.astype(BF16) for k in ks]
    a = [jnp.where(strict[i], _dot_bf(kb[i], kbf[i], _NT) * decay[i], 0.0) for i in n]
    qk = [(_dot_bf(qs[i], kbf[i], _NT) * decay[i]).astype(BF16) for i in n]
    p = [jnp.where(blk(TRI_BASE), a[i], 0.0) for i in n]
    x = [eye - p[i] for i in n]
    for _ in range(int(math.log2(TRI_BASE)) - 1):
        p = [_dot_bf(p[i], p[i]) for i in n]
        x = [x[i] + _dot_bf(x[i], p[i]) for i in n]
    b = 2 * TRI_BASE
    while b <= c:
        off = blk(b) & jnp.logical_not(blk(b // 2))
        lx = [_dot_bf(jnp.where(off, a[i], 0.0), x[i]) for i in n]
        x = [x[i] - _dot_bf(x[i], lx[i]) for i in n]
        b *= 2
    sol = [_dot_bf(x[i], jnp.concatenate([vs[i] * betas[i], kb[i] * eg[i]], axis=1)) for i in n]
    return [(sol[i][:, :HEAD_DIM],
             jnp.concatenate([sol[i][:, HEAD_DIM:], qs[i] * eg[i]], axis=0).astype(BF16),
             (ks[i] * jnp.exp(glast[i] - gcs[i])).astype(BF16),
             qk[i]) for i in n]


GDN_LOCAL_CHUNKS = 2


def _gdn_kernel(nblk, qf, kf, vf, gcf, grf, qb, kb, vb, gcb, grb, s0_ref, of_ref, ob_ref, sout_ref,
                s_ref, u_ref, wq_ref, kd_ref, qk_ref):
    i = pl.program_id(0)
    c = DN_CHUNK

    @pl.when(i == 0)
    def _():
        s_ref[...] = s0_ref[...]

    nb = qf.shape[0] // c
    dirs = ((qf, kf, vf, gcf, grf), (qb, kb, vb, gcb, grb))

    def local_body(jp, carry):
        work = []
        for cc in range(GDN_LOCAL_CHUNKS):
            ch = jp * GDN_LOCAL_CHUNKS + cc
            rows = pl.ds(pl.multiple_of(ch * c, c), c)
            for d, (q_ref, k_ref, v_ref, gc_ref, gr_ref) in enumerate(dirs):
                gcol = gc_ref[rows, :]
                grow = gr_ref[ch]
                for h in range(DN_HEADS):
                    gi = d * DN_HEADS + h
                    sl = slice(h * HEAD_DIM, (h + 1) * HEAD_DIM)
                    work.append((ch, gi, (q_ref[rows, sl], k_ref[rows, sl], v_ref[rows, sl], gcol[:, gi:gi + 1],
                                          gcol[:, N_GATES + gi:N_GATES + gi + 1], grow[gi:gi + 1, :], d == 0)))
        done = _gdn_local([args for _, _, args in work])
        for (ch, gi, _), (u, wq, kd, qk) in zip(work, done):
            u_ref[ch, gi] = u
            wq_ref[ch, gi] = wq
            kd_ref[ch, gi] = kd
            qk_ref[ch, gi] = qk
        return carry

    lax.fori_loop(0, nb // GDN_LOCAL_CHUNKS, local_body, 0)

    def seq_body(j, carry):
        work = []
        for d, (gr_ref, o_ref) in enumerate(((grf, of_ref), (grb, ob_ref))):
            jj = j if d == 0 else nb - 1 - j
            grow = gr_ref[jj]
            for h in range(DN_HEADS):
                gi = d * DN_HEADS + h
                glast = grow[gi:gi + 1, c - 1:c] if d == 0 else grow[gi:gi + 1, 0:1]
                work.append((o_ref, jj, h, gi, s_ref[gi], u_ref[jj, gi], wq_ref[jj, gi], kd_ref[jj, gi], qk_ref[jj, gi],
                             jnp.exp(glast)))
        ws = [jnp.dot(w[6], w[4].astype(BF16), preferred_element_type=F32) for w in work]
        v_new = [(w[5] - ws_[:c]).astype(BF16) for w, ws_ in zip(work, ws)]
        o = [ws_[c:] + jnp.dot(w[8], vn, preferred_element_type=F32) for w, ws_, vn in zip(work, ws, v_new)]
        s_new = [w[4] * w[9] + lax.dot_general(w[7], vn, _TN, preferred_element_type=F32) for w, vn in zip(work, v_new)]
        for (o_ref, jj, h, gi, *_), o_, s_ in zip(work, o, s_new):
            o_ref[pl.ds(pl.multiple_of(jj * c, c), c), h * HEAD_DIM:(h + 1) * HEAD_DIM] = o_
            s_ref[gi] = s_
        return carry

    lax.fori_loop(0, nb, seq_body, 0)

    @pl.when(i == nblk - 1)
    def _():
        sout_ref[...] = s_ref[...]


def _gdn_scan(q, k, v, gcol, grow, s0, tm):
    t = q.shape[0]
    nblk = t // tm
    nch = tm // DN_CHUNK
    fwd = lambda w: pl.BlockSpec((tm, w), lambda i: (i, 0))
    bwd = lambda w: pl.BlockSpec((tm, w), lambda i: (nblk - 1 - i, 0))
    sds = jax.ShapeDtypeStruct
    s_shape = (N_GATES, HEAD_DIM, HEAD_DIM)
    return pl.pallas_call(
        functools.partial(_gdn_kernel, nblk),
        grid=(nblk,),
        in_specs=[fwd(DN_WIDTH), fwd(DN_WIDTH), fwd(DN_WIDTH), fwd(LANES),
                  pl.BlockSpec((nch, N_GATES, DN_CHUNK), lambda i: (i, 0, 0)),
                  bwd(DN_WIDTH), bwd(DN_WIDTH), bwd(DN_WIDTH), bwd(LANES),
                  pl.BlockSpec((nch, N_GATES, DN_CHUNK), lambda i: (nblk - 1 - i, 0, 0)),
                  _const_spec(s_shape)],
        out_specs=[fwd(DN_WIDTH), bwd(DN_WIDTH), pl.BlockSpec(s_shape, lambda i: (0, 0, 0))],
        out_shape=[sds((t, DN_WIDTH), F32), sds((t, DN_WIDTH), F32), sds(s_shape, F32)],
        scratch_shapes=[pltpu.VMEM(s_shape, F32),
                        pltpu.VMEM((nch, N_GATES, DN_CHUNK, HEAD_DIM), F32),
                        pltpu.VMEM((nch, N_GATES, 2 * DN_CHUNK, HEAD_DIM), BF16),
                        pltpu.VMEM((nch, N_GATES, DN_CHUNK, HEAD_DIM), BF16),
                        pltpu.VMEM((nch, N_GATES, DN_CHUNK, DN_CHUNK), BF16)],
        compiler_params=_cparams("arbitrary"),
        name="gdn_scan",
    )(q, k, v, gcol, grow, q, k, v, gcol, grow, s0)


S5_SW = 4 * S5_STATE


def _s5_matrices(a_re, a_im, log_dt, b_re, b_im, c_re, c_im, d):
    L, G, P, H = S5_CHUNK, S5_GROUPS, S5_STATE, S5_GROUP_CH
    hi = lax.Precision.HIGHEST
    a_re, a_im = a_re.astype(F32), a_im.astype(F32)
    dt = jnp.exp(log_dt.astype(F32))[..., None]
    n_re = jnp.exp(a_re * dt) * jnp.cos(a_im * dt) - 1.0
    n_im = jnp.exp(a_re * dt) * jnp.sin(a_im * dt)
    den = a_re * a_re + a_im * a_im
    co_re = ((n_re * a_re + n_im * a_im) / den)[..., None]
    co_im = ((n_im * a_re - n_re * a_im) / den)[..., None]
    br, bi = b_re.astype(F32)[None], b_im.astype(F32)[None]
    bb_re, bb_im = co_re * br - co_im * bi, co_re * bi + co_im * br
    tau = jnp.arange(L + 1, dtype=F32)[:, None, None, None]
    mag = jnp.exp(a_re[None] * dt[None] * tau)
    ang = a_im[None] * dt[None] * tau
    pw_re, pw_im = mag * jnp.cos(ang), mag * jnp.sin(ang)
    cr, ci = c_re.astype(F32), c_im.astype(F32)
    e_re = cr[None] * pw_re[:, :, :, None, :] - ci[None] * pw_im[:, :, :, None, :]
    e_im = cr[None] * pw_im[:, :, :, None, :] + ci[None] * pw_re[:, :, :, None, :]
    kk = (jnp.einsum('tdghp,dgpi->tdghi', e_re[:L], bb_re, precision=hi)
          - jnp.einsum('tdghp,dgpi->tdghi', e_im[:L], bb_im, precision=hi))
    s_i = jnp.arange(L)[:, None]
    t_i = jnp.arange(L)[None, :]
    lag = t_i - s_i
    kf = jnp.where((lag >= 0)[..., None, None, None], kk[jnp.clip(lag, 0, L - 1), 0], 0.0)
    kb = jnp.where((lag <= 0)[..., None, None, None], kk[jnp.clip(-lag, 0, L - 1), 1], 0.0)
    dskip = d.astype(F32).reshape(G, H)
    eye_t = (lag == 0).astype(F32)
    skip = eye_t[:, :, None, None, None] * (dskip[:, :, None] * jnp.eye(H, dtype=F32)[None])[None, None]
    toep = (kf + kb + skip).transpose(2, 0, 4, 1, 3).reshape(G, L * H, L * H)

    def loc(pr, pi, dr):
        w_re = pr[..., None] * bb_re[dr][None] - pi[..., None] * bb_im[dr][None]
        w_im = pr[..., None] * bb_im[dr][None] + pi[..., None] * bb_re[dr][None]
        f = lambda w: w.transpose(1, 0, 3, 2).reshape(G, L * H, P)
        return f(w_re), f(w_im)
    wf_re, wf_im = loc(pw_re[:L, 0][::-1], pw_im[:L, 0][::-1], 0)
    wb_re, wb_im = loc(pw_re[:L, 1], pw_im[:L, 1], 1)
    wcat = jnp.concatenate([toep, wf_re, wb_re, wf_im, wb_im], axis=-1).astype(BF16)

    carry = lambda e: e.transpose(1, 3, 0, 2).reshape(G, P, L * H)
    mf_re, mf_im = carry(e_re[1:, 0]), carry(-e_im[1:, 0])
    mb_re, mb_im = carry(e_re[1:, 1][::-1]), carry(-e_im[1:, 1][::-1])
    mcat = jnp.concatenate([mf_re, mb_re, mf_im, mb_im], axis=1).astype(BF16)
    al_re = jnp.concatenate([pw_re[L, 0], pw_re[L, 1]], axis=-1)
    al_im = jnp.concatenate([pw_im[L, 0], pw_im[L, 1]], axis=-1)
    al = jnp.stack([al_re.reshape(-1), al_im.reshape(-1)])
    return wcat, mcat, al


S5_HW = 2 * S5_STATE


def _s5_local_kernel(u_ref, w_ref, y_ref, xre_ref, xim_ref):
    r = jnp.dot(u_ref[...], w_ref[...], preferred_element_type=F32)
    y_ref[...] = r[:, :S5_CW]
    xre_ref[...] = r[:, S5_CW:S5_CW + S5_HW]
    xim_ref[...] = r[:, S5_CW + S5_HW:]


def _s5_local(ug, wcat):
    g, n, _ = ug.shape
    out = lambda w: pl.BlockSpec((n, w), lambda i: (0, i))
    sds = jax.ShapeDtypeStruct
    return pl.pallas_call(
        _s5_local_kernel,
        grid=(g,),
        in_specs=[pl.BlockSpec((None, n, S5_CW), lambda i: (i, 0, 0)),
                  pl.BlockSpec((None, S5_CW, S5_CW + S5_SW), lambda i: (i, 0, 0))],
        out_specs=[out(S5_CW), out(S5_HW), out(S5_HW)],
        out_shape=[sds((n, g * S5_CW), F32), sds((n, g * S5_HW), F32), sds((n, g * S5_HW), F32)],
        compiler_params=_cparams("parallel"),
        name="s5_local",
    )(ug, wcat)


def _s5_scan_kernel(nblk, xfr_ref, xfi_ref, xbr_ref, xbi_ref, al_ref, s0_ref,
                    cfr_ref, cfi_ref, cbr_ref, cbi_ref, sout_ref, s_ref):
    i = pl.program_id(0)

    @pl.when(i == 0)
    def _():
        s_ref[...] = s0_ref[...]

    nb = xfr_ref.shape[0]
    al_re, al_im = al_ref[0:1, :], al_ref[1:2, :]
    is_fwd = (lax.broadcasted_iota(jnp.int32, (1, al_ref.shape[1]), 1) % S5_HW) < S5_STATE

    def body(j, carry):
        re, im = carry
        rf, rb = pl.ds(j, 1), pl.ds(nb - 1 - j, 1)
        cfr_ref[rf, :] = re
        cfi_ref[rf, :] = im
        cbr_ref[rb, :] = re
        cbi_ref[rb, :] = im
        in_re = jnp.where(is_fwd, xfr_ref[rf, :], xbr_ref[rb, :])
        in_im = jnp.where(is_fwd, xfi_ref[rf, :], xbi_ref[rb, :])
        return al_re * re - al_im * im + in_re, al_re * im + al_im * re + in_im

    re, im = lax.fori_loop(0, nb, body, (s_ref[0:1, :], s_ref[1:2, :]))
    s_ref[0:1, :] = re
    s_ref[1:2, :] = im

    @pl.when(i == nblk - 1)
    def _():
        sout_ref[...] = s_ref[...]


def _s5_scan(x_re, x_im, al, s0, nb):
    n, w = x_re.shape
    nblk = n // nb
    fwd = pl.BlockSpec((nb, w), lambda i: (i, 0))
    bwd = pl.BlockSpec((nb, w), lambda i: (nblk - 1 - i, 0))
    sds = jax.ShapeDtypeStruct
    return pl.pallas_call(
        functools.partial(_s5_scan_kernel, nblk),
        grid=(nblk,),
        in_specs=[fwd, fwd, bwd, bwd, _const_spec(al.shape), _const_spec(s0.shape)],
        out_specs=[fwd, fwd, bwd, bwd, pl.BlockSpec(s0.shape, lambda i: (0, 0))],
        out_shape=[sds((n, w), F32)] * 4 + [sds(s0.shape, F32)],
        scratch_shapes=[pltpu.VMEM(s0.shape, F32)],
        compiler_params=_cparams("arbitrary"),
        name="s5_scan",
    )(x_re, x_im, x_re, x_im, al, s0)


def _s5_carry_kernel(y_ref, cfr_ref, cfi_ref, cbr_ref, cbi_ref, m_ref, o_ref):
    is_fwd = lax.broadcasted_iota(jnp.int32, (1, S5_HW), 1) < S5_STATE
    cin = jnp.concatenate([jnp.where(is_fwd, cfr_ref[...], cbr_ref[...]),
                           jnp.where(is_fwd, cfi_ref[...], cbi_ref[...])], axis=1).astype(BF16)
    o_ref[...] = y_ref[...] + jnp.dot(cin, m_ref[...], preferred_element_type=F32)


def _s5_carry(y, cins, mcat):
    n = y.shape[0]
    g = mcat.shape[0]
    blk = lambda w: pl.BlockSpec((n, w), lambda i: (0, i))
    return pl.pallas_call(
        _s5_carry_kernel,
        grid=(g,),
        in_specs=[blk(S5_CW)] + [blk(S5_HW)] * 4 + [pl.BlockSpec((None, S5_SW, S5_CW), lambda i: (i, 0, 0))],
        out_specs=blk(S5_CW),
        out_shape=jax.ShapeDtypeStruct(y.shape, F32),
        compiler_params=_cparams("parallel"),
        name="s5_carry",
    )(y, *cins, mcat)


def _s5_mixer(u, mats, s0):
    wcat, mcat, al = mats
    t = u.shape[0]
    n = t // S5_CHUNK
    ug = u.reshape(n, S5_CHUNK, S5_GROUPS, S5_GROUP_CH).transpose(2, 0, 1, 3).reshape(S5_GROUPS, n, S5_CW)
    y_loc, x_re, x_im = _s5_local(ug, wcat)
    *cins, s_fin = _s5_scan(x_re, x_im, al, s0, min(n, 128))
    y = _s5_carry(y_loc, cins, mcat)
    y = y.reshape(n, S5_GROUPS, S5_CHUNK, S5_GROUP_CH).transpose(0, 2, 1, 3).reshape(t, S5_WIDTH)
    return y, s_fin


ROUTER_E0 = MOE_GROUPS
ROUTE_SEL = 32


def _gelu_tanh(x):
    return 0.5 * x * (1.0 + jnp.tanh(math.sqrt(2.0 / math.pi) * (x + 0.044715 * (x * x * x))))


def _route(logits):
    ninf = float("-inf")
    lane = lax.broadcasted_iota(jnp.int32, (1, LANES), 1)
    lanef = lane.astype(F32)
    first = lambda hit: jnp.min(jnp.where(hit, lanef, float(LANES)), axis=-1, keepdims=True)
    gl = jnp.where(lane < MOE_GROUPS, logits, ninf)
    gmax = jnp.max(gl, axis=-1, keepdims=True)
    pg_top = 1.0 / jnp.sum(jnp.exp(gl - gmax), axis=-1, keepdims=True)
    base = ROUTER_E0 + MOE_PER_GROUP * first(gl == gmax)
    el = jnp.where((lanef >= base) & (lanef < base + MOE_PER_GROUP), logits, ninf)
    emax = jnp.max(el, axis=-1, keepdims=True)
    esum = jnp.sum(jnp.exp(el - emax), axis=-1, keepdims=True)
    i1 = first(el == emax)
    el2 = jnp.where(lanef == i1, ninf, el)
    emax2 = jnp.max(el2, axis=-1, keepdims=True)
    i2 = first(el2 == emax2)
    p1 = 1.0 / esum
    p2 = jnp.exp(emax2 - emax) / esum
    w1 = pg_top * (p1 / (p1 + p2))
    w2 = pg_top * (p2 / (p1 + p2))
    comb = jnp.where(lanef == i1, w1, 0.0) + jnp.where(lanef == i2, w2, 0.0)
    sel = (jnp.where(lane == ROUTE_SEL, i1 - ROUTER_E0, 0.0) + jnp.where(lane == ROUTE_SEL + 1, i2 - ROUTER_E0, 0.0)
           + jnp.where(lane == ROUTE_SEL + 2, w1, 0.0) + jnp.where(lane == ROUTE_SEL + 3, w2, 0.0))
    return comb + sel


def _out_kernel(att_ref, of_ref, ob_ref, gate_ref, y_ref, x_ref, g1_ref, sh2_ref, sc2_ref, n2_ref, dnw_ref, bglu_ref,
                wglu_ref, woa_ref, wob_ref, woc_ref, wr_ref, br_ref, xo_ref, comb_ref, h_ref=None):
    o = of_ref[...] + ob_ref[...]
    gate = _silu(gate_ref[...])
    dn = jnp.concatenate(
        [_rms_rows(o[:, h * HEAD_DIM:(h + 1) * HEAD_DIM]) * dnw_ref[...] for h in range(DN_HEADS)], axis=1) * gate
    z = _gelu_tanh(y_ref[...])
    s5 = z * jax.nn.sigmoid(jnp.dot(z.astype(BF16), wglu_ref[...], preferred_element_type=F32) + bglu_ref[...])
    mix = (lax.dot_general(att_ref[...], woa_ref[...], _TN, preferred_element_type=F32)
           + jnp.dot(dn.astype(BF16), wob_ref[...], preferred_element_type=F32)
           + jnp.dot(s5.astype(BF16), woc_ref[...], preferred_element_type=F32))
    x = x_ref[...] + g1_ref[...] * mix
    xo_ref[...] = x
    h = _rms_rows(x) * n2_ref[...] * (1.0 + sc2_ref[...]) + sh2_ref[...]
    if h_ref is not None:
        h_ref[...] = h.astype(BF16)
    comb_ref[...] = _route(_dot_f32(h, wr_ref[...]) + br_ref[...])


def _mix_out(att_t, o_f, o_b, gate, y, x, mod, lw, tm, emit_h):
    t = x.shape[0]
    row = lambda w: pl.BlockSpec((tm, w), lambda i: (i, 0))
    vec = lambda a: _const_spec(a.shape)
    consts = [mod["g1"], mod["sh2"], mod["sc2"], lw["norm2_w"], lw["dnw"], lw["bglu"], lw["wglu"], lw["wo_a"],
              lw["wo_b"], lw["wo_c"], lw["wr"], lw["br"]]
    sds = jax.ShapeDtypeStruct
    return pl.pallas_call(
        _out_kernel,
        grid=(t // tm,),
        in_specs=[pl.BlockSpec((ATTN_WIDTH, tm), lambda i: (0, i)), row(DN_WIDTH), row(DN_WIDTH), row(DN_WIDTH),
                  row(S5_WIDTH), row(D_MODEL)] + [vec(a) for a in consts],
        out_specs=[row(D_MODEL), row(LANES)] + ([row(D_MODEL)] if emit_h else []),
        out_shape=[sds((t, D_MODEL), F32), sds((t, LANES), F32)] + ([sds((t, D_MODEL), BF16)] if emit_h else []),
        compiler_params=_cparams("parallel"),
        name="mix_out",
    )(att_t, o_f, o_b, gate, y, x, *consts)


def _moe_kernel(h_ref, comb_ref, x_ref, g2_ref, wg_ref, wu_ref, wd_ref, o_ref, acc_ref):
    e = pl.program_id(1)

    @pl.when(e == 0)
    def _():
        acc_ref[...] = jnp.zeros(acc_ref.shape, F32)

    h = h_ref[...]
    mid = _silu(jnp.dot(h, wg_ref[...], preferred_element_type=F32)) * jnp.dot(h, wu_ref[...], preferred_element_type=F32)
    lane = lax.broadcasted_iota(jnp.int32, (1, LANES), 1)
    cw = jnp.sum(jnp.where(lane == e + ROUTER_E0, comb_ref[...], 0.0), axis=-1, keepdims=True)
    acc_ref[...] += cw * jnp.dot(mid.astype(BF16), wd_ref[...], preferred_element_type=F32)

    @pl.when(e == MOE_EXPERTS - 1)
    def _():
        o_ref[...] = x_ref[...] + g2_ref[...] * acc_ref[...]


def _moe(h, comb, x, g2, lw, tm):
    t = x.shape[0]
    row = lambda w: pl.BlockSpec((tm, w), lambda i, e: (i, 0))
    return pl.pallas_call(
        _moe_kernel,
        grid=(t // tm, MOE_EXPERTS),
        in_specs=[row(D_MODEL), row(LANES), row(D_MODEL), _const_spec(g2.shape),
                  pl.BlockSpec((None, D_MODEL, MOE_HIDDEN), lambda i, e: (e, 0, 0)),
                  pl.BlockSpec((None, D_MODEL, MOE_HIDDEN), lambda i, e: (e, 0, 0)),
                  pl.BlockSpec((None, MOE_HIDDEN, D_MODEL), lambda i, e: (e, 0, 0))],
        out_specs=row(D_MODEL),
        out_shape=jax.ShapeDtypeStruct((t, D_MODEL), F32),
        scratch_shapes=[pltpu.VMEM((tm, D_MODEL), F32)],
        compiler_params=_cparams("parallel", "arbitrary"),
        name="moe",
    )(h, comb, x, g2, lw["w_gate"], lw["w_up"], lw["w_down"])


MOE_PAIRS = MOE_PER_GROUP * (MOE_PER_GROUP - 1) // 2
MOE_BUCKETS = MOE_GROUPS * MOE_PAIRS
MOE_TM = 128


def _moe_plan(route, t):
    n_tiles = t // MOE_TM + MOE_BUCKETS
    e1 = route[:, ROUTE_SEL].astype(jnp.int32)
    e2 = route[:, ROUTE_SEL + 1].astype(jnp.int32)
    w1, w2 = route[:, ROUTE_SEL + 2], route[:, ROUTE_SEL + 3]
    lo, hi = jnp.minimum(e1, e2), jnp.maximum(e1, e2)
    w_lo, w_hi = jnp.where(e1 < e2, w1, w2), jnp.where(e1 < e2, w2, w1)
    a, b = lo % MOE_PER_GROUP, hi % MOE_PER_GROUP
    pair = a * (2 * MOE_PER_GROUP - 1 - a) // 2 + (b - a - 1)
    bucket = (lo // MOE_PER_GROUP) * MOE_PAIRS + pair
    onehot = (bucket[:, None] == jnp.arange(MOE_BUCKETS)[None, :]).astype(jnp.int32)
    csum = jnp.cumsum(onehot, axis=0)
    counts = csum[-1]
    rank = jnp.sum((csum - 1) * onehot, axis=1)
    tiles = (counts + MOE_TM - 1) // MOE_TM
    tile_end = jnp.cumsum(tiles)
    tile_off = tile_end - tiles
    slot = tile_off[bucket] * MOE_TM + rank
    tok = jnp.zeros((n_tiles * MOE_TM,), jnp.int32).at[slot].set(jnp.arange(t, dtype=jnp.int32))
    wrow = jnp.zeros((n_tiles * MOE_TM, LANES), F32).at[slot, 0].set(w_lo).at[slot, 1].set(w_hi)
    tile = jnp.arange(n_tiles)
    tb = jnp.minimum(jnp.searchsorted(tile_end, tile, side="right"), MOE_BUCKETS - 1)
    nvalid = jnp.clip(counts[tb] - (tile - tile_off[tb]) * MOE_TM, 0, MOE_TM)
    nvalid = jnp.where(tile < tile_end[-1], nvalid, 0).astype(jnp.int32)
    pairs = [(i, j) for i in range(MOE_PER_GROUP) for j in range(i + 1, MOE_PER_GROUP)]
    pa = jnp.array([p[0] for p in pairs], jnp.int32)
    pb = jnp.array([p[1] for p in pairs], jnp.int32)
    grp = (tb // MOE_PAIRS) * MOE_PER_GROUP
    e_lo = (grp + pa[tb % MOE_PAIRS]).astype(jnp.int32)
    e_hi = (grp + pb[tb % MOE_PAIRS]).astype(jnp.int32)
    return tok, nvalid, e_lo, e_hi, wrow


def _moe_routed_kernel(n_tiles, tok_ref, nv_ref, elo_ref, ehi_ref,
                       x_hbm, wrow_ref, g2_ref, sh2_ref, sc2_ref, n2_ref, wgl_ref, wul_ref, wdl_ref,
                       wgh_ref, wuh_ref, wdh_ref, o_hbm, xbuf, obuf, gsem, ssem):
    t = pl.program_id(0)
    slot = lax.rem(t, 2)

    def row_copy(tile, s, r, scatter):
        tok = tok_ref[tile * MOE_TM + r]
        if scatter:
            return pltpu.make_async_copy(obuf.at[s, pl.ds(r, 1)], o_hbm.at[pl.ds(tok, 1)], ssem.at[s])
        return pltpu.make_async_copy(x_hbm.at[pl.ds(tok, 1)], xbuf.at[s, pl.ds(r, 1)], gsem.at[s])

    def start_rows(tile, s, scatter):
        def body(r, carry):
            row_copy(tile, s, r, scatter).start()
            return carry
        lax.fori_loop(0, nv_ref[tile], body, 0)

    def wait_rows(tile, s, scatter):
        n = nv_ref[tile]
        for bit in range(MOE_TM.bit_length()):
            rows = 1 << bit

            @pl.when((n >> bit) & 1 == 1)
            def _():
                if scatter:
                    pltpu.make_async_copy(obuf.at[s, pl.ds(0, rows)], o_hbm.at[pl.ds(0, rows)], ssem.at[s]).wait()
                else:
                    pltpu.make_async_copy(x_hbm.at[pl.ds(0, rows)], xbuf.at[s, pl.ds(0, rows)], gsem.at[s]).wait()

    @pl.when(t == 0)
    def _():
        xbuf[...] = jnp.zeros(xbuf.shape, F32)
        start_rows(0, 0, False)

    @pl.when(t + 1 < n_tiles)
    def _():
        start_rows(t + 1, 1 - slot, False)

    wait_rows(t, slot, False)

    @pl.when(t >= 2)
    def _():
        wait_rows(t - 2, slot, True)

    @pl.when(nv_ref[t] > 0)
    def _():
        x = xbuf[slot]
        h = (_rms_rows(x) * n2_ref[...] * (1.0 + sc2_ref[...]) + sh2_ref[...]).astype(BF16)
        w = wrow_ref[...]
        y = None
        for k, (wg_ref, wu_ref, wd_ref) in enumerate(((wgl_ref, wul_ref, wdl_ref), (wgh_ref, wuh_ref, wdh_ref))):
            mid = (_silu(jnp.dot(h, wg_ref[...], preferred_element_type=F32))
                   * jnp.dot(h, wu_ref[...], preferred_element_type=F32))
            yk = w[:, k:k + 1] * jnp.dot(mid.astype(BF16), wd_ref[...], preferred_element_type=F32)
            y = yk if y is None else y + yk
        obuf[slot] = x + g2_ref[...] * y
        start_rows(t, slot, True)

    @pl.when(t == n_tiles - 1)
    def _():
        if n_tiles >= 2:
            wait_rows(t - 1, 1 - slot, True)
        wait_rows(t, slot, True)


def _moe_routed(x, route, mod, lw):
    t = x.shape[0]
    tok, nvalid, e_lo, e_hi, wrow = _moe_plan(route, t)
    n_tiles = nvalid.shape[0]
    vec = lambda a: pl.BlockSpec(a.shape, lambda i, *_: (0,) * a.ndim, pipeline_mode=pl.Buffered(1))
    wspec = lambda shape, which: pl.BlockSpec(
        (None,) + shape, (lambda i, tok, nv, elo, ehi: (elo[i], 0, 0)) if which == 0
        else (lambda i, tok, nv, elo, ehi: (ehi[i], 0, 0)))
    up, down = (D_MODEL, MOE_HIDDEN), (MOE_HIDDEN, D_MODEL)
    consts = [mod["g2"], mod["sh2"], mod["sc2"], lw["norm2_w"]]
    grid_spec = pltpu.PrefetchScalarGridSpec(
        num_scalar_prefetch=4,
        grid=(n_tiles,),
        in_specs=[pl.BlockSpec(memory_space=pl.ANY), pl.BlockSpec((MOE_TM, LANES), lambda i, *_: (i, 0))]
                 + [vec(a) for a in consts]
                 + [wspec(up, 0), wspec(up, 0), wspec(down, 0), wspec(up, 1), wspec(up, 1), wspec(down, 1)],
        out_specs=pl.BlockSpec(memory_space=pl.ANY),
        scratch_shapes=[pltpu.VMEM((2, MOE_TM, D_MODEL), F32), pltpu.VMEM((2, MOE_TM, D_MODEL), F32),
                        pltpu.SemaphoreType.DMA((2,)), pltpu.SemaphoreType.DMA((2,))],
    )
    return pl.pallas_call(
        functools.partial(_moe_routed_kernel, n_tiles),
        grid_spec=grid_spec,
        out_shape=jax.ShapeDtypeStruct((t, D_MODEL), F32),
        compiler_params=_cparams("arbitrary"),
        name="moe_routed",
    )(tok, nvalid, e_lo, e_hi, x, wrow, *consts, lw["w_gate"], lw["w_up"], lw["w_down"],
      lw["w_gate"], lw["w_up"], lw["w_down"])


def _rope_tables(n_tokens):
    t = jnp.arange(n_tokens)
    row = (t // GRID_W).astype(F32)
    col = (t % GRID_W).astype(F32)
    half = HEAD_DIM // 2
    inv = ROPE_THETA ** (-jnp.arange(0, half, 2, dtype=F32) / half)
    ang_r = row[:, None] * inv[None, :]
    ang_c = col[:, None] * inv[None, :]
    ang = jnp.concatenate([ang_r, ang_r, ang_c, ang_c], axis=-1)
    cos, sin = jnp.cos(ang), jnp.sin(ang)
    return cos, sin, cos.T, sin.T


def _layer_weights(l, p):
    o = [0]
    for s in IN_SPLITS:
        o.append(o[-1] + s)
    w_in = p["w_in"][l]
    seg = lambda i: w_in[:, o[i]:o[i + 1]]
    ba = jnp.concatenate([seg(5), seg(6)], axis=1)
    lanes16 = lambda v: jnp.zeros((LANES,), F32).at[N_GATES:2 * N_GATES].set(v.reshape(N_GATES).astype(F32))
    alog, dtb = lanes16(p["dn_a_log"][l]), lanes16(p["dn_dt_bias"][l])
    w_out = p["w_out"][l]
    wr = jnp.zeros((D_MODEL, LANES), F32)
    wr = wr.at[:, :MOE_GROUPS].set(p["router_g_w"][l]).at[:, ROUTER_E0:ROUTER_E0 + MOE_EXPERTS].set(p["router_e_w"][l])
    br = jnp.zeros((1, LANES), F32)
    br = br.at[0, :MOE_GROUPS].set(p["router_g_b"][l]).at[0, ROUTER_E0:ROUTER_E0 + MOE_EXPERTS].set(p["router_e_b"][l])
    return dict(
        norm1_w=p["norm1_w"][l][None], norm2_w=p["norm2_w"][l][None],
        wqt=seg(0).T.astype(BF16), wk=seg(1).astype(BF16), wvt=seg(2).T.astype(BF16), wdn=seg(3).astype(BF16),
        wg=seg(4).astype(BF16), wba=jnp.pad(ba, ((0, 0), (0, LANES - 2 * N_GATES))).astype(BF16),
        wbat=ba.T.astype(BF16), ws5=seg(7).astype(BF16),
        qn_col=p["q_norm_w"][l][:, None], kn_row=p["k_norm_w"][l][None],
        conv_w=jnp.pad(p["dn_conv_w"][l], ((0, SUBLANES - DN_CONV), (0, 0))),
        alog_row=alog[None], dtb_row=dtb[None], alog_col=alog[:2 * N_GATES, None], dtb_col=dtb[:2 * N_GATES, None],
        dnw=p["dn_out_norm_w"][l][None], wglu=p["s5_w_glu"][l].astype(BF16), bglu=p["s5_b_glu"][l][None],
        wo_a=w_out[:ATTN_WIDTH].astype(BF16), wo_b=w_out[ATTN_WIDTH:ATTN_WIDTH + DN_WIDTH].astype(BF16),
        wo_c=w_out[ATTN_WIDTH + DN_WIDTH:].astype(BF16), wr=wr, br=br,
        w_gate=p["w_gate"][l].astype(BF16), w_up=p["w_up"][l].astype(BF16), w_down=p["w_down"][l].astype(BF16),
        s5=_s5_matrices(p["s5_a_re"][l], p["s5_a_im"][l], p["s5_log_dt"][l], p["s5_b_re"][l], p["s5_b_im"][l],
                        p["s5_c_re"][l], p["s5_c_im"][l], p["s5_d"][l]),
    )


TM_CTX = 256
TM_IN = 256
TM_SEQ = 512


def _mixers(x, mod, lw, tabs, rope, s_dn, s_s5, tm_in, tm):
    qt, k, vt, dn, gate, ba, bat, u = _in_proj(x, mod, lw, tabs, rope, tm_in)
    q_dn, k_dn, v_dn, gcol, grow = _gdn_prep(dn, ba, bat, lw, tm)
    o_f, o_b, s_dn = _gdn_scan(q_dn, k_dn, v_dn, gcol, grow, s_dn, tm)
    y, s_s5 = _s5_mixer(u, lw["s5"], s_s5)
    return dict(qt=qt, k=k, vt=vt, gate=gate, o_f=o_f, o_b=o_b, y=y), s_dn, s_s5


def kernel(x, c, ctx, c_ctx, w_ada, b_ada, norm1_w, norm2_w, w_in, q_norm_w, k_norm_w, dn_conv_w, dn_a_log, dn_dt_bias,
           dn_out_norm_w, s5_a_re, s5_a_im, s5_log_dt, s5_b_re, s5_b_im, s5_c_re, s5_c_im, s5_d, s5_w_glu, s5_b_glu,
           w_out, router_g_w, router_g_b, router_e_w, router_e_b, w_gate, w_up, w_down):
    p = dict(norm1_w=norm1_w, norm2_w=norm2_w, w_in=w_in, q_norm_w=q_norm_w, k_norm_w=k_norm_w, dn_conv_w=dn_conv_w,
             dn_a_log=dn_a_log, dn_dt_bias=dn_dt_bias, dn_out_norm_w=dn_out_norm_w, s5_a_re=s5_a_re, s5_a_im=s5_a_im,
             s5_log_dt=s5_log_dt, s5_b_re=s5_b_re, s5_b_im=s5_b_im, s5_c_re=s5_c_re, s5_c_im=s5_c_im, s5_d=s5_d,
             s5_w_glu=s5_w_glu, s5_b_glu=s5_b_glu, w_out=w_out, router_g_w=router_g_w, router_g_b=router_g_b,
             router_e_w=router_e_w, router_e_b=router_e_b, w_gate=w_gate, w_up=w_up, w_down=w_down)
    assert x.shape[0] == 1 and ctx.shape[0] == 1
    xl, xc = x[0], ctx[0]
    n_ctx = xc.shape[0]
    cvec = jnp.zeros((SUBLANES, D_MODEL), F32).at[0].set(c[0]).at[1].set(c_ctx)
    mods = _adaln_mod(cvec, w_ada, b_ada)
    tabs = _rope_tables(xl.shape[0])
    tabs_c = (tabs[0][:n_ctx], tabs[1][:n_ctx], tabs[2][:, :n_ctx], tabs[3][:, :n_ctx])
    names = ("sh1", "sc1", "g1", "sh2", "sc2", "g2")
    for l in range(DEPTH):
        lw = _layer_weights(l, p)
        mod_l = {n: mods[l, 0:1, i * D_MODEL:(i + 1) * D_MODEL] for i, n in enumerate(names)}
        mod_c = {n: mods[l, 1:2, i * D_MODEL:(i + 1) * D_MODEL] for i, n in enumerate(names)}
        s_dn0 = jnp.zeros((N_GATES, HEAD_DIM, HEAD_DIM), F32)
        s_s50 = jnp.zeros((2, S5_GROUPS * S5_HW), F32)
        mc, s_dn, s_s5 = _mixers(xc, mod_c, lw, tabs_c, False, s_dn0, s_s50, TM_CTX, TM_CTX)
        ml, _, _ = _mixers(xl, mod_l, lw, tabs, True, s_dn, s_s5, TM_IN, TM_SEQ)
        att_l = _attention(ml["qt"], jnp.concatenate([mc["k"], ml["k"]], axis=0),
                           jnp.concatenate([mc["vt"], ml["vt"]], axis=0))
        xl, route = _mix_out(att_l, ml["o_f"], ml["o_b"], ml["gate"], ml["y"], xl, mod_l, lw, TM_SEQ, False)
        xl = _moe_routed(xl, route, mod_l, lw)
        if l < DEPTH - 1:
            att_c = _attention(mc["qt"], mc["k"], mc["vt"])
            xc, comb, h2 = _mix_out(att_c, mc["o_f"], mc["o_b"], mc["gate"], mc["y"], xc, mod_c, lw, TM_CTX, True)
            xc = _moe(h2, comb, xc, mod_c["g2"], lw, TM_CTX)
    return xl[None]
```

```python
import functools
import math

import jax
import jax.numpy as jnp
from jax import lax
from jax.experimental import pallas as pl
from jax.experimental.pallas import tpu as pltpu

F32 = jnp.float32
BF16 = jnp.bfloat16

D_MODEL = 2048
DEPTH = 2
GRID_W = 64
N_DIR = 2
HEAD_DIM = 128
ATTN_WIDTH = D_MODEL // 2
ATTN_Q_HEADS = ATTN_WIDTH // HEAD_DIM
ATTN_KV_HEADS = ATTN_Q_HEADS // 4
KV_WIDTH = ATTN_KV_HEADS * HEAD_DIM
ROPE_THETA = 10000.0
DN_WIDTH = D_MODEL // 4
DN_HEADS = DN_WIDTH // HEAD_DIM
DN_CONV = 5
DN_CHUNK = 64
S5_WIDTH = D_MODEL // 4
S5_GROUP_CH = 16
S5_GROUPS = S5_WIDTH // S5_GROUP_CH
S5_STATE = 64
MIX_WIDTH = ATTN_WIDTH + DN_WIDTH + S5_WIDTH
IN_SPLITS = (ATTN_WIDTH, KV_WIDTH, KV_WIDTH, 3 * DN_WIDTH, DN_WIDTH, N_DIR * DN_HEADS, N_DIR * DN_HEADS, S5_WIDTH)
MOE_GROUPS = 4
MOE_PER_GROUP = 4
MOE_EXPERTS = MOE_GROUPS * MOE_PER_GROUP
MOE_HIDDEN = D_MODEL // 4
EPS = 1e-6

LANES = 128
SUBLANES = 8
VMEM_LIMIT_BYTES = 56 * 1024 * 1024

N_GATES = N_DIR * DN_HEADS
S5_CHUNK = 16
S5_CW = S5_CHUNK * S5_GROUP_CH


def _cparams(*sem):
    return pltpu.CompilerParams(dimension_semantics=sem, vmem_limit_bytes=VMEM_LIMIT_BYTES)


def _const_spec(shape):
    nd = len(shape)
    return pl.BlockSpec(shape, lambda *_: (0,) * nd, pipeline_mode=pl.Buffered(1))


def _silu(x):
    return x * jax.nn.sigmoid(x)


MOD_TN = 1024


def _mod_kernel(cv_ref, w_ref, b_ref, o_ref):
    s = _silu(cv_ref[...]).astype(BF16)
    o_ref[...] = jnp.dot(s, w_ref[...].astype(BF16), preferred_element_type=F32) + b_ref[...]


def _adaln_mod(cvec, w_ada, b_ada):
    n6 = w_ada.shape[-1]
    return pl.pallas_call(
        _mod_kernel,
        grid=(DEPTH, n6 // MOD_TN),
        in_specs=[
            pl.BlockSpec((SUBLANES, D_MODEL), lambda l, j: (0, 0)),
            pl.BlockSpec((None, D_MODEL, MOD_TN), lambda l, j: (l, 0, j)),
            pl.BlockSpec((None, 1, MOD_TN), lambda l, j: (l, 0, j)),
        ],
        out_specs=pl.BlockSpec((None, SUBLANES, MOD_TN), lambda l, j: (l, 0, j)),
        out_shape=jax.ShapeDtypeStruct((DEPTH, SUBLANES, n6), F32),
        compiler_params=_cparams("parallel", "parallel"),
        name="adaln_mod",
    )(cvec, w_ada, b_ada.reshape(DEPTH, 1, n6))


def _rms_rows(x):
    return x * lax.rsqrt(jnp.mean(x * x, axis=-1, keepdims=True) + EPS)


def _in_kernel(rope, x_ref, nw_ref, sh_ref, sc_ref, wqt_ref, wk_ref, wvt_ref, wdn_ref, wg_ref, wba_ref, wbat_ref,
               ws5_ref, qn_ref, kn_ref, cos_ref, sin_ref, cost_ref, sint_ref,
               qt_ref, k_ref, vt_ref, dn_ref, g_ref, ba_ref, bat_ref, u_ref):
    x = x_ref[...]
    h = (_rms_rows(x) * nw_ref[...] * (1.0 + sc_ref[...]) + sh_ref[...]).astype(BF16)
    nt = (((1,), (1,)), ((), ()))

    qt = lax.dot_general(wqt_ref[...], h, nt, preferred_element_type=F32)
    q_scale = HEAD_DIM ** -0.5 * math.log2(math.e)
    for hd in range(ATTN_Q_HEADS):
        qh = qt[hd * HEAD_DIM:(hd + 1) * HEAD_DIM, :]
        qh = qh * lax.rsqrt(jnp.mean(qh * qh, axis=0, keepdims=True) + EPS) * qn_ref[...]
        if rope:
            a, b, c, d = (qh[i * 32:(i + 1) * 32, :] for i in range(4))
            rot = jnp.concatenate([-b, a, -d, c], axis=0)
            qh = qh * cost_ref[...] + rot * sint_ref[...]
        qt_ref[hd * HEAD_DIM:(hd + 1) * HEAD_DIM, :] = (qh * q_scale).astype(BF16)

    k = jnp.dot(h, wk_ref[...], preferred_element_type=F32)
    lane = lax.broadcasted_iota(jnp.int32, (1, HEAD_DIM), 1)
    first_half = (lane % 64) < 32
    for hd in range(ATTN_KV_HEADS):
        kh = _rms_rows(k[:, hd * HEAD_DIM:(hd + 1) * HEAD_DIM]) * kn_ref[...]
        if rope:
            rot = jnp.where(first_half, -pltpu.roll(kh, 96, 1), pltpu.roll(kh, 32, 1))
            kh = kh * cos_ref[...] + rot * sin_ref[...]
        k_ref[:, hd * HEAD_DIM:(hd + 1) * HEAD_DIM] = kh.astype(BF16)

    vt_ref[...] = lax.dot_general(wvt_ref[...], h, nt, preferred_element_type=F32).astype(BF16)
    dn_ref[...] = jnp.dot(h, wdn_ref[...], preferred_element_type=F32)
    g_ref[...] = jnp.dot(h, wg_ref[...], preferred_element_type=F32)
    ba_ref[...] = jnp.dot(h, wba_ref[...], preferred_element_type=F32)
    bat_ref[...] = lax.dot_general(wbat_ref[...], h, nt, preferred_element_type=F32)
    u_ref[...] = jnp.dot(h, ws5_ref[...], preferred_element_type=F32).astype(BF16)


def _in_proj(x, mod, lw, rope_tabs, rope, tm):
    t = x.shape[0]
    cos, sin, cost, sint = rope_tabs
    row = lambda w: pl.BlockSpec((tm, w), lambda i: (i, 0))
    col = lambda w: pl.BlockSpec((w, tm), lambda i: (0, i))
    vec = lambda a: _const_spec(a.shape)
    ws = [lw["wqt"], lw["wk"], lw["wvt"], lw["wdn"], lw["wg"], lw["wba"], lw["wbat"], lw["ws5"]]
    in_specs = ([row(D_MODEL), vec(lw["norm1_w"]), vec(mod["sh1"]), vec(mod["sc1"])] + [vec(w) for w in ws]
                + [vec(lw["qn_col"]), vec(lw["kn_row"]), row(HEAD_DIM), row(HEAD_DIM), col(HEAD_DIM), col(HEAD_DIM)])
    out_specs = [col(ATTN_WIDTH), row(KV_WIDTH), pl.BlockSpec((None, KV_WIDTH, tm), lambda i: (i, 0, 0)), row(3 * DN_WIDTH), row(DN_WIDTH), row(LANES),
                 col(2 * N_GATES), row(S5_WIDTH)]
    sds = jax.ShapeDtypeStruct
    out_shape = [sds((ATTN_WIDTH, t), BF16), sds((t, KV_WIDTH), BF16), sds((t // tm, KV_WIDTH, tm), BF16),
                 sds((t, 3 * DN_WIDTH), F32), sds((t, DN_WIDTH), F32), sds((t, LANES), F32),
                 sds((2 * N_GATES, t), F32), sds((t, S5_WIDTH), BF16)]
    return pl.pallas_call(
        functools.partial(_in_kernel, rope),
        grid=(t // tm,),
        in_specs=in_specs, out_specs=out_specs, out_shape=out_shape,
        compiler_params=_cparams("parallel"),
        name="in_proj",
    )(x, lw["norm1_w"], mod["sh1"], mod["sc1"], *ws, lw["qn_col"], lw["kn_row"], cos, sin, cost, sint)


ATT_TQ = 256
ATT_TK = 256
GQA = ATTN_Q_HEADS // ATTN_KV_HEADS


ATT_UNROLL = 4
ATT_ONES = 16


def _attn_kernel(nj, qt_ref, k_ref, vt_ref, ot_ref, sa_ref, sb_ref, m_ref, acc_ref):
    tq = qt_ref.shape[1]
    q = jnp.concatenate([qt_ref[h * HEAD_DIM:(h + 1) * HEAD_DIM, :] for h in range(GQA)], axis=1)
    ones = jnp.ones((ATT_ONES, ATT_TK), BF16)

    def scores(j, dst_ref):
        kb = k_ref[pl.ds(pl.multiple_of(j * ATT_TK, ATT_TK), ATT_TK), :]
        dst_ref[...] = jnp.dot(kb, q, preferred_element_type=F32)

    def consume(j, src_ref):
        vt = jnp.concatenate([vt_ref[j], ones], axis=0)
        for h in range(GQA):
            cols = slice(h * tq, (h + 1) * tq)
            s = src_ref[:, cols]
            m_old = m_ref[:, cols]
            m_new = jnp.maximum(m_old, jnp.max(s, axis=0, keepdims=True))
            p = jnp.exp2((s - m_new).astype(BF16))
            acc_ref[:, cols] = (jnp.exp2(m_old - m_new) * acc_ref[:, cols]
                                + jnp.dot(vt, p, preferred_element_type=F32))
            m_ref[:, cols] = m_new

    m_ref[...] = jnp.full(m_ref.shape, -jnp.inf, F32)
    acc_ref[...] = jnp.zeros(acc_ref.shape, F32)
    scores(0, sa_ref)

    def pair(i, carry):
        j = 2 * i
        scores(j + 1, sb_ref)
        consume(j, sa_ref)
        scores(jnp.minimum(j + 2, nj - 1), sa_ref)
        consume(j + 1, sb_ref)
        return carry

    lax.fori_loop(0, nj // 2, pair, 0, unroll=min(ATT_UNROLL, max(nj // 2, 1)))
    if nj % 2:
        consume(nj - 1, sa_ref)
    acc = acc_ref[...]
    o = acc[:HEAD_DIM] * (1.0 / acc[HEAD_DIM:HEAD_DIM + 1])
    for h in range(GQA):
        ot_ref[h * HEAD_DIM:(h + 1) * HEAD_DIM, :] = o[:, h * tq:(h + 1) * tq].astype(BF16)


def _attention(qt, k_all, vt_all):
    t = qt.shape[1]
    tk_total = k_all.shape[0]
    nj = tk_total // ATT_TK
    tq = min(ATT_TQ, t)
    return pl.pallas_call(
        functools.partial(_attn_kernel, nj),
        grid=(ATTN_KV_HEADS, t // tq),
        in_specs=[
            pl.BlockSpec((GQA * HEAD_DIM, tq), lambda g, i: (g, i)),
            pl.BlockSpec((tk_total, HEAD_DIM), lambda g, i: (0, g)),
            pl.BlockSpec((nj, HEAD_DIM, ATT_TK), lambda g, i: (0, g, 0)),
        ],
        out_specs=pl.BlockSpec((GQA * HEAD_DIM, tq), lambda g, i: (g, i)),
        out_shape=jax.ShapeDtypeStruct((ATTN_WIDTH, t), BF16),
        scratch_shapes=[pltpu.VMEM((ATT_TK, GQA * tq), F32), pltpu.VMEM((ATT_TK, GQA * tq), F32),
                        pltpu.VMEM((1, GQA * tq), F32), pltpu.VMEM((HEAD_DIM + ATT_ONES, GQA * tq), F32)],
        compiler_params=_cparams("parallel", "parallel"),
        name="attention",
    )(qt, k_all, vt_all)


def _softplus(x):
    return jnp.maximum(x, 0.0) + jnp.log(1.0 + jnp.exp(-jnp.abs(x)))


def _dot_f32(a, b):
    return jnp.dot(a, b, precision=lax.Precision.HIGHEST, preferred_element_type=F32)


def _gdn_prep_kernel(nblk, cur_ref, prev_ref, next_ref, cw_ref, ba_ref, bat_ref, alog_r, dtb_r, alog_c, dtb_c,
                     q_ref, k_ref, v_ref, gcol_ref, grow_ref, xx_ref):
    i = pl.program_id(0)
    tm = cur_ref.shape[0]
    pad = (DN_CONV - 1) // 2
    xx_ref[0:SUBLANES, :] = jnp.where(i > 0, prev_ref[...], 0.0)
    xx_ref[SUBLANES:SUBLANES + tm, :] = cur_ref[...]
    xx_ref[SUBLANES + tm:2 * SUBLANES + tm, :] = jnp.where(i < nblk - 1, next_ref[...], 0.0)
    acc = None
    for j in range(DN_CONV):
        term = xx_ref[SUBLANES - pad + j:SUBLANES - pad + j + tm, :] * cw_ref[j:j + 1, :]
        acc = term if acc is None else acc + term
    y = _silu(acc)
    for hd in range(DN_HEADS):
        sl = slice(hd * HEAD_DIM, (hd + 1) * HEAD_DIM)
        qh = y[:, sl]
        q_ref[:, sl] = qh * (lax.rsqrt(jnp.sum(qh * qh, axis=-1, keepdims=True) + EPS) * HEAD_DIM ** -0.5)
        kh = y[:, DN_WIDTH + hd * HEAD_DIM:DN_WIDTH + (hd + 1) * HEAD_DIM]
        k_ref[:, sl] = kh * lax.rsqrt(jnp.sum(kh * kh, axis=-1, keepdims=True) + EPS)
    v_ref[...] = y[:, 2 * DN_WIDTH:]

    r = lax.broadcasted_iota(jnp.int32, (tm, tm), 0)
    c = lax.broadcasted_iota(jnp.int32, (tm, tm), 1)
    same = (r // DN_CHUNK) == (c // DN_CHUNK)
    tri_le = jnp.where(same & (c <= r), 1.0, 0.0)
    tri_ge = jnp.where(same & (c >= r), 1.0, 0.0)

    ba = ba_ref[...]
    lane = lax.broadcasted_iota(jnp.int32, (1, LANES), 1)
    g = -jnp.exp(alog_r[...]) * _softplus(ba + dtb_r[...])
    gc = jnp.where(lane < N_GATES + DN_HEADS, _dot_f32(tri_le, g), _dot_f32(tri_ge, g))
    gcol_ref[...] = jnp.where(lane < N_GATES, jax.nn.sigmoid(ba), gc)

    gt = -jnp.exp(alog_c[...]) * _softplus(bat_ref[...] + dtb_c[...])
    row = lax.broadcasted_iota(jnp.int32, (2 * N_GATES, 1), 0)
    gct = jnp.where(row < N_GATES + DN_HEADS, _dot_f32(gt, tri_ge), _dot_f32(gt, tri_le))
    for ch in range(tm // DN_CHUNK):
        grow_ref[ch] = gct[N_GATES:, ch * DN_CHUNK:(ch + 1) * DN_CHUNK]


def _gdn_prep(dn, ba, bat, lw, tm):
    t = dn.shape[0]
    nblk = t // tm
    nsub = tm // SUBLANES
    w3 = 3 * DN_WIDTH
    row = lambda w: pl.BlockSpec((tm, w), lambda i: (i, 0))
    vec = lambda a: _const_spec(a.shape)
    sds = jax.ShapeDtypeStruct
    return pl.pallas_call(
        functools.partial(_gdn_prep_kernel, nblk),
        grid=(nblk,),
        in_specs=[row(w3),
                  pl.BlockSpec((SUBLANES, w3), lambda i: (jnp.maximum(i * nsub - 1, 0), 0)),
                  pl.BlockSpec((SUBLANES, w3), lambda i: (jnp.minimum((i + 1) * nsub, t // SUBLANES - 1), 0)),
                  vec(lw["conv_w"]), row(LANES), pl.BlockSpec((2 * N_GATES, tm), lambda i: (0, i)),
                  vec(lw["alog_row"]), vec(lw["dtb_row"]), vec(lw["alog_col"]), vec(lw["dtb_col"])],
        out_specs=[row(DN_WIDTH), row(DN_WIDTH), row(DN_WIDTH), row(LANES),
                   pl.BlockSpec((tm // DN_CHUNK, N_GATES, DN_CHUNK), lambda i: (i, 0, 0))],
        out_shape=[sds((t, DN_WIDTH), F32), sds((t, DN_WIDTH), F32), sds((t, DN_WIDTH), F32), sds((t, LANES), F32),
                   sds((t // DN_CHUNK, N_GATES, DN_CHUNK), F32)],
        scratch_shapes=[pltpu.VMEM((tm + 2 * SUBLANES, w3), F32)],
        compiler_params=_cparams("parallel"),
        name="gdn_prep",
    )(dn, dn, dn, lw["conv_w"], ba, bat, lw["alog_row"], lw["dtb_row"], lw["alog_col"], lw["dtb_col"])


def _dot_bf(a, b, dims=(((1,), (0,)), ((), ()))):
    return lax.dot_general(a.astype(BF16), b.astype(BF16), dims, preferred_element_type=F32)


_NT = (((1,), (1,)), ((), ()))
_TN = (((0,), (0,)), ((), ()))
TRI_BASE = 8


def _gdn_local(chains):
    c = DN_CHUNK
    ri = lax.broadcasted_iota(jnp.int32, (c, c), 0)
    ci = lax.broadcasted_iota(jnp.int32, (c, c), 1)
    blk = lambda b: (ri // b) == (ci // b)
    eye = jnp.where(ri == ci, 1.0, 0.0)
    n = range(len(chains))
    qs, ks, vs, betas, gcs, grs, fwds = zip(*chains)
    incl = [(ri >= ci) if f else (ri <= ci) for f in fwds]
    strict = [(ri > ci) if f else (ri < ci) for f in fwds]
    decay = [jnp.where(incl[i], jnp.exp(jnp.where(incl[i], gcs[i] - grs[i], 0.0)), 0.0) for i in n]
    glast = [grs[i][:, c - 1:c] if fwds[i] else grs[i][:, 0:1] for i in n]
    eg = [jnp.exp(g) for g in gcs]
    kb = [ks[i] * betas[i] for i in n]
    kbf = [k.astype(BF16) for k in ks]
    a = [jnp.where(strict[i], _dot_bf(kb[i], kbf[i], _NT) * decay[i], 0.0) for i in n]
    qk = [(_dot_bf(qs[i], kbf[i], _NT) * decay[i]).astype(BF16) for i in n]
    p = [jnp.where(blk(TRI_BASE), a[i], 0.0) for i in n]
    x = [eye - p[i] for i in n]
    for _ in range(int(math.log2(TRI_BASE)) - 1):
        p = [_dot_bf(p[i], p[i]) for i in n]
        x = [x[i] + _dot_bf(x[i], p[i]) for i in n]
    b = 2 * TRI_BASE
    while b <= c:
        off = blk(b) & jnp.logical_not(blk(b // 2))
        lx = [_dot_bf(jnp.where(off, a[i], 0.0), x[i]) for i in n]
        x = [x[i] - _dot_bf(x[i], lx[i]) for i in n]
        b *= 2
    sol = [_dot_bf(x[i], jnp.concatenate([vs[i] * betas[i], kb[i] * eg[i]], axis=1)) for i in n]
    return [(sol[i][:, :HEAD_DIM],
             jnp.concatenate([sol[i][:, HEAD_DIM:], qs[i] * eg[i]], axis=0).astype(BF16),
             (ks[i] * jnp.exp(glast[i] - gcs[i])).astype(BF16),
             qk[i]) for i in n]


GDN_LOCAL_CHUNKS = 2


def _gdn_kernel(nblk, qf, kf, vf, gcf, grf, qb, kb, vb, gcb, grb, s0_ref, of_ref, ob_ref, sout_ref,
                s_ref, u_ref, wq_ref, kd_ref, qk_ref):
    i = pl.program_id(0)
    c = DN_CHUNK

    @pl.when(i == 0)
    def _():
        s_ref[...] = s0_ref[...]

    nb = qf.shape[0] // c
    dirs = ((qf, kf, vf, gcf, grf), (qb, kb, vb, gcb, grb))

    def local_body(jp, carry):
        work = []
        for cc in range(GDN_LOCAL_CHUNKS):
            ch = jp * GDN_LOCAL_CHUNKS + cc
            rows = pl.ds(pl.multiple_of(ch * c, c), c)
            for d, (q_ref, k_ref, v_ref, gc_ref, gr_ref) in enumerate(dirs):
                gcol = gc_ref[rows, :]
                grow = gr_ref[ch]
                for h in range(DN_HEADS):
                    gi = d * DN_HEADS + h
                    sl = slice(h * HEAD_DIM, (h + 1) * HEAD_DIM)
                    work.append((ch, gi, (q_ref[rows, sl], k_ref[rows, sl], v_ref[rows, sl], gcol[:, gi:gi + 1],
                                          gcol[:, N_GATES + gi:N_GATES + gi + 1], grow[gi:gi + 1, :], d == 0)))
        done = _gdn_local([args for _, _, args in work])
        for (ch, gi, _), (u, wq, kd, qk) in zip(work, done):
            u_ref[ch, gi] = u
            wq_ref[ch, gi] = wq
            kd_ref[ch, gi] = kd
            qk_ref[ch, gi] = qk
        return carry

    lax.fori_loop(0, nb // GDN_LOCAL_CHUNKS, local_body, 0)

    def seq_body(j, carry):
        work = []
        for d, (gr_ref, o_ref) in enumerate(((grf, of_ref), (grb, ob_ref))):
            jj = j if d == 0 else nb - 1 - j
            grow = gr_ref[jj]
            for h in range(DN_HEADS):
                gi = d * DN_HEADS + h
                glast = grow[gi:gi + 1, c - 1:c] if d == 0 else grow[gi:gi + 1, 0:1]
                work.append((o_ref, jj, h, gi, s_ref[gi], u_ref[jj, gi], wq_ref[jj, gi], kd_ref[jj, gi], qk_ref[jj, gi],
                             jnp.exp(glast)))
        ws = [jnp.dot(w[6], w[4].astype(BF16), preferred_element_type=F32) for w in work]
        v_new = [(w[5] - ws_[:c]).astype(BF16) for w, ws_ in zip(work, ws)]
        o = [ws_[c:] + jnp.dot(w[8], vn, preferred_element_type=F32) for w, ws_, vn in zip(work, ws, v_new)]
        s_new = [w[4] * w[9] + lax.dot_general(w[7], vn, _TN, preferred_element_type=F32) for w, vn in zip(work, v_new)]
        for (o_ref, jj, h, gi, *_), o_, s_ in zip(work, o, s_new):
            o_ref[pl.ds(pl.multiple_of(jj * c, c), c), h * HEAD_DIM:(h + 1) * HEAD_DIM] = o_
            s_ref[gi] = s_
        return carry

    lax.fori_loop(0, nb, seq_body, 0)

    @pl.when(i == nblk - 1)
    def _():
        sout_ref[...] = s_ref[...]


def _gdn_scan(q, k, v, gcol, grow, s0, tm):
    t = q.shape[0]
    nblk = t // tm
    nch = tm // DN_CHUNK
    fwd = lambda w: pl.BlockSpec((tm, w), lambda i: (i, 0))
    bwd = lambda w: pl.BlockSpec((tm, w), lambda i: (nblk - 1 - i, 0))
    sds = jax.ShapeDtypeStruct
    s_shape = (N_GATES, HEAD_DIM, HEAD_DIM)
    return pl.pallas_call(
        functools.partial(_gdn_kernel, nblk),
        grid=(nblk,),
        in_specs=[fwd(DN_WIDTH), fwd(DN_WIDTH), fwd(DN_WIDTH), fwd(LANES),
                  pl.BlockSpec((nch, N_GATES, DN_CHUNK), lambda i: (i, 0, 0)),
                  bwd(DN_WIDTH), bwd(DN_WIDTH), bwd(DN_WIDTH), bwd(LANES),
                  pl.BlockSpec((nch, N_GATES, DN_CHUNK), lambda i: (nblk - 1 - i, 0, 0)),
                  _const_spec(s_shape)],
        out_specs=[fwd(DN_WIDTH), bwd(DN_WIDTH), pl.BlockSpec(s_shape, lambda i: (0, 0, 0))],
        out_shape=[sds((t, DN_WIDTH), F32), sds((t, DN_WIDTH), F32), sds(s_shape, F32)],
        scratch_shapes=[pltpu.VMEM(s_shape, F32),
                        pltpu.VMEM((nch, N_GATES, DN_CHUNK, HEAD_DIM), F32),
                        pltpu.VMEM((nch, N_GATES, 2 * DN_CHUNK, HEAD_DIM), BF16),
                        pltpu.VMEM((nch, N_GATES, DN_CHUNK, HEAD_DIM), BF16),
                        pltpu.VMEM((nch, N_GATES, DN_CHUNK, DN_CHUNK), BF16)],
        compiler_params=_cparams("arbitrary"),
        name="gdn_scan",
    )(q, k, v, gcol, grow, q, k, v, gcol, grow, s0)


S5_SW = 4 * S5_STATE


def _s5_matrices(a_re, a_im, log_dt, b_re, b_im, c_re, c_im, d):
    L, G, P, H = S5_CHUNK, S5_GROUPS, S5_STATE, S5_GROUP_CH
    hi = lax.Precision.HIGHEST
    a_re, a_im = a_re.astype(F32), a_im.astype(F32)
    dt = jnp.exp(log_dt.astype(F32))[..., None]
    n_re = jnp.exp(a_re * dt) * jnp.cos(a_im * dt) - 1.0
    n_im = jnp.exp(a_re * dt) * jnp.sin(a_im * dt)
    den = a_re * a_re + a_im * a_im
    co_re = ((n_re * a_re + n_im * a_im) / den)[..., None]
    co_im = ((n_im * a_re - n_re * a_im) / den)[..., None]
    br, bi = b_re.astype(F32)[None], b_im.astype(F32)[None]
    bb_re, bb_im = co_re * br - co_im * bi, co_re * bi + co_im * br
    tau = jnp.arange(L + 1, dtype=F32)[:, None, None, None]
    mag = jnp.exp(a_re[None] * dt[None] * tau)
    ang = a_im[None] * dt[None] * tau
    pw_re, pw_im = mag * jnp.cos(ang), mag * jnp.sin(ang)
    cr, ci = c_re.astype(F32), c_im.astype(F32)
    e_re = cr[None] * pw_re[:, :, :, None, :] - ci[None] * pw_im[:, :, :, None, :]
    e_im = cr[None] * pw_im[:, :, :, None, :] + ci[None] * pw_re[:, :, :, None, :]
    kk = (jnp.einsum('tdghp,dgpi->tdghi', e_re[:L], bb_re, precision=hi)
          - jnp.einsum('tdghp,dgpi->tdghi', e_im[:L], bb_im, precision=hi))
    s_i = jnp.arange(L)[:, None]
    t_i = jnp.arange(L)[None, :]
    lag = t_i - s_i
    kf = jnp.where((lag >= 0)[..., None, None, None], kk[jnp.clip(lag, 0, L - 1), 0], 0.0)
    kb = jnp.where((lag <= 0)[..., None, None, None], kk[jnp.clip(-lag, 0, L - 1), 1], 0.0)
    dskip = d.astype(F32).reshape(G, H)
    eye_t = (lag == 0).astype(F32)
    skip = eye_t[:, :, None, None, None] * (dskip[:, :, None] * jnp.eye(H, dtype=F32)[None])[None, None]
    toep = (kf + kb + skip).transpose(2, 0, 4, 1, 3).reshape(G, L * H, L * H)

    def loc(pr, pi, dr):
        w_re = pr[..., None] * bb_re[dr][None] - pi[..., None] * bb_im[dr][None]
        w_im = pr[..., None] * bb_im[dr][None] + pi[..., None] * bb_re[dr][None]
        f = lambda w: w.transpose(1, 0, 3, 2).reshape(G, L * H, P)
        return f(w_re), f(w_im)
    wf_re, wf_im = loc(pw_re[:L, 0][::-1], pw_im[:L, 0][::-1], 0)
    wb_re, wb_im = loc(pw_re[:L, 1], pw_im[:L, 1], 1)
    wcat = jnp.concatenate([toep, wf_re, wb_re, wf_im, wb_im], axis=-1).astype(BF16)

    carry = lambda e: e.transpose(1, 3, 0, 2).reshape(G, P, L * H)
    mf_re, mf_im = carry(e_re[1:, 0]), carry(-e_im[1:, 0])
    mb_re, mb_im = carry(e_re[1:, 1][::-1]), carry(-e_im[1:, 1][::-1])
    mcat = jnp.concatenate([mf_re, mb_re, mf_im, mb_im], axis=1).astype(BF16)
    al_re = jnp.concatenate([pw_re[L, 0], pw_re[L, 1]], axis=-1)
    al_im = jnp.concatenate([pw_im[L, 0], pw_im[L, 1]], axis=-1)
    al = jnp.stack([al_re.reshape(-1), al_im.reshape(-1)])
    return wcat, mcat, al


S5_HW = 2 * S5_STATE


def _s5_local_kernel(u_ref, w_ref, y_ref, xre_ref, xim_ref):
    r = jnp.dot(u_ref[...], w_ref[...], preferred_element_type=F32)
    y_ref[...] = r[:, :S5_CW]
    xre_ref[...] = r[:, S5_CW:S5_CW + S5_HW]
    xim_ref[...] = r[:, S5_CW + S5_HW:]


def _s5_local(ug, wcat):
    g, n, _ = ug.shape
    out = lambda w: pl.BlockSpec((n, w), lambda i: (0, i))
    sds = jax.ShapeDtypeStruct
    return pl.pallas_call(
        _s5_local_kernel,
        grid=(g,),
        in_specs=[pl.BlockSpec((None, n, S5_CW), lambda i: (i, 0, 0)),
                  pl.BlockSpec((None, S5_CW, S5_CW + S5_SW), lambda i: (i, 0, 0))],
        out_specs=[out(S5_CW), out(S5_HW), out(S5_HW)],
        out_shape=[sds((n, g * S5_CW), F32), sds((n, g * S5_HW), F32), sds((n, g * S5_HW), F32)],
        compiler_params=_cparams("parallel"),
        name="s5_local",
    )(ug, wcat)


def _s5_scan_kernel(nblk, xfr_ref, xfi_ref, xbr_ref, xbi_ref, al_ref, s0_ref,
                    cfr_ref, cfi_ref, cbr_ref, cbi_ref, sout_ref, s_ref):
    i = pl.program_id(0)

    @pl.when(i == 0)
    def _():
        s_ref[...] = s0_ref[...]

    nb = xfr_ref.shape[0]
    al_re, al_im = al_ref[0:1, :], al_ref[1:2, :]
    is_fwd = (lax.broadcasted_iota(jnp.int32, (1, al_ref.shape[1]), 1) % S5_HW) < S5_STATE

    def body(j, carry):
        re, im = carry
        rf, rb = pl.ds(j, 1), pl.ds(nb - 1 - j, 1)
        cfr_ref[rf, :] = re
        cfi_ref[rf, :] = im
        cbr_ref[rb, :] = re
        cbi_ref[rb, :] = im
        in_re = jnp.where(is_fwd, xfr_ref[rf, :], xbr_ref[rb, :])
        in_im = jnp.where(is_fwd, xfi_ref[rf, :], xbi_ref[rb, :])
        return al_re * re - al_im * im + in_re, al_re * im + al_im * re + in_im

    re, im = lax.fori_loop(0, nb, body, (s_ref[0:1, :], s_ref[1:2, :]))
    s_ref[0:1, :] = re
    s_ref[1:2, :] = im

    @pl.when(i == nblk - 1)
    def _():
        sout_ref[...] = s_ref[...]


def _s5_scan(x_re, x_im, al, s0, nb):
    n, w = x_re.shape
    nblk = n // nb
    fwd = pl.BlockSpec((nb, w), lambda i: (i, 0))
    bwd = pl.BlockSpec((nb, w), lambda i: (nblk - 1 - i, 0))
    sds = jax.ShapeDtypeStruct
    return pl.pallas_call(
        functools.partial(_s5_scan_kernel, nblk),
        grid=(nblk,),
        in_specs=[fwd, fwd, bwd, bwd, _const_spec(al.shape), _const_spec(s0.shape)],
        out_specs=[fwd, fwd, bwd, bwd, pl.BlockSpec(s0.shape, lambda i: (0, 0))],
        out_shape=[sds((n, w), F32)] * 4 + [sds(s0.shape, F32)],
        scratch_shapes=[pltpu.VMEM(s0.shape, F32)],
        compiler_params=_cparams("arbitrary"),
        name="s5_scan",
    )(x_re, x_im, x_re, x_im, al, s0)


def _s5_carry_kernel(y_ref, cfr_ref, cfi_ref, cbr_ref, cbi_ref, m_ref, o_ref):
    is_fwd = lax.broadcasted_iota(jnp.int32, (1, S5_HW), 1) < S5_STATE
    cin = jnp.concatenate([jnp.where(is_fwd, cfr_ref[...], cbr_ref[...]),
                           jnp.where(is_fwd, cfi_ref[...], cbi_ref[...])], axis=1).astype(BF16)
    o_ref[...] = y_ref[...] + jnp.dot(cin, m_ref[...], preferred_element_type=F32)


def _s5_carry(y, cins, mcat):
    n = y.shape[0]
    g = mcat.shape[0]
    blk = lambda w: pl.BlockSpec((n, w), lambda i: (0, i))
    return pl.pallas_call(
        _s5_carry_kernel,
        grid=(g,),
        in_specs=[blk(S5_CW)] + [blk(S5_HW)] * 4 + [pl.BlockSpec((None, S5_SW, S5_CW), lambda i: (i, 0, 0))],
        out_specs=blk(S5_CW),
        out_shape=jax.ShapeDtypeStruct(y.shape, F32),
        compiler_params=_cparams("parallel"),
        name="s5_carry",
    )(y, *cins, mcat)


def _s5_mixer(u, mats, s0):
    wcat, mcat, al = mats
    t = u.shape[0]
    n = t // S5_CHUNK
    ug = u.reshape(n, S5_CHUNK, S5_GROUPS, S5_GROUP_CH).transpose(2, 0, 1, 3).reshape(S5_GROUPS, n, S5_CW)
    y_loc, x_re, x_im = _s5_local(ug, wcat)
    *cins, s_fin = _s5_scan(x_re, x_im, al, s0, min(n, 128))
    y = _s5_carry(y_loc, cins, mcat)
    y = y.reshape(n, S5_GROUPS, S5_CHUNK, S5_GROUP_CH).transpose(0, 2, 1, 3).reshape(t, S5_WIDTH)
    return y, s_fin


ROUTER_E0 = MOE_GROUPS
ROUTE_SEL = 32


def _gelu_tanh(x):
    return 0.5 * x * (1.0 + jnp.tanh(math.sqrt(2.0 / math.pi) * (x + 0.044715 * (x * x * x))))


def _route(logits):
    ninf = float("-inf")
    lane = lax.broadcasted_iota(jnp.int32, (1, LANES), 1)
    lanef = lane.astype(F32)
    first = lambda hit: jnp.min(jnp.where(hit, lanef, float(LANES)), axis=-1, keepdims=True)
    gl = jnp.where(lane < MOE_GROUPS, logits, ninf)
    gmax = jnp.max(gl, axis=-1, keepdims=True)
    pg_top = 1.0 / jnp.sum(jnp.exp(gl - gmax), axis=-1, keepdims=True)
    base = ROUTER_E0 + MOE_PER_GROUP * first(gl == gmax)
    el = jnp.where((lanef >= base) & (lanef < base + MOE_PER_GROUP), logits, ninf)
    emax = jnp.max(el, axis=-1, keepdims=True)
    esum = jnp.sum(jnp.exp(el - emax), axis=-1, keepdims=True)
    i1 = first(el == emax)
    el2 = jnp.where(lanef == i1, ninf, el)
    emax2 = jnp.max(el2, axis=-1, keepdims=True)
    i2 = first(el2 == emax2)
    p1 = 1.0 / esum
    p2 = jnp.exp(emax2 - emax) / esum
    w1 = pg_top * (p1 / (p1 + p2))
    w2 = pg_top * (p2 / (p1 + p2))
    comb = jnp.where(lanef == i1, w1, 0.0) + jnp.where(lanef == i2, w2, 0.0)
    sel = (jnp.where(lane == ROUTE_SEL, i1 - ROUTER_E0, 0.0) + jnp.where(lane == ROUTE_SEL + 1, i2 - ROUTER_E0, 0.0)
           + jnp.where(lane == ROUTE_SEL + 2, w1, 0.0) + jnp.where(lane == ROUTE_SEL + 3, w2, 0.0))
    return comb + sel


def _bucket_rank(route, carry):
    tm = route.shape[0]
    lanef = lax.broadcasted_iota(jnp.int32, (1, LANES), 1).astype(F32)
    e1, e2 = route[:, ROUTE_SEL:ROUTE_SEL + 1], route[:, ROUTE_SEL + 1:ROUTE_SEL + 2]
    lo, hi = jnp.minimum(e1, e2), jnp.maximum(e1, e2)
    grp = jnp.floor(lo * (1.0 / MOE_PER_GROUP))
    a, b = lo - MOE_PER_GROUP * grp, hi - MOE_PER_GROUP * grp
    bucket = grp * MOE_PAIRS + (a * (2 * MOE_PER_GROUP - 1 - a) * 0.5 + (b - a - 1.0))
    onehot = jnp.where(lanef == bucket, 1.0, 0.0)
    r = lax.broadcasted_iota(jnp.int32, (tm, tm), 0)
    c = lax.broadcasted_iota(jnp.int32, (tm, tm), 1)
    before = jnp.where(c < r, 1.0, 0.0).astype(BF16)
    seen = jnp.dot(before, onehot.astype(BF16), preferred_element_type=F32) + carry
    rank = jnp.sum(onehot * seen, axis=-1, keepdims=True)
    lane = lax.broadcasted_iota(jnp.int32, (1, LANES), 1)
    route = route + jnp.where(lane == ROUTE_SEL + 4, bucket, 0.0) + jnp.where(lane == ROUTE_SEL + 5, rank, 0.0)
    return route, carry + jnp.sum(onehot, axis=0, keepdims=True)


def _out_kernel(routed, att_ref, of_ref, ob_ref, gate_ref, y_ref, x_ref, g1_ref, sh2_ref, sc2_ref, n2_ref, dnw_ref,
                bglu_ref, wglu_ref, woa_ref, wob_ref, woc_ref, wr_ref, br_ref, xo_ref, comb_ref, aux_ref, cnt_ref=None):
    o = of_ref[...] + ob_ref[...]
    gate = _silu(gate_ref[...])
    dn = jnp.concatenate(
        [_rms_rows(o[:, h * HEAD_DIM:(h + 1) * HEAD_DIM]) * dnw_ref[...] for h in range(DN_HEADS)], axis=1) * gate
    z = _gelu_tanh(y_ref[...])
    s5 = z * jax.nn.sigmoid(jnp.dot(z.astype(BF16), wglu_ref[...], preferred_element_type=F32) + bglu_ref[...])
    mix = (lax.dot_general(att_ref[...], woa_ref[...], _TN, preferred_element_type=F32)
           + jnp.dot(dn.astype(BF16), wob_ref[...], preferred_element_type=F32)
           + jnp.dot(s5.astype(BF16), woc_ref[...], preferred_element_type=F32))
    x = x_ref[...] + g1_ref[...] * mix
    h = _rms_rows(x) * n2_ref[...] * (1.0 + sc2_ref[...]) + sh2_ref[...]
    route = _route(_dot_f32(h, wr_ref[...]) + br_ref[...])
    if not routed:
        xo_ref[...] = x
        aux_ref[...] = h.astype(BF16)
        comb_ref[...] = route
        return
    i = pl.program_id(0)

    @pl.when(i == 0)
    def _():
        cnt_ref[...] = jnp.zeros(cnt_ref.shape, F32)

    route, cnt = _bucket_rank(route, cnt_ref[0:1, :])
    cnt_ref[0:1, :] = cnt
    xo_ref[:, :D_MODEL] = x
    xo_ref[:, D_MODEL:] = route
    comb_ref[...] = route
    aux_ref[...] = cnt_ref[...]


def _mix_out(att_t, o_f, o_b, gate, y, x, mod, lw, tm, routed):
    t = x.shape[0]
    row = lambda w: pl.BlockSpec((tm, w), lambda i: (i, 0))
    vec = lambda a: _const_spec(a.shape)
    consts = [mod["g1"], mod["sh2"], mod["sc2"], lw["norm2_w"], lw["dnw"], lw["bglu"], lw["wglu"], lw["wo_a"],
              lw["wo_b"], lw["wo_c"], lw["wr"], lw["br"]]
    sds = jax.ShapeDtypeStruct
    if routed:
        out_specs = [row(D_MODEL + LANES), row(LANES), pl.BlockSpec((SUBLANES, LANES), lambda i: (0, 0))]
        out_shape = [sds((t, D_MODEL + LANES), F32), sds((t, LANES), F32), sds((SUBLANES, LANES), F32)]
        scratch = [pltpu.VMEM((SUBLANES, LANES), F32)]
    else:
        out_specs = [row(D_MODEL), row(LANES), row(D_MODEL)]
        out_shape = [sds((t, D_MODEL), F32), sds((t, LANES), F32), sds((t, D_MODEL), BF16)]
        scratch = []
    return pl.pallas_call(
        functools.partial(_out_kernel, routed),
        grid=(t // tm,),
        in_specs=[pl.BlockSpec((ATTN_WIDTH, tm), lambda i: (0, i)), row(DN_WIDTH), row(DN_WIDTH), row(DN_WIDTH),
                  row(S5_WIDTH), row(D_MODEL)] + [vec(a) for a in consts],
        out_specs=out_specs, out_shape=out_shape, scratch_shapes=scratch,
        compiler_params=_cparams("arbitrary" if routed else "parallel"),
        name="mix_out",
    )(att_t, o_f, o_b, gate, y, x, *consts)


def _moe_kernel(h_ref, comb_ref, x_ref, g2_ref, wg_ref, wu_ref, wd_ref, o_ref, acc_ref):
    e = pl.program_id(1)

    @pl.when(e == 0)
    def _():
        acc_ref[...] = jnp.zeros(acc_ref.shape, F32)

    h = h_ref[...]
    mid = _silu(jnp.dot(h, wg_ref[...], preferred_element_type=F32)) * jnp.dot(h, wu_ref[...], preferred_element_type=F32)
    lane = lax.broadcasted_iota(jnp.int32, (1, LANES), 1)
    cw = jnp.sum(jnp.where(lane == e + ROUTER_E0, comb_ref[...], 0.0), axis=-1, keepdims=True)
    acc_ref[...] += cw * jnp.dot(mid.astype(BF16), wd_ref[...], preferred_element_type=F32)

    @pl.when(e == MOE_EXPERTS - 1)
    def _():
        o_ref[...] = x_ref[...] + g2_ref[...] * acc_ref[...]


def _moe(h, comb, x, g2, lw, tm):
    t = x.shape[0]
    row = lambda w: pl.BlockSpec((tm, w), lambda i, e: (i, 0))
    return pl.pallas_call(
        _moe_kernel,
        grid=(t // tm, MOE_EXPERTS),
        in_specs=[row(D_MODEL), row(LANES), row(D_MODEL), _const_spec(g2.shape),
                  pl.BlockSpec((None, D_MODEL, MOE_HIDDEN), lambda i, e: (e, 0, 0)),
                  pl.BlockSpec((None, D_MODEL, MOE_HIDDEN), lambda i, e: (e, 0, 0)),
                  pl.BlockSpec((None, MOE_HIDDEN, D_MODEL), lambda i, e: (e, 0, 0))],
        out_specs=row(D_MODEL),
        out_shape=jax.ShapeDtypeStruct((t, D_MODEL), F32),
        scratch_shapes=[pltpu.VMEM((tm, D_MODEL), F32)],
        compiler_params=_cparams("parallel", "arbitrary"),
        name="moe",
    )(h, comb, x, g2, lw["w_gate"], lw["w_up"], lw["w_down"])


MOE_PAIRS = MOE_PER_GROUP * (MOE_PER_GROUP - 1) // 2
MOE_BUCKETS = MOE_GROUPS * MOE_PAIRS
MOE_TM = 128
MOE_SCATTER_GROUP = 8


def _moe_plan(route, counts, t):
    n_tiles = t // MOE_TM + MOE_BUCKETS
    bucket = route[:, ROUTE_SEL + 4].astype(jnp.int32)
    rank = route[:, ROUTE_SEL + 5].astype(jnp.int32)
    counts = counts[0, :MOE_BUCKETS].astype(jnp.int32)
    tiles = (counts + MOE_TM - 1) // MOE_TM
    tile_end = jnp.cumsum(tiles)
    tile_off = tile_end - tiles
    slot = tile_off[bucket] * MOE_TM + rank
    tok = jnp.zeros(((n_tiles + 1) * MOE_TM,), jnp.int32).at[slot].set(jnp.arange(t, dtype=jnp.int32))
    tile = jnp.arange(n_tiles)
    tb = jnp.minimum(jnp.sum((tile[:, None] >= tile_end[None, :]).astype(jnp.int32), axis=1), MOE_BUCKETS - 1)
    nvalid = jnp.clip(counts[tb] - (tile - tile_off[tb]) * MOE_TM, 0, MOE_TM)
    nvalid = jnp.where(tile < tile_end[-1], nvalid, 0).astype(jnp.int32)
    pairs = [(i, j) for i in range(MOE_PER_GROUP) for j in range(i + 1, MOE_PER_GROUP)]
    pa = jnp.array([p[0] for p in pairs], jnp.int32)
    pb = jnp.array([p[1] for p in pairs], jnp.int32)
    grp = (tb // MOE_PAIRS) * MOE_PER_GROUP
    e_lo = (grp + pa[tb % MOE_PAIRS]).astype(jnp.int32)
    e_hi = (grp + pb[tb % MOE_PAIRS]).astype(jnp.int32)
    return tok, nvalid, e_lo, e_hi


def _moe_routed_kernel(n_tiles, tok_ref, nv_ref, elo_ref, ehi_ref,
                       x_hbm, g2_ref, sh2_ref, sc2_ref, n2_ref, wgl_ref, wul_ref, wdl_ref,
                       wgh_ref, wuh_ref, wdh_ref, o_hbm, xbuf, obuf, gsem, ssem):
    t = pl.program_id(0)
    slot = lax.rem(t, 2)

    def row_copy(tile, s, r, scatter):
        tok = tok_ref[tile * MOE_TM + r]
        if scatter:
            return pltpu.make_async_copy(obuf.at[s, pl.ds(r, 1)], o_hbm.at[pl.ds(tok, 1)], ssem.at[s])
        return pltpu.make_async_copy(x_hbm.at[pl.ds(tok, 1)], xbuf.at[s, pl.ds(r, 1)], gsem.at[s])

    def start_gather(tile, s):
        for r in range(MOE_TM):
            row_copy(tile, s, r, False).start()

    def wait_gather(s):
        pltpu.make_async_copy(x_hbm.at[pl.ds(0, MOE_TM)], xbuf.at[s], gsem.at[s]).wait()

    def start_scatter(tile, s):
        n = nv_ref[tile]
        for g in range(MOE_TM // MOE_SCATTER_GROUP):
            @pl.when((g + 1) * MOE_SCATTER_GROUP <= n)
            def _():
                for r in range(g * MOE_SCATTER_GROUP, (g + 1) * MOE_SCATTER_GROUP):
                    row_copy(tile, s, r, True).start()

        def body(r, carry):
            row_copy(tile, s, r, True).start()
            return carry
        lax.fori_loop(n - lax.rem(n, MOE_SCATTER_GROUP), n, body, 0)

    def wait_scatter(tile, s):
        n = nv_ref[tile]
        for bit in range(MOE_TM.bit_length()):
            rows = 1 << bit

            @pl.when((n >> bit) & 1 == 1)
            def _():
                pltpu.make_async_copy(obuf.at[s, pl.ds(0, rows)], o_hbm.at[pl.ds(0, rows)], ssem.at[s]).wait()

    @pl.when(t == 0)
    def _():
        start_gather(0, 0)

    wait_gather(slot)

    @pl.when(t >= 2)
    def _():
        wait_scatter(t - 2, slot)

    start_gather(t + 1, 1 - slot)
    x = xbuf[slot, :, :D_MODEL]
    rec = xbuf[slot, :, D_MODEL:]
    h = (_rms_rows(x) * n2_ref[...] * (1.0 + sc2_ref[...]) + sh2_ref[...]).astype(BF16)
    top1_is_lo = rec[:, ROUTE_SEL:ROUTE_SEL + 1] == elo_ref[t].astype(F32)
    w1, w2 = rec[:, ROUTE_SEL + 2:ROUTE_SEL + 3], rec[:, ROUTE_SEL + 3:ROUTE_SEL + 4]
    w = (jnp.where(top1_is_lo, w1, w2), jnp.where(top1_is_lo, w2, w1))
    y = None
    for k, (wg_ref, wu_ref, wd_ref) in enumerate(((wgl_ref, wul_ref, wdl_ref), (wgh_ref, wuh_ref, wdh_ref))):
        mid = (_silu(jnp.dot(h, wg_ref[...], preferred_element_type=F32))
               * jnp.dot(h, wu_ref[...], preferred_element_type=F32))
        yk = w[k] * jnp.dot(mid.astype(BF16), wd_ref[...], preferred_element_type=F32)
        y = yk if y is None else y + yk
    obuf[slot] = x + g2_ref[...] * y
    start_scatter(t, slot)

    @pl.when(t == n_tiles - 1)
    def _():
        wait_gather(1 - slot)
        if n_tiles >= 2:
            wait_scatter(t - 1, 1 - slot)
        wait_scatter(t, slot)


def _moe_routed(x, route, counts, mod, lw):
    t = x.shape[0]
    tok, nvalid, e_lo, e_hi = _moe_plan(route, counts, t)
    n_tiles = nvalid.shape[0]
    vec = lambda a: pl.BlockSpec(a.shape, lambda i, *_: (0,) * a.ndim, pipeline_mode=pl.Buffered(1))
    wspec = lambda shape, which: pl.BlockSpec(
        (None,) + shape, (lambda i, tok, nv, elo, ehi: (elo[i], 0, 0)) if which == 0
        else (lambda i, tok, nv, elo, ehi: (ehi[i], 0, 0)))
    up, down = (D_MODEL, MOE_HIDDEN), (MOE_HIDDEN, D_MODEL)
    consts = [mod["g2"], mod["sh2"], mod["sc2"], lw["norm2_w"]]
    grid_spec = pltpu.PrefetchScalarGridSpec(
        num_scalar_prefetch=4,
        grid=(n_tiles,),
        in_specs=[pl.BlockSpec(memory_space=pl.ANY)] + [vec(a) for a in consts]
                 + [wspec(up, 0), wspec(up, 0), wspec(down, 0), wspec(up, 1), wspec(up, 1), wspec(down, 1)],
        out_specs=pl.BlockSpec(memory_space=pl.ANY),
        scratch_shapes=[pltpu.VMEM((2, MOE_TM, D_MODEL + LANES), F32), pltpu.VMEM((2, MOE_TM, D_MODEL), F32),
                        pltpu.SemaphoreType.DMA((2,)), pltpu.SemaphoreType.DMA((2,))],
    )
    return pl.pallas_call(
        functools.partial(_moe_routed_kernel, n_tiles),
        grid_spec=grid_spec,
        out_shape=jax.ShapeDtypeStruct((t, D_MODEL), F32),
        compiler_params=_cparams("arbitrary"),
        name="moe_routed",
    )(tok, nvalid, e_lo, e_hi, x, *consts, lw["w_gate"], lw["w_up"], lw["w_down"],
      lw["w_gate"], lw["w_up"], lw["w_down"])


def _rope_tables(n_tokens):
    t = jnp.arange(n_tokens)
    row = (t // GRID_W).astype(F32)
    col = (t % GRID_W).astype(F32)
    half = HEAD_DIM // 2
    inv = ROPE_THETA ** (-jnp.arange(0, half, 2, dtype=F32) / half)
    ang_r = row[:, None] * inv[None, :]
    ang_c = col[:, None] * inv[None, :]
    ang = jnp.concatenate([ang_r, ang_r, ang_c, ang_c], axis=-1)
    cos, sin = jnp.cos(ang), jnp.sin(ang)
    return cos, sin, cos.T, sin.T


def _layer_weights(l, p):
    o = [0]
    for s in IN_SPLITS:
        o.append(o[-1] + s)
    w_in = p["w_in"][l]
    seg = lambda i: w_in[:, o[i]:o[i + 1]]
    ba = jnp.concatenate([seg(5), seg(6)], axis=1)
    lanes16 = lambda v: jnp.zeros((LANES,), F32).at[N_GATES:2 * N_GATES].set(v.reshape(N_GATES).astype(F32))
    alog, dtb = lanes16(p["dn_a_log"][l]), lanes16(p["dn_dt_bias"][l])
    w_out = p["w_out"][l]
    wr = jnp.zeros((D_MODEL, LANES), F32)
    wr = wr.at[:, :MOE_GROUPS].set(p["router_g_w"][l]).at[:, ROUTER_E0:ROUTER_E0 + MOE_EXPERTS].set(p["router_e_w"][l])
    br = jnp.zeros((1, LANES), F32)
    br = br.at[0, :MOE_GROUPS].set(p["router_g_b"][l]).at[0, ROUTER_E0:ROUTER_E0 + MOE_EXPERTS].set(p["router_e_b"][l])
    return dict(
        norm1_w=p["norm1_w"][l][None], norm2_w=p["norm2_w"][l][None],
        wqt=seg(0).T.astype(BF16), wk=seg(1).astype(BF16), wvt=seg(2).T.astype(BF16), wdn=seg(3).astype(BF16),
        wg=seg(4).astype(BF16), wba=jnp.pad(ba, ((0, 0), (0, LANES - 2 * N_GATES))).astype(BF16),
        wbat=ba.T.astype(BF16), ws5=seg(7).astype(BF16),
        qn_col=p["q_norm_w"][l][:, None], kn_row=p["k_norm_w"][l][None],
        conv_w=jnp.pad(p["dn_conv_w"][l], ((0, SUBLANES - DN_CONV), (0, 0))),
        alog_row=alog[None], dtb_row=dtb[None], alog_col=alog[:2 * N_GATES, None], dtb_col=dtb[:2 * N_GATES, None],
        dnw=p["dn_out_norm_w"][l][None], wglu=p["s5_w_glu"][l].astype(BF16), bglu=p["s5_b_glu"][l][None],
        wo_a=w_out[:ATTN_WIDTH].astype(BF16), wo_b=w_out[ATTN_WIDTH:ATTN_WIDTH + DN_WIDTH].astype(BF16),
        wo_c=w_out[ATTN_WIDTH + DN_WIDTH:].astype(BF16), wr=wr, br=br,
        w_gate=p["w_gate"][l].astype(BF16), w_up=p["w_up"][l].astype(BF16), w_down=p["w_down"][l].astype(BF16),
        s5=_s5_matrices(p["s5_a_re"][l], p["s5_a_im"][l], p["s5_log_dt"][l], p["s5_b_re"][l], p["s5_b_im"][l],
                        p["s5_c_re"][l], p["s5_c_im"][l], p["s5_d"][l]),
    )


TM_CTX = 256
TM_IN = 256
TM_SEQ = 512


def _mixers(x, mod, lw, tabs, rope, s_dn, s_s5, tm_in, tm):
    qt, k, vt, dn, gate, ba, bat, u = _in_proj(x, mod, lw, tabs, rope, tm_in)
    q_dn, k_dn, v_dn, gcol, grow = _gdn_prep(dn, ba, bat, lw, tm)
    o_f, o_b, s_dn = _gdn_scan(q_dn, k_dn, v_dn, gcol, grow, s_dn, tm)
    y, s_s5 = _s5_mixer(u, lw["s5"], s_s5)
    return dict(qt=qt, k=k, vt=vt, gate=gate, o_f=o_f, o_b=o_b, y=y), s_dn, s_s5


def kernel(x, c, ctx, c_ctx, w_ada, b_ada, norm1_w, norm2_w, w_in, q_norm_w, k_norm_w, dn_conv_w, dn_a_log, dn_dt_bias,
           dn_out_norm_w, s5_a_re, s5_a_im, s5_log_dt, s5_b_re, s5_b_im, s5_c_re, s5_c_im, s5_d, s5_w_glu, s5_b_glu,
           w_out, router_g_w, router_g_b, router_e_w, router_e_b, w_gate, w_up, w_down):
    p = dict(norm1_w=norm1_w, norm2_w=norm2_w, w_in=w_in, q_norm_w=q_norm_w, k_norm_w=k_norm_w, dn_conv_w=dn_conv_w,
             dn_a_log=dn_a_log, dn_dt_bias=dn_dt_bias, dn_out_norm_w=dn_out_norm_w, s5_a_re=s5_a_re, s5_a_im=s5_a_im,
             s5_log_dt=s5_log_dt, s5_b_re=s5_b_re, s5_b_im=s5_b_im, s5_c_re=s5_c_re, s5_c_im=s5_c_im, s5_d=s5_d,
             s5_w_glu=s5_w_glu, s5_b_glu=s5_b_glu, w_out=w_out, router_g_w=router_g_w, router_g_b=router_g_b,
             router_e_w=router_e_w, router_e_b=router_e_b, w_gate=w_gate, w_up=w_up, w_down=w_down)
    assert x.shape[0] == 1 and ctx.shape[0] == 1
    xl, xc = x[0], ctx[0]
    n_ctx = xc.shape[0]
    cvec = jnp.zeros((SUBLANES, D_MODEL), F32).at[0].set(c[0]).at[1].set(c_ctx)
    mods = _adaln_mod(cvec, w_ada, b_ada)
    tabs = _rope_tables(xl.shape[0])
    tabs_c = (tabs[0][:n_ctx], tabs[1][:n_ctx], tabs[2][:, :n_ctx], tabs[3][:, :n_ctx])
    names = ("sh1", "sc1", "g1", "sh2", "sc2", "g2")
    for l in range(DEPTH):
        lw = _layer_weights(l, p)
        mod_l = {n: mods[l, 0:1, i * D_MODEL:(i + 1) * D_MODEL] for i, n in enumerate(names)}
        mod_c = {n: mods[l, 1:2, i * D_MODEL:(i + 1) * D_MODEL] for i, n in enumerate(names)}
        s_dn0 = jnp.zeros((N_GATES, HEAD_DIM, HEAD_DIM), F32)
        s_s50 = jnp.zeros((2, S5_GROUPS * S5_HW), F32)
        mc, s_dn, s_s5 = _mixers(xc, mod_c, lw, tabs_c, False, s_dn0, s_s50, TM_CTX, TM_CTX)
        ml, _, _ = _mixers(xl, mod_l, lw, tabs, True, s_dn, s_s5, TM_IN, TM_SEQ)
        att_l = _attention(ml["qt"], jnp.concatenate([mc["k"], ml["k"]], axis=0),
                           jnp.concatenate([mc["vt"], ml["vt"]], axis=0))
        xl, route, counts = _mix_out(att_l, ml["o_f"], ml["o_b"], ml["gate"], ml["y"], xl, mod_l, lw, TM_SEQ, True)
        xl = _moe_routed(xl, route, counts, mod_l, lw)
        if l < DEPTH - 1:
            att_c = _attention(mc["qt"], mc["k"], mc["vt"])
            xc, comb, h2 = _mix_out(att_c, mc["o_f"], mc["o_b"], mc["gate"], mc["y"], xc, mod_c, lw, TM_CTX, False)
            xc = _moe(h2, comb, xc, mod_c["g2"], lw, TM_CTX)
    return xl[None]
```

```python
import functools
import math

import jax
import jax.numpy as jnp
from jax import lax
from jax.experimental import pallas as pl
from jax.experimental.pallas import tpu as pltpu

F32 = jnp.float32
BF16 = jnp.bfloat16

D_MODEL = 2048
DEPTH = 2
GRID_W = 64
N_DIR = 2
HEAD_DIM = 128
ATTN_WIDTH = D_MODEL // 2
ATTN_Q_HEADS = ATTN_WIDTH // HEAD_DIM
ATTN_KV_HEADS = ATTN_Q_HEADS // 4
KV_WIDTH = ATTN_KV_HEADS * HEAD_DIM
ROPE_THETA = 10000.0
DN_WIDTH = D_MODEL // 4
DN_HEADS = DN_WIDTH // HEAD_DIM
DN_CONV = 5
DN_CHUNK = 64
S5_WIDTH = D_MODEL // 4
S5_GROUP_CH = 16
S5_GROUPS = S5_WIDTH // S5_GROUP_CH
S5_STATE = 64
MIX_WIDTH = ATTN_WIDTH + DN_WIDTH + S5_WIDTH
IN_SPLITS = (ATTN_WIDTH, KV_WIDTH, KV_WIDTH, 3 * DN_WIDTH, DN_WIDTH, N_DIR * DN_HEADS, N_DIR * DN_HEADS, S5_WIDTH)
MOE_GROUPS = 4
MOE_PER_GROUP = 4
MOE_EXPERTS = MOE_GROUPS * MOE_PER_GROUP
MOE_HIDDEN = D_MODEL // 4
EPS = 1e-6

LANES = 128
SUBLANES = 8
VMEM_LIMIT_BYTES = 56 * 1024 * 1024

N_GATES = N_DIR * DN_HEADS
S5_CHUNK = 16
S5_CW = S5_CHUNK * S5_GROUP_CH


def _cparams(*sem):
    return pltpu.CompilerParams(dimension_semantics=sem, vmem_limit_bytes=VMEM_LIMIT_BYTES)


def _const_spec(shape):
    nd = len(shape)
    return pl.BlockSpec(shape, lambda *_: (0,) * nd, pipeline_mode=pl.Buffered(1))


def _silu(x):
    return x * jax.nn.sigmoid(x)


MOD_TN = 1024


def _mod_kernel(cv_ref, w_ref, b_ref, o_ref):
    s = _silu(cv_ref[...]).astype(BF16)
    o_ref[...] = jnp.dot(s, w_ref[...].astype(BF16), preferred_element_type=F32) + b_ref[...]


def _adaln_mod(cvec, w_ada, b_ada):
    n6 = w_ada.shape[-1]
    return pl.pallas_call(
        _mod_kernel,
        grid=(DEPTH, n6 // MOD_TN),
        in_specs=[
            pl.BlockSpec((SUBLANES, D_MODEL), lambda l, j: (0, 0)),
            pl.BlockSpec((None, D_MODEL, MOD_TN), lambda l, j: (l, 0, j)),
            pl.BlockSpec((None, 1, MOD_TN), lambda l, j: (l, 0, j)),
        ],
        out_specs=pl.BlockSpec((None, SUBLANES, MOD_TN), lambda l, j: (l, 0, j)),
        out_shape=jax.ShapeDtypeStruct((DEPTH, SUBLANES, n6), F32),
        compiler_params=_cparams("parallel", "parallel"),
        name="adaln_mod",
    )(cvec, w_ada, b_ada.reshape(DEPTH, 1, n6))


def _rms_rows(x):
    return x * lax.rsqrt(jnp.mean(x * x, axis=-1, keepdims=True) + EPS)


def _in_kernel(rope, x_ref, nw_ref, sh_ref, sc_ref, wqt_ref, wk_ref, wvt_ref, wdn_ref, wg_ref, wba_ref, wbat_ref,
               ws5_ref, qn_ref, kn_ref, cos_ref, sin_ref, cost_ref, sint_ref,
               qt_ref, k_ref, vt_ref, dn_ref, g_ref, ba_ref, bat_ref, u_ref):
    x = x_ref[...]
    h = (_rms_rows(x) * nw_ref[...] * (1.0 + sc_ref[...]) + sh_ref[...]).astype(BF16)
    nt = (((1,), (1,)), ((), ()))

    qt = lax.dot_general(wqt_ref[...], h, nt, preferred_element_type=F32)
    q_scale = HEAD_DIM ** -0.5 * math.log2(math.e)
    for hd in range(ATTN_Q_HEADS):
        qh = qt[hd * HEAD_DIM:(hd + 1) * HEAD_DIM, :]
        qh = qh * lax.rsqrt(jnp.mean(qh * qh, axis=0, keepdims=True) + EPS) * qn_ref[...]
        if rope:
            a, b, c, d = (qh[i * 32:(i + 1) * 32, :] for i in range(4))
            rot = jnp.concatenate([-b, a, -d, c], axis=0)
            qh = qh * cost_ref[...] + rot * sint_ref[...]
        qt_ref[hd * HEAD_DIM:(hd + 1) * HEAD_DIM, :] = (qh * q_scale).astype(BF16)

    k = jnp.dot(h, wk_ref[...], preferred_element_type=F32)
    lane = lax.broadcasted_iota(jnp.int32, (1, HEAD_DIM), 1)
    first_half = (lane % 64) < 32
    for hd in range(ATTN_KV_HEADS):
        kh = _rms_rows(k[:, hd * HEAD_DIM:(hd + 1) * HEAD_DIM]) * kn_ref[...]
        if rope:
            rot = jnp.where(first_half, -pltpu.roll(kh, 96, 1), pltpu.roll(kh, 32, 1))
            kh = kh * cos_ref[...] + rot * sin_ref[...]
        k_ref[:, hd * HEAD_DIM:(hd + 1) * HEAD_DIM] = kh.astype(BF16)

    vt_ref[...] = lax.dot_general(wvt_ref[...], h, nt, preferred_element_type=F32).astype(BF16)
    dn_ref[...] = jnp.dot(h, wdn_ref[...], preferred_element_type=F32)
    g_ref[...] = jnp.dot(h, wg_ref[...], preferred_element_type=F32)
    ba_ref[...] = jnp.dot(h, wba_ref[...], preferred_element_type=F32)
    bat_ref[...] = lax.dot_general(wbat_ref[...], h, nt, preferred_element_type=F32)
    u_ref[...] = jnp.dot(h, ws5_ref[...], preferred_element_type=F32).astype(BF16)


def _in_proj(x, mod, lw, rope_tabs, rope, tm):
    t = x.shape[0]
    cos, sin, cost, sint = rope_tabs
    row = lambda w: pl.BlockSpec((tm, w), lambda i: (i, 0))
    col = lambda w: pl.BlockSpec((w, tm), lambda i: (0, i))
    vec = lambda a: _const_spec(a.shape)
    ws = [lw["wqt"], lw["wk"], lw["wvt"], lw["wdn"], lw["wg"], lw["wba"], lw["wbat"], lw["ws5"]]
    in_specs = ([row(D_MODEL), vec(lw["norm1_w"]), vec(mod["sh1"]), vec(mod["sc1"])] + [vec(w) for w in ws]
                + [vec(lw["qn_col"]), vec(lw["kn_row"]), row(HEAD_DIM), row(HEAD_DIM), col(HEAD_DIM), col(HEAD_DIM)])
    out_specs = [col(ATTN_WIDTH), row(KV_WIDTH), pl.BlockSpec((None, KV_WIDTH, tm), lambda i: (i, 0, 0)), row(3 * DN_WIDTH), row(DN_WIDTH), row(LANES),
                 col(2 * N_GATES), row(S5_WIDTH)]
    sds = jax.ShapeDtypeStruct
    out_shape = [sds((ATTN_WIDTH, t), BF16), sds((t, KV_WIDTH), BF16), sds((t // tm, KV_WIDTH, tm), BF16),
                 sds((t, 3 * DN_WIDTH), F32), sds((t, DN_WIDTH), F32), sds((t, LANES), F32),
                 sds((2 * N_GATES, t), F32), sds((t, S5_WIDTH), BF16)]
    return pl.pallas_call(
        functools.partial(_in_kernel, rope),
        grid=(t // tm,),
        in_specs=in_specs, out_specs=out_specs, out_shape=out_shape,
        compiler_params=_cparams("parallel"),
        name="in_proj",
    )(x, lw["norm1_w"], mod["sh1"], mod["sc1"], *ws, lw["qn_col"], lw["kn_row"], cos, sin, cost, sint)


ATT_TQ = 256
ATT_TK = 256
GQA = ATTN_Q_HEADS // ATTN_KV_HEADS


ATT_UNROLL = 4
ATT_ONES = 16


def _attn_kernel(n_lat, qt_ref, kc_ref, vtc_ref, *refs):
    if n_lat:
        kl_ref, vtl_ref, ot_ref, sa_ref, sb_ref, m_ref, acc_ref = refs
    else:
        ot_ref, sa_ref, sb_ref, m_ref, acc_ref = refs
    tq = qt_ref.shape[1]
    q = jnp.concatenate([qt_ref[h * HEAD_DIM:(h + 1) * HEAD_DIM, :] for h in range(GQA)], axis=1)
    ones = jnp.ones((ATT_ONES, ATT_TK), BF16)

    def scores(kb, dst_ref):
        dst_ref[...] = jnp.dot(kb, q, preferred_element_type=F32)

    def lat_keys(j):
        return kl_ref[pl.ds(pl.multiple_of(j * ATT_TK, ATT_TK), ATT_TK), :]

    def consume(vt_chunk, src_ref):
        vt = jnp.concatenate([vt_chunk, ones], axis=0)
        for h in range(GQA):
            cols = slice(h * tq, (h + 1) * tq)
            s = src_ref[:, cols]
            m_old = m_ref[:, cols]
            m_new = jnp.maximum(m_old, jnp.max(s, axis=0, keepdims=True))
            p = jnp.exp2((s - m_new).astype(BF16))
            acc_ref[:, cols] = (jnp.exp2(m_old - m_new) * acc_ref[:, cols]
                                + jnp.dot(vt, p, preferred_element_type=F32))
            m_ref[:, cols] = m_new

    m_ref[...] = jnp.full(m_ref.shape, -jnp.inf, F32)
    acc_ref[...] = jnp.zeros(acc_ref.shape, F32)
    scores(kc_ref[...], sa_ref)
    if n_lat:
        scores(lat_keys(0), sb_ref)
        consume(vtc_ref[0], sa_ref)

        def pair(i, carry):
            j = 2 * i
            scores(lat_keys(j + 1), sa_ref)
            consume(vtl_ref[j], sb_ref)
            scores(lat_keys(jnp.minimum(j + 2, n_lat - 1)), sb_ref)
            consume(vtl_ref[j + 1], sa_ref)
            return carry

        lax.fori_loop(0, n_lat // 2, pair, 0, unroll=min(ATT_UNROLL, n_lat // 2))
    else:
        consume(vtc_ref[0], sa_ref)
    acc = acc_ref[...]
    o = acc[:HEAD_DIM] * (1.0 / acc[HEAD_DIM:HEAD_DIM + 1])
    for h in range(GQA):
        ot_ref[h * HEAD_DIM:(h + 1) * HEAD_DIM, :] = o[:, h * tq:(h + 1) * tq].astype(BF16)


def _attention(qt, kv_ctx, kv_lat=None):
    t = qt.shape[1]
    tq = min(ATT_TQ, t)
    kv_specs = lambda k, vt: [pl.BlockSpec((k.shape[0], HEAD_DIM), lambda g, i: (0, g)),
                              pl.BlockSpec((vt.shape[0], HEAD_DIM, ATT_TK), lambda g, i: (0, g, 0))]
    assert kv_ctx[0].shape[0] == ATT_TK
    n_lat = 0
    kv = list(kv_ctx)
    specs = kv_specs(*kv_ctx)
    if kv_lat is not None:
        n_lat = kv_lat[0].shape[0] // ATT_TK
        assert n_lat % 2 == 0
        kv += list(kv_lat)
        specs += kv_specs(*kv_lat)
    return pl.pallas_call(
        functools.partial(_attn_kernel, n_lat),
        grid=(ATTN_KV_HEADS, t // tq),
        in_specs=[pl.BlockSpec((GQA * HEAD_DIM, tq), lambda g, i: (g, i))] + specs,
        out_specs=pl.BlockSpec((GQA * HEAD_DIM, tq), lambda g, i: (g, i)),
        out_shape=jax.ShapeDtypeStruct((ATTN_WIDTH, t), BF16),
        scratch_shapes=[pltpu.VMEM((ATT_TK, GQA * tq), F32), pltpu.VMEM((ATT_TK, GQA * tq), F32),
                        pltpu.VMEM((1, GQA * tq), F32), pltpu.VMEM((HEAD_DIM + ATT_ONES, GQA * tq), F32)],
        compiler_params=_cparams("parallel", "parallel"),
        name="attention",
    )(qt, *kv)


def _softplus(x):
    return jnp.maximum(x, 0.0) + jnp.log(1.0 + jnp.exp(-jnp.abs(x)))


def _dot_f32(a, b):
    return jnp.dot(a, b, precision=lax.Precision.HIGHEST, preferred_element_type=F32)


def _gdn_prep_kernel(nblk, cur_ref, prev_ref, next_ref, cw_ref, ba_ref, bat_ref, alog_r, dtb_r, alog_c, dtb_c,
                     q_ref, k_ref, v_ref, gcol_ref, grow_ref, xx_ref):
    i = pl.program_id(0)
    tm = cur_ref.shape[0]
    pad = (DN_CONV - 1) // 2
    xx_ref[0:SUBLANES, :] = jnp.where(i > 0, prev_ref[...], 0.0)
    xx_ref[SUBLANES:SUBLANES + tm, :] = cur_ref[...]
    xx_ref[SUBLANES + tm:2 * SUBLANES + tm, :] = jnp.where(i < nblk - 1, next_ref[...], 0.0)
    acc = None
    for j in range(DN_CONV):
        term = xx_ref[SUBLANES - pad + j:SUBLANES - pad + j + tm, :] * cw_ref[j:j + 1, :]
        acc = term if acc is None else acc + term
    y = _silu(acc)
    for hd in range(DN_HEADS):
        sl = slice(hd * HEAD_DIM, (hd + 1) * HEAD_DIM)
        qh = y[:, sl]
        q_ref[:, sl] = qh * (lax.rsqrt(jnp.sum(qh * qh, axis=-1, keepdims=True) + EPS) * HEAD_DIM ** -0.5)
        kh = y[:, DN_WIDTH + hd * HEAD_DIM:DN_WIDTH + (hd + 1) * HEAD_DIM]
        k_ref[:, sl] = kh * lax.rsqrt(jnp.sum(kh * kh, axis=-1, keepdims=True) + EPS)
    v_ref[...] = y[:, 2 * DN_WIDTH:]

    r = lax.broadcasted_iota(jnp.int32, (tm, tm), 0)
    c = lax.broadcasted_iota(jnp.int32, (tm, tm), 1)
    same = (r // DN_CHUNK) == (c // DN_CHUNK)
    tri_le = jnp.where(same & (c <= r), 1.0, 0.0)
    tri_ge = jnp.where(same & (c >= r), 1.0, 0.0)

    ba = ba_ref[...]
    lane = lax.broadcasted_iota(jnp.int32, (1, LANES), 1)
    g = -jnp.exp(alog_r[...]) * _softplus(ba + dtb_r[...])
    gc = jnp.where(lane < N_GATES + DN_HEADS, _dot_f32(tri_le, g), _dot_f32(tri_ge, g))
    gcol_ref[...] = jnp.where(lane < N_GATES, jax.nn.sigmoid(ba), gc)

    gt = -jnp.exp(alog_c[...]) * _softplus(bat_ref[...] + dtb_c[...])
    row = lax.broadcasted_iota(jnp.int32, (2 * N_GATES, 1), 0)
    gct = jnp.where(row < N_GATES + DN_HEADS, _dot_f32(gt, tri_ge), _dot_f32(gt, tri_le))
    for ch in range(tm // DN_CHUNK):
        grow_ref[ch] = gct[N_GATES:, ch * DN_CHUNK:(ch + 1) * DN_CHUNK]


def _gdn_prep(dn, ba, bat, lw, tm):
    t = dn.shape[0]
    nblk = t // tm
    nsub = tm // SUBLANES
    w3 = 3 * DN_WIDTH
    row = lambda w: pl.BlockSpec((tm, w), lambda i: (i, 0))
    vec = lambda a: _const_spec(a.shape)
    sds = jax.ShapeDtypeStruct
    return pl.pallas_call(
        functools.partial(_gdn_prep_kernel, nblk),
        grid=(nblk,),
        in_specs=[row(w3),
                  pl.BlockSpec((SUBLANES, w3), lambda i: (jnp.maximum(i * nsub - 1, 0), 0)),
                  pl.BlockSpec((SUBLANES, w3), lambda i: (jnp.minimum((i + 1) * nsub, t // SUBLANES - 1), 0)),
                  vec(lw["conv_w"]), row(LANES), pl.BlockSpec((2 * N_GATES, tm), lambda i: (0, i)),
                  vec(lw["alog_row"]), vec(lw["dtb_row"]), vec(lw["alog_col"]), vec(lw["dtb_col"])],
        out_specs=[row(DN_WIDTH), row(DN_WIDTH), row(DN_WIDTH), row(LANES),
                   pl.BlockSpec((tm // DN_CHUNK, N_GATES, DN_CHUNK), lambda i: (i, 0, 0))],
        out_shape=[sds((t, DN_WIDTH), F32), sds((t, DN_WIDTH), F32), sds((t, DN_WIDTH), F32), sds((t, LANES), F32),
                   sds((t // DN_CHUNK, N_GATES, DN_CHUNK), F32)],
        scratch_shapes=[pltpu.VMEM((tm + 2 * SUBLANES, w3), F32)],
        compiler_params=_cparams("parallel"),
        name="gdn_prep",
    )(dn, dn, dn, lw["conv_w"], ba, bat, lw["alog_row"], lw["dtb_row"], lw["alog_col"], lw["dtb_col"])


def _dot_bf(a, b, dims=(((1,), (0,)), ((), ()))):
    return lax.dot_general(a.astype(BF16), b.astype(BF16), dims, preferred_element_type=F32)


_NT = (((1,), (1,)), ((), ()))
_TN = (((0,), (0,)), ((), ()))
TRI_BASE = 8


def _gdn_local(chains):
    c = DN_CHUNK
    ri = lax.broadcasted_iota(jnp.int32, (c, c), 0)
    ci = lax.broadcasted_iota(jnp.int32, (c, c), 1)
    blk = lambda b: (ri // b) == (ci // b)
    eye = jnp.where(ri == ci, 1.0, 0.0)
    n = range(len(chains))
    qs, ks, vs, betas, gcs, grs, fwds = zip(*chains)
    incl = [(ri >= ci) if f else (ri <= ci) for f in fwds]
    strict = [(ri > ci) if f else (ri < ci) for f in fwds]
    decay = [jnp.where(incl[i], jnp.exp(jnp.where(incl[i], gcs[i] - grs[i], 0.0)), 0.0) for i in n]
    glast = [grs[i][:, c - 1:c] if fwds[i] else grs[i][:, 0:1] for i in n]
    eg = [jnp.exp(g) for g in gcs]
    kb = [ks[i] * betas[i] for i in n]
    kbf = [k.astype(BF16) for k in ks]
    a = [jnp.where(strict[i], _dot_bf(kb[i], kbf[i], _NT) * decay[i], 0.0) for i in n]
    qk = [(_dot_bf(qs[i], kbf[i], _NT) * decay[i]).astype(BF16) for i in n]
    p = [jnp.where(blk(TRI_BASE), a[i], 0.0) for i in n]
    x = [eye - p[i] for i in n]
    for _ in range(int(math.log2(TRI_BASE)) - 1):
        p = [_dot_bf(p[i], p[i]) for i in n]
        x = [x[i] + _dot_bf(x[i], p[i]) for i in n]
    b = 2 * TRI_BASE
    while b <= c:
        off = blk(b) & jnp.logical_not(blk(b // 2))
        lx = [_dot_bf(jnp.where(off, a[i], 0.0), x[i]) for i in n]
        x = [x[i] - _dot_bf(x[i], lx[i]) for i in n]
        b *= 2
    sol = [_dot_bf(x[i], jnp.concatenate([vs[i] * betas[i], kb[i] * eg[i]], axis=1)) for i in n]
    return [(sol[i][:, :HEAD_DIM],
             jnp.concatenate([sol[i][:, HEAD_DIM:], qs[i] * eg[i]], axis=0).astype(BF16),
             (ks[i] * jnp.exp(glast[i] - gcs[i])).astype(BF16),
             qk[i]) for i in n]


GDN_LOCAL_CHUNKS = 2


def _gdn_kernel(nblk, qf, kf, vf, gcf, grf, qb, kb, vb, gcb, grb, s0_ref, of_ref, ob_ref, sout_ref,
                s_ref, u_ref, wq_ref, kd_ref, qk_ref):
    i = pl.program_id(0)
    c = DN_CHUNK

    @pl.when(i == 0)
    def _():
        s_ref[...] = s0_ref[...]

    nb = qf.shape[0] // c
    dirs = ((qf, kf, vf, gcf, grf), (qb, kb, vb, gcb, grb))

    def local_body(jp, carry):
        work = []
        for cc in range(GDN_LOCAL_CHUNKS):
            ch = jp * GDN_LOCAL_CHUNKS + cc
            rows = pl.ds(pl.multiple_of(ch * c, c), c)
            for d, (q_ref, k_ref, v_ref, gc_ref, gr_ref) in enumerate(dirs):
                gcol = gc_ref[rows, :]
                grow = gr_ref[ch]
                for h in range(DN_HEADS):
                    gi = d * DN_HEADS + h
                    sl = slice(h * HEAD_DIM, (h + 1) * HEAD_DIM)
                    work.append((ch, gi, (q_ref[rows, sl], k_ref[rows, sl], v_ref[rows, sl], gcol[:, gi:gi + 1],
                                          gcol[:, N_GATES + gi:N_GATES + gi + 1], grow[gi:gi + 1, :], d == 0)))
        done = _gdn_local([args for _, _, args in work])
        for (ch, gi, _), (u, wq, kd, qk) in zip(work, done):
            u_ref[ch, gi] = u
            wq_ref[ch, gi] = wq
            kd_ref[ch, gi] = kd
            qk_ref[ch, gi] = qk
        return carry

    lax.fori_loop(0, nb // GDN_LOCAL_CHUNKS, local_body, 0)

    def seq_body(j, carry):
        work = []
        for d, (gr_ref, o_ref) in enumerate(((grf, of_ref), (grb, ob_ref))):
            jj = j if d == 0 else nb - 1 - j
            grow = gr_ref[jj]
            for h in range(DN_HEADS):
                gi = d * DN_HEADS + h
                glast = grow[gi:gi + 1, c - 1:c] if d == 0 else grow[gi:gi + 1, 0:1]
                work.append((o_ref, jj, h, gi, s_ref[gi], u_ref[jj, gi], wq_ref[jj, gi], kd_ref[jj, gi], qk_ref[jj, gi],
                             jnp.exp(glast)))
        ws = [jnp.dot(w[6], w[4].astype(BF16), preferred_element_type=F32) for w in work]
        v_new = [(w[5] - ws_[:c]).astype(BF16) for w, ws_ in zip(work, ws)]
        o = [ws_[c:] + jnp.dot(w[8], vn, preferred_element_type=F32) for w, ws_, vn in zip(work, ws, v_new)]
        s_new = [w[4] * w[9] + lax.dot_general(w[7], vn, _TN, preferred_element_type=F32) for w, vn in zip(work, v_new)]
        for (o_ref, jj, h, gi, *_), o_, s_ in zip(work, o, s_new):
            o_ref[pl.ds(pl.multiple_of(jj * c, c), c), h * HEAD_DIM:(h + 1) * HEAD_DIM] = o_
            s_ref[gi] = s_
        return carry

    lax.fori_loop(0, nb, seq_body, 0)

    @pl.when(i == nblk - 1)
    def _():
        sout_ref[...] = s_ref[...]


def _gdn_scan(q, k, v, gcol, grow, s0, tm):
    t = q.shape[0]
    nblk = t // tm
    nch = tm // DN_CHUNK
    fwd = lambda w: pl.BlockSpec((tm, w), lambda i: (i, 0))
    bwd = lambda w: pl.BlockSpec((tm, w), lambda i: (nblk - 1 - i, 0))
    sds = jax.ShapeDtypeStruct
    s_shape = (N_GATES, HEAD_DIM, HEAD_DIM)
    return pl.pallas_call(
        functools.partial(_gdn_kernel, nblk),
        grid=(nblk,),
        in_specs=[fwd(DN_WIDTH), fwd(DN_WIDTH), fwd(DN_WIDTH), fwd(LANES),
                  pl.BlockSpec((nch, N_GATES, DN_CHUNK), lambda i: (i, 0, 0)),
                  bwd(DN_WIDTH), bwd(DN_WIDTH), bwd(DN_WIDTH), bwd(LANES),
                  pl.BlockSpec((nch, N_GATES, DN_CHUNK), lambda i: (nblk - 1 - i, 0, 0)),
                  _const_spec(s_shape)],
        out_specs=[fwd(DN_WIDTH), bwd(DN_WIDTH), pl.BlockSpec(s_shape, lambda i: (0, 0, 0))],
        out_shape=[sds((t, DN_WIDTH), F32), sds((t, DN_WIDTH), F32), sds(s_shape, F32)],
        scratch_shapes=[pltpu.VMEM(s_shape, F32),
                        pltpu.VMEM((nch, N_GATES, DN_CHUNK, HEAD_DIM), F32),
                        pltpu.VMEM((nch, N_GATES, 2 * DN_CHUNK, HEAD_DIM), BF16),
                        pltpu.VMEM((nch, N_GATES, DN_CHUNK, HEAD_DIM), BF16),
                        pltpu.VMEM((nch, N_GATES, DN_CHUNK, DN_CHUNK), BF16)],
        compiler_params=_cparams("arbitrary"),
        name="gdn_scan",
    )(q, k, v, gcol, grow, q, k, v, gcol, grow, s0)


S5_SW = 4 * S5_STATE


def _s5_matrices(a_re, a_im, log_dt, b_re, b_im, c_re, c_im, d):
    L, G, P, H = S5_CHUNK, S5_GROUPS, S5_STATE, S5_GROUP_CH
    hi = lax.Precision.HIGHEST
    a_re, a_im = a_re.astype(F32), a_im.astype(F32)
    dt = jnp.exp(log_dt.astype(F32))[..., None]
    n_re = jnp.exp(a_re * dt) * jnp.cos(a_im * dt) - 1.0
    n_im = jnp.exp(a_re * dt) * jnp.sin(a_im * dt)
    den = a_re * a_re + a_im * a_im
    co_re = ((n_re * a_re + n_im * a_im) / den)[..., None]
    co_im = ((n_im * a_re - n_re * a_im) / den)[..., None]
    br, bi = b_re.astype(F32)[None], b_im.astype(F32)[None]
    bb_re, bb_im = co_re * br - co_im * bi, co_re * bi + co_im * br
    tau = jnp.arange(L + 1, dtype=F32)[:, None, None, None]
    mag = jnp.exp(a_re[None] * dt[None] * tau)
    ang = a_im[None] * dt[None] * tau
    pw_re, pw_im = mag * jnp.cos(ang), mag * jnp.sin(ang)
    cr, ci = c_re.astype(F32), c_im.astype(F32)
    e_re = cr[None] * pw_re[:, :, :, None, :] - ci[None] * pw_im[:, :, :, None, :]
    e_im = cr[None] * pw_im[:, :, :, None, :] + ci[None] * pw_re[:, :, :, None, :]
    kk = (jnp.einsum('tdghp,dgpi->tdghi', e_re[:L], bb_re, precision=hi)
          - jnp.einsum('tdghp,dgpi->tdghi', e_im[:L], bb_im, precision=hi))
    s_i = jnp.arange(L)[:, None]
    t_i = jnp.arange(L)[None, :]
    lag = t_i - s_i
    kf = jnp.where((lag >= 0)[..., None, None, None], kk[jnp.clip(lag, 0, L - 1), 0], 0.0)
    kb = jnp.where((lag <= 0)[..., None, None, None], kk[jnp.clip(-lag, 0, L - 1), 1], 0.0)
    dskip = d.astype(F32).reshape(G, H)
    eye_t = (lag == 0).astype(F32)
    skip = eye_t[:, :, None, None, None] * (dskip[:, :, None] * jnp.eye(H, dtype=F32)[None])[None, None]
    toep = (kf + kb + skip).transpose(2, 0, 4, 1, 3).reshape(G, L * H, L * H)

    def loc(pr, pi, dr):
        w_re = pr[..., None] * bb_re[dr][None] - pi[..., None] * bb_im[dr][None]
        w_im = pr[..., None] * bb_im[dr][None] + pi[..., None] * bb_re[dr][None]
        f = lambda w: w.transpose(1, 0, 3, 2).reshape(G, L * H, P)
        return f(w_re), f(w_im)
    wf_re, wf_im = loc(pw_re[:L, 0][::-1], pw_im[:L, 0][::-1], 0)
    wb_re, wb_im = loc(pw_re[:L, 1], pw_im[:L, 1], 1)
    wcat = jnp.concatenate([toep, wf_re, wb_re, wf_im, wb_im], axis=-1).astype(BF16)

    carry = lambda e: e.transpose(1, 3, 0, 2).reshape(G, P, L * H)
    mf_re, mf_im = carry(e_re[1:, 0]), carry(-e_im[1:, 0])
    mb_re, mb_im = carry(e_re[1:, 1][::-1]), carry(-e_im[1:, 1][::-1])
    mcat = jnp.concatenate([mf_re, mb_re, mf_im, mb_im], axis=1).astype(BF16)
    al_re = jnp.concatenate([pw_re[L, 0], pw_re[L, 1]], axis=-1)
    al_im = jnp.concatenate([pw_im[L, 0], pw_im[L, 1]], axis=-1)
    al = jnp.stack([al_re.reshape(-1), al_im.reshape(-1)])
    return wcat, mcat, al


S5_HW = 2 * S5_STATE


def _s5_local_kernel(u_ref, w_ref, y_ref, xre_ref, xim_ref):
    r = jnp.dot(u_ref[...], w_ref[...], preferred_element_type=F32)
    y_ref[...] = r[:, :S5_CW]
    xre_ref[...] = r[:, S5_CW:S5_CW + S5_HW]
    xim_ref[...] = r[:, S5_CW + S5_HW:]


def _s5_local(ug, wcat):
    g, n, _ = ug.shape
    out = lambda w: pl.BlockSpec((n, w), lambda i: (0, i))
    sds = jax.ShapeDtypeStruct
    return pl.pallas_call(
        _s5_local_kernel,
        grid=(g,),
        in_specs=[pl.BlockSpec((None, n, S5_CW), lambda i: (i, 0, 0)),
                  pl.BlockSpec((None, S5_CW, S5_CW + S5_SW), lambda i: (i, 0, 0))],
        out_specs=[out(S5_CW), out(S5_HW), out(S5_HW)],
        out_shape=[sds((n, g * S5_CW), F32), sds((n, g * S5_HW), F32), sds((n, g * S5_HW), F32)],
        compiler_params=_cparams("parallel"),
        name="s5_local",
    )(ug, wcat)


def _s5_scan_kernel(nblk, xfr_ref, xfi_ref, xbr_ref, xbi_ref, al_ref, s0_ref,
                    cfr_ref, cfi_ref, cbr_ref, cbi_ref, sout_ref, s_ref):
    i = pl.program_id(0)

    @pl.when(i == 0)
    def _():
        s_ref[...] = s0_ref[...]

    nb = xfr_ref.shape[0]
    al_re, al_im = al_ref[0:1, :], al_ref[1:2, :]
    is_fwd = (lax.broadcasted_iota(jnp.int32, (1, al_ref.shape[1]), 1) % S5_HW) < S5_STATE

    def body(j, carry):
        re, im = carry
        rf, rb = pl.ds(j, 1), pl.ds(nb - 1 - j, 1)
        cfr_ref[rf, :] = re
        cfi_ref[rf, :] = im
        cbr_ref[rb, :] = re
        cbi_ref[rb, :] = im
        in_re = jnp.where(is_fwd, xfr_ref[rf, :], xbr_ref[rb, :])
        in_im = jnp.where(is_fwd, xfi_ref[rf, :], xbi_ref[rb, :])
        return al_re * re - al_im * im + in_re, al_re * im + al_im * re + in_im

    re, im = lax.fori_loop(0, nb, body, (s_ref[0:1, :], s_ref[1:2, :]))
    s_ref[0:1, :] = re
    s_ref[1:2, :] = im

    @pl.when(i == nblk - 1)
    def _():
        sout_ref[...] = s_ref[...]


def _s5_scan(x_re, x_im, al, s0, nb):
    n, w = x_re.shape
    nblk = n // nb
    fwd = pl.BlockSpec((nb, w), lambda i: (i, 0))
    bwd = pl.BlockSpec((nb, w), lambda i: (nblk - 1 - i, 0))
    sds = jax.ShapeDtypeStruct
    return pl.pallas_call(
        functools.partial(_s5_scan_kernel, nblk),
        grid=(nblk,),
        in_specs=[fwd, fwd, bwd, bwd, _const_spec(al.shape), _const_spec(s0.shape)],
        out_specs=[fwd, fwd, bwd, bwd, pl.BlockSpec(s0.shape, lambda i: (0, 0))],
        out_shape=[sds((n, w), F32)] * 4 + [sds(s0.shape, F32)],
        scratch_shapes=[pltpu.VMEM(s0.shape, F32)],
        compiler_params=_cparams("arbitrary"),
        name="s5_scan",
    )(x_re, x_im, x_re, x_im, al, s0)


def _s5_carry_kernel(y_ref, cfr_ref, cfi_ref, cbr_ref, cbi_ref, m_ref, o_ref):
    is_fwd = lax.broadcasted_iota(jnp.int32, (1, S5_HW), 1) < S5_STATE
    cin = jnp.concatenate([jnp.where(is_fwd, cfr_ref[...], cbr_ref[...]),
                           jnp.where(is_fwd, cfi_ref[...], cbi_ref[...])], axis=1).astype(BF16)
    o_ref[...] = y_ref[...] + jnp.dot(cin, m_ref[...], preferred_element_type=F32)


def _s5_carry(y, cins, mcat):
    n = y.shape[0]
    g = mcat.shape[0]
    blk = lambda w: pl.BlockSpec((n, w), lambda i: (0, i))
    return pl.pallas_call(
        _s5_carry_kernel,
        grid=(g,),
        in_specs=[blk(S5_CW)] + [blk(S5_HW)] * 4 + [pl.BlockSpec((None, S5_SW, S5_CW), lambda i: (i, 0, 0))],
        out_specs=blk(S5_CW),
        out_shape=jax.ShapeDtypeStruct(y.shape, F32),
        compiler_params=_cparams("parallel"),
        name="s5_carry",
    )(y, *cins, mcat)


def _s5_mixer(u, mats, s0):
    wcat, mcat, al = mats
    t = u.shape[0]
    n = t // S5_CHUNK
    ug = u.reshape(n, S5_CHUNK, S5_GROUPS, S5_GROUP_CH).transpose(2, 0, 1, 3).reshape(S5_GROUPS, n, S5_CW)
    y_loc, x_re, x_im = _s5_local(ug, wcat)
    *cins, s_fin = _s5_scan(x_re, x_im, al, s0, min(n, 128))
    y = _s5_carry(y_loc, cins, mcat)
    y = y.reshape(n, S5_GROUPS, S5_CHUNK, S5_GROUP_CH).transpose(0, 2, 1, 3).reshape(t, S5_WIDTH)
    return y, s_fin


ROUTER_E0 = MOE_GROUPS
ROUTE_SEL = 32
OUT_ROWS = 128


def _gelu_tanh(x):
    return 0.5 * x * (1.0 + jnp.tanh(math.sqrt(2.0 / math.pi) * (x + 0.044715 * (x * x * x))))


def _route(logits):
    ninf = float("-inf")
    lane = lax.broadcasted_iota(jnp.int32, (1, LANES), 1)
    lanef = lane.astype(F32)
    first = lambda hit: jnp.min(jnp.where(hit, lanef, float(LANES)), axis=-1, keepdims=True)
    gl = jnp.where(lane < MOE_GROUPS, logits, ninf)
    gmax = jnp.max(gl, axis=-1, keepdims=True)
    pg_top = 1.0 / jnp.sum(jnp.exp(gl - gmax), axis=-1, keepdims=True)
    base = ROUTER_E0 + MOE_PER_GROUP * first(gl == gmax)
    el = jnp.where((lanef >= base) & (lanef < base + MOE_PER_GROUP), logits, ninf)
    emax = jnp.max(el, axis=-1, keepdims=True)
    esum = jnp.sum(jnp.exp(el - emax), axis=-1, keepdims=True)
    i1 = first(el == emax)
    el2 = jnp.where(lanef == i1, ninf, el)
    emax2 = jnp.max(el2, axis=-1, keepdims=True)
    i2 = first(el2 == emax2)
    p1 = 1.0 / esum
    p2 = jnp.exp(emax2 - emax) / esum
    w1 = pg_top * (p1 / (p1 + p2))
    w2 = pg_top * (p2 / (p1 + p2))
    comb = jnp.where(lanef == i1, w1, 0.0) + jnp.where(lanef == i2, w2, 0.0)
    sel = (jnp.where(lane == ROUTE_SEL, i1 - ROUTER_E0, 0.0) + jnp.where(lane == ROUTE_SEL + 1, i2 - ROUTER_E0, 0.0)
           + jnp.where(lane == ROUTE_SEL + 2, w1, 0.0) + jnp.where(lane == ROUTE_SEL + 3, w2, 0.0))
    return comb + sel


def _bucket_rank(route, carry):
    tm = route.shape[0]
    lanef = lax.broadcasted_iota(jnp.int32, (1, LANES), 1).astype(F32)
    e1, e2 = route[:, ROUTE_SEL:ROUTE_SEL + 1], route[:, ROUTE_SEL + 1:ROUTE_SEL + 2]
    lo, hi = jnp.minimum(e1, e2), jnp.maximum(e1, e2)
    grp = jnp.floor(lo * (1.0 / MOE_PER_GROUP))
    a, b = lo - MOE_PER_GROUP * grp, hi - MOE_PER_GROUP * grp
    bucket = grp * MOE_PAIRS + (a * (2 * MOE_PER_GROUP - 1 - a) * 0.5 + (b - a - 1.0))
    onehot = jnp.where(lanef == bucket, 1.0, 0.0)
    r = lax.broadcasted_iota(jnp.int32, (tm, tm), 0)
    c = lax.broadcasted_iota(jnp.int32, (tm, tm), 1)
    before = jnp.where(c < r, 1.0, 0.0).astype(BF16)
    seen = jnp.dot(before, onehot.astype(BF16), preferred_element_type=F32) + carry
    rank = jnp.sum(onehot * seen, axis=-1, keepdims=True)
    lane = lax.broadcasted_iota(jnp.int32, (1, LANES), 1)
    route = route + jnp.where(lane == ROUTE_SEL + 4, bucket, 0.0) + jnp.where(lane == ROUTE_SEL + 5, rank, 0.0)
    return route, carry + jnp.sum(onehot, axis=0, keepdims=True)


def _out_kernel(routed, att_ref, of_ref, ob_ref, gate_ref, y_ref, x_ref, g1_ref, sh2_ref, sc2_ref, n2_ref, dnw_ref,
                bglu_ref, wglu_ref, woa_ref, wob_ref, woc_ref, wr_ref, br_ref, xo_ref, comb_ref, aux_ref, cnt_ref=None):
    if routed:
        @pl.when(pl.program_id(0) == 0)
        def _():
            cnt_ref[...] = jnp.zeros(cnt_ref.shape, F32)
        cnt = cnt_ref[0:1, :]
    tm = x_ref.shape[0]
    for c0 in range(0, tm, OUT_ROWS):
        rows = slice(c0, c0 + OUT_ROWS)
        o = of_ref[rows, :] + ob_ref[rows, :]
        gate = _silu(gate_ref[rows, :])
        dn = jnp.concatenate(
            [_rms_rows(o[:, h * HEAD_DIM:(h + 1) * HEAD_DIM]) * dnw_ref[...] for h in range(DN_HEADS)], axis=1) * gate
        z = _gelu_tanh(y_ref[rows, :])
        s5 = z * jax.nn.sigmoid(jnp.dot(z.astype(BF16), wglu_ref[...], preferred_element_type=F32) + bglu_ref[...])
        mix = (lax.dot_general(att_ref[:, rows], woa_ref[...], _TN, preferred_element_type=F32)
               + jnp.dot(dn.astype(BF16), wob_ref[...], preferred_element_type=F32)
               + jnp.dot(s5.astype(BF16), woc_ref[...], preferred_element_type=F32))
        x = x_ref[rows, :] + g1_ref[...] * mix
        h = _rms_rows(x) * n2_ref[...] * (1.0 + sc2_ref[...]) + sh2_ref[...]
        h_hi = h.astype(BF16)
        h_lo = (h - h_hi.astype(F32)).astype(BF16)
        logits = (jnp.dot(h_hi, wr_ref[0], preferred_element_type=F32) + jnp.dot(h_lo, wr_ref[0], preferred_element_type=F32)
                  + jnp.dot(h_hi, wr_ref[1], preferred_element_type=F32) + br_ref[...])
        route = _route(logits)
        if not routed:
            xo_ref[rows, :] = x
            aux_ref[rows, :] = h_hi
            comb_ref[rows, :] = route
            continue
        route, cnt = _bucket_rank(route, cnt)
        xo_ref[rows, :D_MODEL] = x
        xo_ref[rows, D_MODEL:] = route
        comb_ref[rows, :] = route
    if routed:
        cnt_ref[0:1, :] = cnt
        aux_ref[...] = cnt_ref[...]


def _mix_out(att_t, o_f, o_b, gate, y, x, mod, lw, tm, routed):
    t = x.shape[0]
    row = lambda w: pl.BlockSpec((tm, w), lambda i: (i, 0))
    vec = lambda a: _const_spec(a.shape)
    consts = [mod["g1"], mod["sh2"], mod["sc2"], lw["norm2_w"], lw["dnw"], lw["bglu"], lw["wglu"], lw["wo_a"],
              lw["wo_b"], lw["wo_c"], lw["wr"], lw["br"]]
    sds = jax.ShapeDtypeStruct
    if routed:
        out_specs = [row(D_MODEL + LANES), row(LANES), pl.BlockSpec((SUBLANES, LANES), lambda i: (0, 0))]
        out_shape = [sds((t, D_MODEL + LANES), F32), sds((t, LANES), F32), sds((SUBLANES, LANES), F32)]
        scratch = [pltpu.VMEM((SUBLANES, LANES), F32)]
    else:
        out_specs = [row(D_MODEL), row(LANES), row(D_MODEL)]
        out_shape = [sds((t, D_MODEL), F32), sds((t, LANES), F32), sds((t, D_MODEL), BF16)]
        scratch = []
    return pl.pallas_call(
        functools.partial(_out_kernel, routed),
        grid=(t // tm,),
        in_specs=[pl.BlockSpec((ATTN_WIDTH, tm), lambda i: (0, i)), row(DN_WIDTH), row(DN_WIDTH), row(DN_WIDTH),
                  row(S5_WIDTH), row(D_MODEL)] + [vec(a) for a in consts],
        out_specs=out_specs, out_shape=out_shape, scratch_shapes=scratch,
        compiler_params=_cparams("arbitrary" if routed else "parallel"),
        name="mix_out",
    )(att_t, o_f, o_b, gate, y, x, *consts)


def _moe_kernel(h_ref, comb_ref, x_ref, g2_ref, wg_ref, wu_ref, wd_ref, o_ref, acc_ref):
    e = pl.program_id(1)

    @pl.when(e == 0)
    def _():
        acc_ref[...] = jnp.zeros(acc_ref.shape, F32)

    h = h_ref[...]
    mid = _silu(jnp.dot(h, wg_ref[...], preferred_element_type=F32)) * jnp.dot(h, wu_ref[...], preferred_element_type=F32)
    lane = lax.broadcasted_iota(jnp.int32, (1, LANES), 1)
    cw = jnp.sum(jnp.where(lane == e + ROUTER_E0, comb_ref[...], 0.0), axis=-1, keepdims=True)
    acc_ref[...] += cw * jnp.dot(mid.astype(BF16), wd_ref[...], preferred_element_type=F32)

    @pl.when(e == MOE_EXPERTS - 1)
    def _():
        o_ref[...] = x_ref[...] + g2_ref[...] * acc_ref[...]


def _moe(h, comb, x, g2, lw, tm):
    t = x.shape[0]
    row = lambda w: pl.BlockSpec((tm, w), lambda i, e: (i, 0))
    return pl.pallas_call(
        _moe_kernel,
        grid=(t // tm, MOE_EXPERTS),
        in_specs=[row(D_MODEL), row(LANES), row(D_MODEL), _const_spec(g2.shape),
                  pl.BlockSpec((None, D_MODEL, MOE_HIDDEN), lambda i, e: (e, 0, 0)),
                  pl.BlockSpec((None, D_MODEL, MOE_HIDDEN), lambda i, e: (e, 0, 0)),
                  pl.BlockSpec((None, MOE_HIDDEN, D_MODEL), lambda i, e: (e, 0, 0))],
        out_specs=row(D_MODEL),
        out_shape=jax.ShapeDtypeStruct((t, D_MODEL), F32),
        scratch_shapes=[pltpu.VMEM((tm, D_MODEL), F32)],
        compiler_params=_cparams("parallel", "arbitrary"),
        name="moe",
    )(h, comb, x, g2, lw["w_gate"], lw["w_up"], lw["w_down"])


MOE_PAIRS = MOE_PER_GROUP * (MOE_PER_GROUP - 1) // 2
MOE_BUCKETS = MOE_GROUPS * MOE_PAIRS
MOE_TM = 128
MOE_SCATTER_GROUP = 8


def _moe_plan(route, counts, t):
    n_tiles = t // MOE_TM + MOE_BUCKETS
    bucket = route[:, ROUTE_SEL + 4].astype(jnp.int32)
    rank = route[:, ROUTE_SEL + 5].astype(jnp.int32)
    counts = counts[0, :MOE_BUCKETS].astype(jnp.int32)
    tiles = (counts + MOE_TM - 1) // MOE_TM
    tile_end = jnp.cumsum(tiles)
    tile_off = tile_end - tiles
    slot = tile_off[bucket] * MOE_TM + rank
    tok = jnp.zeros(((n_tiles + 1) * MOE_TM,), jnp.int32).at[slot].set(jnp.arange(t, dtype=jnp.int32))
    tile = jnp.arange(n_tiles)
    tb = jnp.minimum(jnp.sum((tile[:, None] >= tile_end[None, :]).astype(jnp.int32), axis=1), MOE_BUCKETS - 1)
    nvalid = jnp.clip(counts[tb] - (tile - tile_off[tb]) * MOE_TM, 0, MOE_TM)
    nvalid = jnp.where(tile < tile_end[-1], nvalid, 0).astype(jnp.int32)
    pairs = [(i, j) for i in range(MOE_PER_GROUP) for j in range(i + 1, MOE_PER_GROUP)]
    pa = jnp.array([p[0] for p in pairs], jnp.int32)
    pb = jnp.array([p[1] for p in pairs], jnp.int32)
    grp = (tb // MOE_PAIRS) * MOE_PER_GROUP
    e_lo = (grp + pa[tb % MOE_PAIRS]).astype(jnp.int32)
    e_hi = (grp + pb[tb % MOE_PAIRS]).astype(jnp.int32)
    return tok, nvalid, e_lo, e_hi


def _moe_routed_kernel(n_tiles, tok_ref, nv_ref, elo_ref, ehi_ref,
                       x_hbm, g2_ref, sh2_ref, sc2_ref, n2_ref, wgl_ref, wul_ref, wdl_ref,
                       wgh_ref, wuh_ref, wdh_ref, o_hbm, xbuf, obuf, gsem, ssem):
    t = pl.program_id(0)
    slot = lax.rem(t, 2)

    def row_copy(tile, s, r, scatter):
        tok = tok_ref[tile * MOE_TM + r]
        if scatter:
            return pltpu.make_async_copy(obuf.at[s, pl.ds(r, 1)], o_hbm.at[pl.ds(tok, 1)], ssem.at[s])
        return pltpu.make_async_copy(x_hbm.at[pl.ds(tok, 1)], xbuf.at[s, pl.ds(r, 1)], gsem.at[s])

    def start_gather(tile, s):
        for r in range(MOE_TM):
            row_copy(tile, s, r, False).start()

    def wait_gather(s):
        pltpu.make_async_copy(x_hbm.at[pl.ds(0, MOE_TM)], xbuf.at[s], gsem.at[s]).wait()

    def start_scatter(tile, s):
        n = nv_ref[tile]
        for g in range(MOE_TM // MOE_SCATTER_GROUP):
            @pl.when((g + 1) * MOE_SCATTER_GROUP <= n)
            def _():
                for r in range(g * MOE_SCATTER_GROUP, (g + 1) * MOE_SCATTER_GROUP):
                    row_copy(tile, s, r, True).start()

        def body(r, carry):
            row_copy(tile, s, r, True).start()
            return carry
        lax.fori_loop(n - lax.rem(n, MOE_SCATTER_GROUP), n, body, 0)

    def wait_scatter(tile, s):
        n = nv_ref[tile]
        for bit in range(MOE_TM.bit_length()):
            rows = 1 << bit

            @pl.when((n >> bit) & 1 == 1)
            def _():
                pltpu.make_async_copy(obuf.at[s, pl.ds(0, rows)], o_hbm.at[pl.ds(0, rows)], ssem.at[s]).wait()

    @pl.when(t == 0)
    def _():
        start_gather(0, 0)

    wait_gather(slot)

    @pl.when(t >= 2)
    def _():
        wait_scatter(t - 2, slot)

    start_gather(t + 1, 1 - slot)
    x = xbuf[slot, :, :D_MODEL]
    rec = xbuf[slot, :, D_MODEL:]
    h = (_rms_rows(x) * n2_ref[...] * (1.0 + sc2_ref[...]) + sh2_ref[...]).astype(BF16)
    top1_is_lo = rec[:, ROUTE_SEL:ROUTE_SEL + 1] == elo_ref[t].astype(F32)
    w1, w2 = rec[:, ROUTE_SEL + 2:ROUTE_SEL + 3], rec[:, ROUTE_SEL + 3:ROUTE_SEL + 4]
    w = (jnp.where(top1_is_lo, w1, w2), jnp.where(top1_is_lo, w2, w1))
    y = None
    for k, (wg_ref, wu_ref, wd_ref) in enumerate(((wgl_ref, wul_ref, wdl_ref), (wgh_ref, wuh_ref, wdh_ref))):
        mid = (_silu(jnp.dot(h, wg_ref[...], preferred_element_type=F32))
               * jnp.dot(h, wu_ref[...], preferred_element_type=F32))
        yk = w[k] * jnp.dot(mid.astype(BF16), wd_ref[...], preferred_element_type=F32)
        y = yk if y is None else y + yk
    obuf[slot] = x + g2_ref[...] * y
    start_scatter(t, slot)

    @pl.when(t == n_tiles - 1)
    def _():
        wait_gather(1 - slot)
        if n_tiles >= 2:
            wait_scatter(t - 1, 1 - slot)
        wait_scatter(t, slot)


def _moe_routed(x, route, counts, mod, lw):
    t = x.shape[0]
    tok, nvalid, e_lo, e_hi = _moe_plan(route, counts, t)
    n_tiles = nvalid.shape[0]
    vec = lambda a: pl.BlockSpec(a.shape, lambda i, *_: (0,) * a.ndim, pipeline_mode=pl.Buffered(1))
    wspec = lambda shape, which: pl.BlockSpec(
        (None,) + shape, (lambda i, tok, nv, elo, ehi: (elo[i], 0, 0)) if which == 0
        else (lambda i, tok, nv, elo, ehi: (ehi[i], 0, 0)))
    up, down = (D_MODEL, MOE_HIDDEN), (MOE_HIDDEN, D_MODEL)
    consts = [mod["g2"], mod["sh2"], mod["sc2"], lw["norm2_w"]]
    grid_spec = pltpu.PrefetchScalarGridSpec(
        num_scalar_prefetch=4,
        grid=(n_tiles,),
        in_specs=[pl.BlockSpec(memory_space=pl.ANY)] + [vec(a) for a in consts]
                 + [wspec(up, 0), wspec(up, 0), wspec(down, 0), wspec(up, 1), wspec(up, 1), wspec(down, 1)],
        out_specs=pl.BlockSpec(memory_space=pl.ANY),
        scratch_shapes=[pltpu.VMEM((2, MOE_TM, D_MODEL + LANES), F32), pltpu.VMEM((2, MOE_TM, D_MODEL), F32),
                        pltpu.SemaphoreType.DMA((2,)), pltpu.SemaphoreType.DMA((2,))],
    )
    return pl.pallas_call(
        functools.partial(_moe_routed_kernel, n_tiles),
        grid_spec=grid_spec,
        out_shape=jax.ShapeDtypeStruct((t, D_MODEL), F32),
        compiler_params=_cparams("arbitrary"),
        name="moe_routed",
    )(tok, nvalid, e_lo, e_hi, x, *consts, lw["w_gate"], lw["w_up"], lw["w_down"],
      lw["w_gate"], lw["w_up"], lw["w_down"])


def _rope_tables(n_tokens):
    t = jnp.arange(n_tokens)
    row = (t // GRID_W).astype(F32)
    col = (t % GRID_W).astype(F32)
    half = HEAD_DIM // 2
    inv = ROPE_THETA ** (-jnp.arange(0, half, 2, dtype=F32) / half)
    ang_r = row[:, None] * inv[None, :]
    ang_c = col[:, None] * inv[None, :]
    ang = jnp.concatenate([ang_r, ang_r, ang_c, ang_c], axis=-1)
    cos, sin = jnp.cos(ang), jnp.sin(ang)
    return cos, sin, cos.T, sin.T


def _layer_weights(l, p):
    o = [0]
    for s in IN_SPLITS:
        o.append(o[-1] + s)
    w_in = p["w_in"][l]
    seg = lambda i: w_in[:, o[i]:o[i + 1]]
    ba = jnp.concatenate([seg(5), seg(6)], axis=1)
    lanes16 = lambda v: jnp.zeros((LANES,), F32).at[N_GATES:2 * N_GATES].set(v.reshape(N_GATES).astype(F32))
    alog, dtb = lanes16(p["dn_a_log"][l]), lanes16(p["dn_dt_bias"][l])
    w_out = p["w_out"][l]
    wr = jnp.zeros((D_MODEL, LANES), F32)
    wr = wr.at[:, :MOE_GROUPS].set(p["router_g_w"][l]).at[:, ROUTER_E0:ROUTER_E0 + MOE_EXPERTS].set(p["router_e_w"][l])
    br = jnp.zeros((1, LANES), F32)
    br = br.at[0, :MOE_GROUPS].set(p["router_g_b"][l]).at[0, ROUTER_E0:ROUTER_E0 + MOE_EXPERTS].set(p["router_e_b"][l])
    return dict(
        norm1_w=p["norm1_w"][l][None], norm2_w=p["norm2_w"][l][None],
        wqt=seg(0).T.astype(BF16), wk=seg(1).astype(BF16), wvt=seg(2).T.astype(BF16), wdn=seg(3).astype(BF16),
        wg=seg(4).astype(BF16), wba=jnp.pad(ba, ((0, 0), (0, LANES - 2 * N_GATES))).astype(BF16),
        wbat=ba.T.astype(BF16), ws5=seg(7).astype(BF16),
        qn_col=p["q_norm_w"][l][:, None], kn_row=p["k_norm_w"][l][None],
        conv_w=jnp.pad(p["dn_conv_w"][l], ((0, SUBLANES - DN_CONV), (0, 0))),
        alog_row=alog[None], dtb_row=dtb[None], alog_col=alog[:2 * N_GATES, None], dtb_col=dtb[:2 * N_GATES, None],
        dnw=p["dn_out_norm_w"][l][None], wglu=p["s5_w_glu"][l].astype(BF16), bglu=p["s5_b_glu"][l][None],
        wo_a=w_out[:ATTN_WIDTH].astype(BF16), wo_b=w_out[ATTN_WIDTH:ATTN_WIDTH + DN_WIDTH].astype(BF16),
        wo_c=w_out[ATTN_WIDTH + DN_WIDTH:].astype(BF16), br=br,
        wr=jnp.stack([wr.astype(BF16), (wr - wr.astype(BF16).astype(F32)).astype(BF16)]),
        w_gate=p["w_gate"][l].astype(BF16), w_up=p["w_up"][l].astype(BF16), w_down=p["w_down"][l].astype(BF16),
        s5=_s5_matrices(p["s5_a_re"][l], p["s5_a_im"][l], p["s5_log_dt"][l], p["s5_b_re"][l], p["s5_b_im"][l],
                        p["s5_c_re"][l], p["s5_c_im"][l], p["s5_d"][l]),
    )


TM_CTX = 256
TM_IN = 256
TM_SEQ = 512


def _mixers(x, mod, lw, tabs, rope, s_dn, s_s5, tm_in, tm):
    qt, k, vt, dn, gate, ba, bat, u = _in_proj(x, mod, lw, tabs, rope, tm_in)
    q_dn, k_dn, v_dn, gcol, grow = _gdn_prep(dn, ba, bat, lw, tm)
    o_f, o_b, s_dn = _gdn_scan(q_dn, k_dn, v_dn, gcol, grow, s_dn, tm)
    y, s_s5 = _s5_mixer(u, lw["s5"], s_s5)
    return dict(qt=qt, k=k, vt=vt, gate=gate, o_f=o_f, o_b=o_b, y=y), s_dn, s_s5


def kernel(x, c, ctx, c_ctx, w_ada, b_ada, norm1_w, norm2_w, w_in, q_norm_w, k_norm_w, dn_conv_w, dn_a_log, dn_dt_bias,
           dn_out_norm_w, s5_a_re, s5_a_im, s5_log_dt, s5_b_re, s5_b_im, s5_c_re, s5_c_im, s5_d, s5_w_glu, s5_b_glu,
           w_out, router_g_w, router_g_b, router_e_w, router_e_b, w_gate, w_up, w_down):
    p = dict(norm1_w=norm1_w, norm2_w=norm2_w, w_in=w_in, q_norm_w=q_norm_w, k_norm_w=k_norm_w, dn_conv_w=dn_conv_w,
             dn_a_log=dn_a_log, dn_dt_bias=dn_dt_bias, dn_out_norm_w=dn_out_norm_w, s5_a_re=s5_a_re, s5_a_im=s5_a_im,
             s5_log_dt=s5_log_dt, s5_b_re=s5_b_re, s5_b_im=s5_b_im, s5_c_re=s5_c_re, s5_c_im=s5_c_im, s5_d=s5_d,
             s5_w_glu=s5_w_glu, s5_b_glu=s5_b_glu, w_out=w_out, router_g_w=router_g_w, router_g_b=router_g_b,
             router_e_w=router_e_w, router_e_b=router_e_b, w_gate=w_gate, w_up=w_up, w_down=w_down)
    assert x.shape[0] == 1 and ctx.shape[0] == 1
    xl, xc = x[0], ctx[0]
    n_ctx = xc.shape[0]
    cvec = jnp.zeros((SUBLANES, D_MODEL), F32).at[0].set(c[0]).at[1].set(c_ctx)
    mods = _adaln_mod(cvec, w_ada, b_ada)
    tabs = _rope_tables(xl.shape[0])
    tabs_c = (tabs[0][:n_ctx], tabs[1][:n_ctx], tabs[2][:, :n_ctx], tabs[3][:, :n_ctx])
    names = ("sh1", "sc1", "g1", "sh2", "sc2", "g2")
    for l in range(DEPTH):
        lw = _layer_weights(l, p)
        mod_l = {n: mods[l, 0:1, i * D_MODEL:(i + 1) * D_MODEL] for i, n in enumerate(names)}
        mod_c = {n: mods[l, 1:2, i * D_MODEL:(i + 1) * D_MODEL] for i, n in enumerate(names)}
        s_dn0 = jnp.zeros((N_GATES, HEAD_DIM, HEAD_DIM), F32)
        s_s50 = jnp.zeros((2, S5_GROUPS * S5_HW), F32)
        mc, s_dn, s_s5 = _mixers(xc, mod_c, lw, tabs_c, False, s_dn0, s_s50, TM_CTX, TM_CTX)
        ml, _, _ = _mixers(xl, mod_l, lw, tabs, True, s_dn, s_s5, TM_IN, TM_SEQ)
        att_l = _attention(ml["qt"], (mc["k"], mc["vt"]), (ml["k"], ml["vt"]))
        xl, route, counts = _mix_out(att_l, ml["o_f"], ml["o_b"], ml["gate"], ml["y"], xl, mod_l, lw, TM_SEQ, True)
        xl = _moe_routed(xl, route, counts, mod_l, lw)
        if l < DEPTH - 1:
            att_c = _attention(mc["qt"], (mc["k"], mc["vt"]))
            xc, comb, h2 = _mix_out(att_c, mc["o_f"], mc["o_b"], mc["gate"], mc["y"], xc, mod_c, lw, TM_CTX, False)
            xc = _moe(h2, comb, xc, mod_c["g2"], lw, TM_CTX)
    return xl[None]
```

```python
import functools
import math

import jax
import jax.numpy as jnp
from jax import lax
from jax.experimental import pallas as pl
from jax.experimental.pallas import tpu as pltpu

F32 = jnp.float32
BF16 = jnp.bfloat16

D_MODEL = 2048
DEPTH = 2
GRID_W = 64
N_DIR = 2
HEAD_DIM = 128
ATTN_WIDTH = D_MODEL // 2
ATTN_Q_HEADS = ATTN_WIDTH // HEAD_DIM
ATTN_KV_HEADS = ATTN_Q_HEADS // 4
KV_WIDTH = ATTN_KV_HEADS * HEAD_DIM
ROPE_THETA = 10000.0
DN_WIDTH = D_MODEL // 4
DN_HEADS = DN_WIDTH // HEAD_DIM
DN_CONV = 5
DN_CHUNK = 64
S5_WIDTH = D_MODEL // 4
S5_GROUP_CH = 16
S5_GROUPS = S5_WIDTH // S5_GROUP_CH
S5_STATE = 64
MIX_WIDTH = ATTN_WIDTH + DN_WIDTH + S5_WIDTH
IN_SPLITS = (ATTN_WIDTH, KV_WIDTH, KV_WIDTH, 3 * DN_WIDTH, DN_WIDTH, N_DIR * DN_HEADS, N_DIR * DN_HEADS, S5_WIDTH)
MOE_GROUPS = 4
MOE_PER_GROUP = 4
MOE_EXPERTS = MOE_GROUPS * MOE_PER_GROUP
MOE_HIDDEN = D_MODEL // 4
EPS = 1e-6

LANES = 128
SUBLANES = 8
VMEM_LIMIT_BYTES = 56 * 1024 * 1024

N_GATES = N_DIR * DN_HEADS
S5_CHUNK = 16
S5_CW = S5_CHUNK * S5_GROUP_CH


def _cparams(*sem):
    return pltpu.CompilerParams(dimension_semantics=sem, vmem_limit_bytes=VMEM_LIMIT_BYTES)


def _const_spec(shape):
    nd = len(shape)
    return pl.BlockSpec(shape, lambda *_: (0,) * nd, pipeline_mode=pl.Buffered(1))


def _silu(x):
    return x * jax.nn.sigmoid(x)


MOD_TN = 1024


def _mod_kernel(cv_ref, w_ref, b_ref, o_ref):
    s = _silu(cv_ref[...]).astype(BF16)
    o_ref[...] = jnp.dot(s, w_ref[...].astype(BF16), preferred_element_type=F32) + b_ref[...]


def _adaln_mod(cvec, w_ada, b_ada):
    n6 = w_ada.shape[-1]
    return pl.pallas_call(
        _mod_kernel,
        grid=(DEPTH, n6 // MOD_TN),
        in_specs=[
            pl.BlockSpec((SUBLANES, D_MODEL), lambda l, j: (0, 0)),
            pl.BlockSpec((None, D_MODEL, MOD_TN), lambda l, j: (l, 0, j)),
            pl.BlockSpec((None, 1, MOD_TN), lambda l, j: (l, 0, j)),
        ],
        out_specs=pl.BlockSpec((None, SUBLANES, MOD_TN), lambda l, j: (l, 0, j)),
        out_shape=jax.ShapeDtypeStruct((DEPTH, SUBLANES, n6), F32),
        compiler_params=_cparams("parallel", "parallel"),
        name="adaln_mod",
    )(cvec, w_ada, b_ada.reshape(DEPTH, 1, n6))


def _rms_rows(x):
    return x * lax.rsqrt(jnp.mean(x * x, axis=-1, keepdims=True) + EPS)


def _in_kernel(rope, x_ref, nw_ref, sh_ref, sc_ref, wqt_ref, wk_ref, wvt_ref, wdn_ref, wg_ref, wba_ref, wbat_ref,
               ws5_ref, qn_ref, kn_ref, cos_ref, sin_ref, cost_ref, sint_ref,
               qt_ref, k_ref, vt_ref, dn_ref, g_ref, ba_ref, bat_ref, u_ref):
    x = x_ref[...]
    h = (_rms_rows(x) * nw_ref[...] * (1.0 + sc_ref[...]) + sh_ref[...]).astype(BF16)
    nt = (((1,), (1,)), ((), ()))

    qt = lax.dot_general(wqt_ref[...], h, nt, preferred_element_type=F32)
    q_scale = HEAD_DIM ** -0.5 * math.log2(math.e)
    for hd in range(ATTN_Q_HEADS):
        qh = qt[hd * HEAD_DIM:(hd + 1) * HEAD_DIM, :]
        qh = qh * lax.rsqrt(jnp.mean(qh * qh, axis=0, keepdims=True) + EPS) * qn_ref[...]
        if rope:
            a, b, c, d = (qh[i * 32:(i + 1) * 32, :] for i in range(4))
            rot = jnp.concatenate([-b, a, -d, c], axis=0)
            qh = qh * cost_ref[...] + rot * sint_ref[...]
        qt_ref[hd * HEAD_DIM:(hd + 1) * HEAD_DIM, :] = (qh * q_scale).astype(BF16)

    k = jnp.dot(h, wk_ref[...], preferred_element_type=F32)
    lane = lax.broadcasted_iota(jnp.int32, (1, HEAD_DIM), 1)
    first_half = (lane % 64) < 32
    for hd in range(ATTN_KV_HEADS):
        kh = _rms_rows(k[:, hd * HEAD_DIM:(hd + 1) * HEAD_DIM]) * kn_ref[...]
        if rope:
            rot = jnp.where(first_half, -pltpu.roll(kh, 96, 1), pltpu.roll(kh, 32, 1))
            kh = kh * cos_ref[...] + rot * sin_ref[...]
        k_ref[:, hd * HEAD_DIM:(hd + 1) * HEAD_DIM] = kh.astype(BF16)

    vt_ref[...] = lax.dot_general(wvt_ref[...], h, nt, preferred_element_type=F32).astype(BF16)
    dn_ref[...] = jnp.dot(h, wdn_ref[...], preferred_element_type=F32)
    g_ref[...] = jnp.dot(h, wg_ref[...], preferred_element_type=F32)
    ba_ref[...] = jnp.dot(h, wba_ref[...], preferred_element_type=F32)
    bat_ref[...] = lax.dot_general(wbat_ref[...], h, nt, preferred_element_type=F32)
    u_ref[...] = jnp.dot(h, ws5_ref[...], preferred_element_type=F32).astype(BF16)


def _in_proj(x, mod, lw, rope_tabs, rope, tm):
    t = x.shape[0]
    cos, sin, cost, sint = rope_tabs
    row = lambda w: pl.BlockSpec((tm, w), lambda i: (i, 0))
    col = lambda w: pl.BlockSpec((w, tm), lambda i: (0, i))
    vec = lambda a: _const_spec(a.shape)
    ws = [lw["wqt"], lw["wk"], lw["wvt"], lw["wdn"], lw["wg"], lw["wba"], lw["wbat"], lw["ws5"]]
    in_specs = ([row(D_MODEL), vec(lw["norm1_w"]), vec(mod["sh1"]), vec(mod["sc1"])] + [vec(w) for w in ws]
                + [vec(lw["qn_col"]), vec(lw["kn_row"]), row(HEAD_DIM), row(HEAD_DIM), col(HEAD_DIM), col(HEAD_DIM)])
    out_specs = [col(ATTN_WIDTH), row(KV_WIDTH), pl.BlockSpec((None, KV_WIDTH, tm), lambda i: (i, 0, 0)), row(3 * DN_WIDTH), row(DN_WIDTH), row(LANES),
                 col(2 * N_GATES), row(S5_WIDTH)]
    sds = jax.ShapeDtypeStruct
    out_shape = [sds((ATTN_WIDTH, t), BF16), sds((t, KV_WIDTH), BF16), sds((t // tm, KV_WIDTH, tm), BF16),
                 sds((t, 3 * DN_WIDTH), F32), sds((t, DN_WIDTH), F32), sds((t, LANES), F32),
                 sds((2 * N_GATES, t), F32), sds((t, S5_WIDTH), BF16)]
    return pl.pallas_call(
        functools.partial(_in_kernel, rope),
        grid=(t // tm,),
        in_specs=in_specs, out_specs=out_specs, out_shape=out_shape,
        compiler_params=_cparams("parallel"),
        name="in_proj",
    )(x, lw["norm1_w"], mod["sh1"], mod["sc1"], *ws, lw["qn_col"], lw["kn_row"], cos, sin, cost, sint)


ATT_TQ = 256
ATT_TK = 256
GQA = ATTN_Q_HEADS // ATTN_KV_HEADS


ATT_UNROLL = 4
ATT_ONES = 16


def _attn_kernel(n_lat, qt_ref, kc_ref, vtc_ref, *refs):
    if n_lat:
        kl_ref, vtl_ref, ot_ref, sa_ref, sb_ref, m_ref, acc_ref = refs
    else:
        ot_ref, sa_ref, sb_ref, m_ref, acc_ref = refs
    tq = qt_ref.shape[1]
    q = jnp.concatenate([qt_ref[h * HEAD_DIM:(h + 1) * HEAD_DIM, :] for h in range(GQA)], axis=1)
    ones = jnp.ones((ATT_ONES, ATT_TK), BF16)

    def scores(kb, dst_ref):
        dst_ref[...] = jnp.dot(kb, q, preferred_element_type=F32)

    def lat_keys(j):
        return kl_ref[pl.ds(pl.multiple_of(j * ATT_TK, ATT_TK), ATT_TK), :]

    def consume(vt_chunk, src_ref):
        vt = jnp.concatenate([vt_chunk, ones], axis=0)
        for h in range(GQA):
            cols = slice(h * tq, (h + 1) * tq)
            s = src_ref[:, cols]
            m_old = m_ref[:, cols]
            m_new = jnp.maximum(m_old, jnp.max(s, axis=0, keepdims=True))
            p = jnp.exp2((s - m_new).astype(BF16))
            acc_ref[:, cols] = (jnp.exp2(m_old - m_new) * acc_ref[:, cols]
                                + jnp.dot(vt, p, preferred_element_type=F32))
            m_ref[:, cols] = m_new

    m_ref[...] = jnp.full(m_ref.shape, -jnp.inf, F32)
    acc_ref[...] = jnp.zeros(acc_ref.shape, F32)
    scores(kc_ref[...], sa_ref)
    if n_lat:
        scores(lat_keys(0), sb_ref)
        consume(vtc_ref[0], sa_ref)

        def pair(i, carry):
            j = 2 * i
            scores(lat_keys(j + 1), sa_ref)
            consume(vtl_ref[j], sb_ref)
            scores(lat_keys(jnp.minimum(j + 2, n_lat - 1)), sb_ref)
            consume(vtl_ref[j + 1], sa_ref)
            return carry

        lax.fori_loop(0, n_lat // 2, pair, 0, unroll=min(ATT_UNROLL, n_lat // 2))
    else:
        consume(vtc_ref[0], sa_ref)
    acc = acc_ref[...]
    o = acc[:HEAD_DIM] * (1.0 / acc[HEAD_DIM:HEAD_DIM + 1])
    for h in range(GQA):
        ot_ref[h * HEAD_DIM:(h + 1) * HEAD_DIM, :] = o[:, h * tq:(h + 1) * tq].astype(BF16)


def _attention(qt, kv_ctx, kv_lat=None):
    t = qt.shape[1]
    tq = min(ATT_TQ, t)
    kv_specs = lambda k, vt: [pl.BlockSpec((k.shape[0], HEAD_DIM), lambda g, i: (0, g)),
                              pl.BlockSpec((vt.shape[0], HEAD_DIM, ATT_TK), lambda g, i: (0, g, 0))]
    assert kv_ctx[0].shape[0] == ATT_TK
    n_lat = 0
    kv = list(kv_ctx)
    specs = kv_specs(*kv_ctx)
    if kv_lat is not None:
        n_lat = kv_lat[0].shape[0] // ATT_TK
        assert n_lat % 2 == 0
        kv += list(kv_lat)
        specs += kv_specs(*kv_lat)
    return pl.pallas_call(
        functools.partial(_attn_kernel, n_lat),
        grid=(ATTN_KV_HEADS, t // tq),
        in_specs=[pl.BlockSpec((GQA * HEAD_DIM, tq), lambda g, i: (g, i))] + specs,
        out_specs=pl.BlockSpec((GQA * HEAD_DIM, tq), lambda g, i: (g, i)),
        out_shape=jax.ShapeDtypeStruct((ATTN_WIDTH, t), BF16),
        scratch_shapes=[pltpu.VMEM((ATT_TK, GQA * tq), F32), pltpu.VMEM((ATT_TK, GQA * tq), F32),
                        pltpu.VMEM((1, GQA * tq), F32), pltpu.VMEM((HEAD_DIM + ATT_ONES, GQA * tq), F32)],
        compiler_params=_cparams("parallel", "parallel"),
        name="attention",
    )(qt, *kv)


def _softplus(x):
    return jnp.maximum(x, 0.0) + jnp.log(1.0 + jnp.exp(-jnp.abs(x)))


def _dot_f32(a, b):
    return jnp.dot(a, b, precision=lax.Precision.HIGHEST, preferred_element_type=F32)


def _gdn_prep_kernel(nblk, cur_ref, prev_ref, next_ref, cw_ref, ba_ref, bat_ref, alog_r, dtb_r, alog_c, dtb_c,
                     q_ref, k_ref, v_ref, gcol_ref, grow_ref, xx_ref):
    i = pl.program_id(0)
    tm = cur_ref.shape[0]
    pad = (DN_CONV - 1) // 2
    xx_ref[0:SUBLANES, :] = jnp.where(i > 0, prev_ref[...], 0.0)
    xx_ref[SUBLANES:SUBLANES + tm, :] = cur_ref[...]
    xx_ref[SUBLANES + tm:2 * SUBLANES + tm, :] = jnp.where(i < nblk - 1, next_ref[...], 0.0)
    acc = None
    for j in range(DN_CONV):
        term = xx_ref[SUBLANES - pad + j:SUBLANES - pad + j + tm, :] * cw_ref[j:j + 1, :]
        acc = term if acc is None else acc + term
    y = _silu(acc)
    for hd in range(DN_HEADS):
        sl = slice(hd * HEAD_DIM, (hd + 1) * HEAD_DIM)
        qh = y[:, sl]
        q_ref[:, sl] = qh * (lax.rsqrt(jnp.sum(qh * qh, axis=-1, keepdims=True) + EPS) * HEAD_DIM ** -0.5)
        kh = y[:, DN_WIDTH + hd * HEAD_DIM:DN_WIDTH + (hd + 1) * HEAD_DIM]
        k_ref[:, sl] = kh * lax.rsqrt(jnp.sum(kh * kh, axis=-1, keepdims=True) + EPS)
    v_ref[...] = y[:, 2 * DN_WIDTH:]

    r = lax.broadcasted_iota(jnp.int32, (tm, tm), 0)
    c = lax.broadcasted_iota(jnp.int32, (tm, tm), 1)
    same = (r // DN_CHUNK) == (c // DN_CHUNK)
    tri_le = jnp.where(same & (c <= r), 1.0, 0.0)
    tri_ge = jnp.where(same & (c >= r), 1.0, 0.0)

    ba = ba_ref[...]
    lane = lax.broadcasted_iota(jnp.int32, (1, LANES), 1)
    g = -jnp.exp(alog_r[...]) * _softplus(ba + dtb_r[...])
    gc = jnp.where(lane < N_GATES + DN_HEADS, _dot_f32(tri_le, g), _dot_f32(tri_ge, g))
    gcol_ref[...] = jnp.where(lane < N_GATES, jax.nn.sigmoid(ba), gc)

    gt = -jnp.exp(alog_c[...]) * _softplus(bat_ref[...] + dtb_c[...])
    row = lax.broadcasted_iota(jnp.int32, (2 * N_GATES, 1), 0)
    gct = jnp.where(row < N_GATES + DN_HEADS, _dot_f32(gt, tri_ge), _dot_f32(gt, tri_le))
    for ch in range(tm // DN_CHUNK):
        grow_ref[ch] = gct[N_GATES:, ch * DN_CHUNK:(ch + 1) * DN_CHUNK]


def _gdn_prep(dn, ba, bat, lw, tm):
    t = dn.shape[0]
    nblk = t // tm
    nsub = tm // SUBLANES
    w3 = 3 * DN_WIDTH
    row = lambda w: pl.BlockSpec((tm, w), lambda i: (i, 0))
    vec = lambda a: _const_spec(a.shape)
    sds = jax.ShapeDtypeStruct
    return pl.pallas_call(
        functools.partial(_gdn_prep_kernel, nblk),
        grid=(nblk,),
        in_specs=[row(w3),
                  pl.BlockSpec((SUBLANES, w3), lambda i: (jnp.maximum(i * nsub - 1, 0), 0)),
                  pl.BlockSpec((SUBLANES, w3), lambda i: (jnp.minimum((i + 1) * nsub, t // SUBLANES - 1), 0)),
                  vec(lw["conv_w"]), row(LANES), pl.BlockSpec((2 * N_GATES, tm), lambda i: (0, i)),
                  vec(lw["alog_row"]), vec(lw["dtb_row"]), vec(lw["alog_col"]), vec(lw["dtb_col"])],
        out_specs=[row(DN_WIDTH), row(DN_WIDTH), row(DN_WIDTH), row(LANES),
                   pl.BlockSpec((tm // DN_CHUNK, N_GATES, DN_CHUNK), lambda i: (i, 0, 0))],
        out_shape=[sds((t, DN_WIDTH), F32), sds((t, DN_WIDTH), F32), sds((t, DN_WIDTH), F32), sds((t, LANES), F32),
                   sds((t // DN_CHUNK, N_GATES, DN_CHUNK), F32)],
        scratch_shapes=[pltpu.VMEM((tm + 2 * SUBLANES, w3), F32)],
        compiler_params=_cparams("parallel"),
        name="gdn_prep",
    )(dn, dn, dn, lw["conv_w"], ba, bat, lw["alog_row"], lw["dtb_row"], lw["alog_col"], lw["dtb_col"])


def _dot_bf(a, b, dims=(((1,), (0,)), ((), ()))):
    return lax.dot_general(a.astype(BF16), b.astype(BF16), dims, preferred_element_type=F32)


_NT = (((1,), (1,)), ((), ()))
_TN = (((0,), (0,)), ((), ()))
TRI_BASE = 8


def _gdn_local(chains):
    c = DN_CHUNK
    ri = lax.broadcasted_iota(jnp.int32, (c, c), 0)
    ci = lax.broadcasted_iota(jnp.int32, (c, c), 1)
    blk = lambda b: (ri // b) == (ci // b)
    eye = jnp.where(ri == ci, 1.0, 0.0)
    n = range(len(chains))
    qs, ks, vs, betas, gcs, grs, fwds = zip(*chains)
    incl = [(ri >= ci) if f else (ri <= ci) for f in fwds]
    strict = [(ri > ci) if f else (ri < ci) for f in fwds]
    decay = [jnp.where(incl[i], jnp.exp(jnp.where(incl[i], gcs[i] - grs[i], 0.0)), 0.0) for i in n]
    glast = [grs[i][:, c - 1:c] if fwds[i] else grs[i][:, 0:1] for i in n]
    eg = [jnp.exp(g) for g in gcs]
    kb = [ks[i] * betas[i] for i in n]
    kbf = [k.astype(BF16) for k in ks]
    a = [jnp.where(strict[i], _dot_bf(kb[i], kbf[i], _NT) * decay[i], 0.0) for i in n]
    qk = [(_dot_bf(qs[i], kbf[i], _NT) * decay[i]).astype(BF16) for i in n]
    p = [jnp.where(blk(TRI_BASE), a[i], 0.0) for i in n]
    x = [eye - p[i] for i in n]
    for _ in range(int(math.log2(TRI_BASE)) - 1):
        p = [_dot_bf(p[i], p[i]) for i in n]
        x = [x[i] + _dot_bf(x[i], p[i]) for i in n]
    b = 2 * TRI_BASE
    while b <= c:
        off = blk(b) & jnp.logical_not(blk(b // 2))
        lx = [_dot_bf(jnp.where(off, a[i], 0.0), x[i]) for i in n]
        x = [x[i] - _dot_bf(x[i], lx[i]) for i in n]
        b *= 2
    sol = [_dot_bf(x[i], jnp.concatenate([vs[i] * betas[i], kb[i] * eg[i]], axis=1)) for i in n]
    return [(sol[i][:, :HEAD_DIM],
             jnp.concatenate([sol[i][:, HEAD_DIM:], qs[i] * eg[i]], axis=0).astype(BF16),
             (ks[i] * jnp.exp(glast[i] - gcs[i])).astype(BF16),
             qk[i]) for i in n]


GDN_LOCAL_CHUNKS = 2


def _gdn_kernel(nblk, qf, kf, vf, gcf, grf, qb, kb, vb, gcb, grb, s0_ref, of_ref, ob_ref, sout_ref,
                s_ref, u_ref, wq_ref, kd_ref, qk_ref):
    i = pl.program_id(0)
    c = DN_CHUNK

    @pl.when(i == 0)
    def _():
        s_ref[...] = s0_ref[...]

    nb = qf.shape[0] // c
    dirs = ((qf, kf, vf, gcf, grf), (qb, kb, vb, gcb, grb))

    def local_body(jp, carry):
        work = []
        for cc in range(GDN_LOCAL_CHUNKS):
            ch = jp * GDN_LOCAL_CHUNKS + cc
            rows = pl.ds(pl.multiple_of(ch * c, c), c)
            for d, (q_ref, k_ref, v_ref, gc_ref, gr_ref) in enumerate(dirs):
                gcol = gc_ref[rows, :]
                grow = gr_ref[ch]
                for h in range(DN_HEADS):
                    gi = d * DN_HEADS + h
                    sl = slice(h * HEAD_DIM, (h + 1) * HEAD_DIM)
                    work.append((ch, gi, (q_ref[rows, sl], k_ref[rows, sl], v_ref[rows, sl], gcol[:, gi:gi + 1],
                                          gcol[:, N_GATES + gi:N_GATES + gi + 1], grow[gi:gi + 1, :], d == 0)))
        done = _gdn_local([args for _, _, args in work])
        for (ch, gi, _), (u, wq, kd, qk) in zip(work, done):
            u_ref[ch, gi] = u
            wq_ref[ch, gi] = wq
            kd_ref[ch, gi] = kd
            qk_ref[ch, gi] = qk
        return carry

    lax.fori_loop(0, nb // GDN_LOCAL_CHUNKS, local_body, 0)

    def seq_body(j, carry):
        work = []
        for d, (gr_ref, o_ref) in enumerate(((grf, of_ref), (grb, ob_ref))):
            jj = j if d == 0 else nb - 1 - j
            grow = gr_ref[jj]
            for h in range(DN_HEADS):
                gi = d * DN_HEADS + h
                glast = grow[gi:gi + 1, c - 1:c] if d == 0 else grow[gi:gi + 1, 0:1]
                work.append((o_ref, jj, h, gi, s_ref[gi], u_ref[jj, gi], wq_ref[jj, gi], kd_ref[jj, gi], qk_ref[jj, gi],
                             jnp.exp(glast)))
        ws = [jnp.dot(w[6], w[4].astype(BF16), preferred_element_type=F32) for w in work]
        v_new = [(w[5] - ws_[:c]).astype(BF16) for w, ws_ in zip(work, ws)]
        o = [ws_[c:] + jnp.dot(w[8], vn, preferred_element_type=F32) for w, ws_, vn in zip(work, ws, v_new)]
        s_new = [w[4] * w[9] + lax.dot_general(w[7], vn, _TN, preferred_element_type=F32) for w, vn in zip(work, v_new)]
        for (o_ref, jj, h, gi, *_), o_, s_ in zip(work, o, s_new):
            o_ref[pl.ds(pl.multiple_of(jj * c, c), c), h * HEAD_DIM:(h + 1) * HEAD_DIM] = o_
            s_ref[gi] = s_
        return carry

    lax.fori_loop(0, nb, seq_body, 0)

    @pl.when(i == nblk - 1)
    def _():
        sout_ref[...] = s_ref[...]


def _gdn_scan(q, k, v, gcol, grow, s0, tm):
    t = q.shape[0]
    nblk = t // tm
    nch = tm // DN_CHUNK
    fwd = lambda w: pl.BlockSpec((tm, w), lambda i: (i, 0))
    bwd = lambda w: pl.BlockSpec((tm, w), lambda i: (nblk - 1 - i, 0))
    sds = jax.ShapeDtypeStruct
    s_shape = (N_GATES, HEAD_DIM, HEAD_DIM)
    return pl.pallas_call(
        functools.partial(_gdn_kernel, nblk),
        grid=(nblk,),
        in_specs=[fwd(DN_WIDTH), fwd(DN_WIDTH), fwd(DN_WIDTH), fwd(LANES),
                  pl.BlockSpec((nch, N_GATES, DN_CHUNK), lambda i: (i, 0, 0)),
                  bwd(DN_WIDTH), bwd(DN_WIDTH), bwd(DN_WIDTH), bwd(LANES),
                  pl.BlockSpec((nch, N_GATES, DN_CHUNK), lambda i: (nblk - 1 - i, 0, 0)),
                  _const_spec(s_shape)],
        out_specs=[fwd(DN_WIDTH), bwd(DN_WIDTH), pl.BlockSpec(s_shape, lambda i: (0, 0, 0))],
        out_shape=[sds((t, DN_WIDTH), F32), sds((t, DN_WIDTH), F32), sds(s_shape, F32)],
        scratch_shapes=[pltpu.VMEM(s_shape, F32),
                        pltpu.VMEM((nch, N_GATES, DN_CHUNK, HEAD_DIM), F32),
                        pltpu.VMEM((nch, N_GATES, 2 * DN_CHUNK, HEAD_DIM), BF16),
                        pltpu.VMEM((nch, N_GATES, DN_CHUNK, HEAD_DIM), BF16),
                        pltpu.VMEM((nch, N_GATES, DN_CHUNK, DN_CHUNK), BF16)],
        compiler_params=_cparams("arbitrary"),
        name="gdn_scan",
    )(q, k, v, gcol, grow, q, k, v, gcol, grow, s0)


S5_SW = 4 * S5_STATE


def _s5_matrices(a_re, a_im, log_dt, b_re, b_im, c_re, c_im, d):
    L, G, P, H = S5_CHUNK, S5_GROUPS, S5_STATE, S5_GROUP_CH
    hi = lax.Precision.HIGHEST
    a_re, a_im = a_re.astype(F32), a_im.astype(F32)
    dt = jnp.exp(log_dt.astype(F32))[..., None]
    n_re = jnp.exp(a_re * dt) * jnp.cos(a_im * dt) - 1.0
    n_im = jnp.exp(a_re * dt) * jnp.sin(a_im * dt)
    den = a_re * a_re + a_im * a_im
    co_re = ((n_re * a_re + n_im * a_im) / den)[..., None]
    co_im = ((n_im * a_re - n_re * a_im) / den)[..., None]
    br, bi = b_re.astype(F32)[None], b_im.astype(F32)[None]
    bb_re, bb_im = co_re * br - co_im * bi, co_re * bi + co_im * br
    tau = jnp.arange(L + 1, dtype=F32)[:, None, None, None]
    mag = jnp.exp(a_re[None] * dt[None] * tau)
    ang = a_im[None] * dt[None] * tau
    pw_re, pw_im = mag * jnp.cos(ang), mag * jnp.sin(ang)
    cr, ci = c_re.astype(F32), c_im.astype(F32)
    e_re = cr[None] * pw_re[:, :, :, None, :] - ci[None] * pw_im[:, :, :, None, :]
    e_im = cr[None] * pw_im[:, :, :, None, :] + ci[None] * pw_re[:, :, :, None, :]
    kk = (jnp.einsum('tdghp,dgpi->tdghi', e_re[:L], bb_re, precision=hi)
          - jnp.einsum('tdghp,dgpi->tdghi', e_im[:L], bb_im, precision=hi))
    s_i = jnp.arange(L)[:, None]
    t_i = jnp.arange(L)[None, :]
    lag = t_i - s_i
    kf = jnp.where((lag >= 0)[..., None, None, None], kk[jnp.clip(lag, 0, L - 1), 0], 0.0)
    kb = jnp.where((lag <= 0)[..., None, None, None], kk[jnp.clip(-lag, 0, L - 1), 1], 0.0)
    dskip = d.astype(F32).reshape(G, H)
    eye_t = (lag == 0).astype(F32)
    skip = eye_t[:, :, None, None, None] * (dskip[:, :, None] * jnp.eye(H, dtype=F32)[None])[None, None]
    toep = (kf + kb + skip).transpose(2, 0, 4, 1, 3).reshape(G, L * H, L * H)

    def loc(pr, pi, dr):
        w_re = pr[..., None] * bb_re[dr][None] - pi[..., None] * bb_im[dr][None]
        w_im = pr[..., None] * bb_im[dr][None] + pi[..., None] * bb_re[dr][None]
        f = lambda w: w.transpose(1, 0, 3, 2).reshape(G, L * H, P)
        return f(w_re), f(w_im)
    wf_re, wf_im = loc(pw_re[:L, 0][::-1], pw_im[:L, 0][::-1], 0)
    wb_re, wb_im = loc(pw_re[:L, 1], pw_im[:L, 1], 1)
    wcat = jnp.concatenate([toep, wf_re, wb_re, wf_im, wb_im], axis=-1).astype(BF16)

    carry = lambda e: e.transpose(1, 3, 0, 2).reshape(G, P, L * H)
    mf_re, mf_im = carry(e_re[1:, 0]), carry(-e_im[1:, 0])
    mb_re, mb_im = carry(e_re[1:, 1][::-1]), carry(-e_im[1:, 1][::-1])
    mcat = jnp.concatenate([mf_re, mb_re, mf_im, mb_im], axis=1).astype(BF16)
    al_re = jnp.concatenate([pw_re[L, 0], pw_re[L, 1]], axis=-1)
    al_im = jnp.concatenate([pw_im[L, 0], pw_im[L, 1]], axis=-1)
    al = jnp.stack([al_re.reshape(-1), al_im.reshape(-1)])
    return wcat, mcat, al


S5_HW = 2 * S5_STATE


def _s5_local_kernel(u_ref, w_ref, y_ref, xre_ref, xim_ref):
    r = jnp.dot(u_ref[...], w_ref[...], preferred_element_type=F32)
    y_ref[...] = r[:, :S5_CW]
    xre_ref[...] = r[:, S5_CW:S5_CW + S5_HW]
    xim_ref[...] = r[:, S5_CW + S5_HW:]


def _s5_local(ug, wcat):
    g, n, _ = ug.shape
    out = lambda w: pl.BlockSpec((n, w), lambda i: (0, i))
    sds = jax.ShapeDtypeStruct
    return pl.pallas_call(
        _s5_local_kernel,
        grid=(g,),
        in_specs=[pl.BlockSpec((None, n, S5_CW), lambda i: (i, 0, 0)),
                  pl.BlockSpec((None, S5_CW, S5_CW + S5_SW), lambda i: (i, 0, 0))],
        out_specs=[out(S5_CW), out(S5_HW), out(S5_HW)],
        out_shape=[sds((n, g * S5_CW), F32), sds((n, g * S5_HW), F32), sds((n, g * S5_HW), F32)],
        compiler_params=_cparams("parallel"),
        name="s5_local",
    )(ug, wcat)


def _s5_scan_kernel(nblk, xfr_ref, xfi_ref, xbr_ref, xbi_ref, al_ref, s0_ref,
                    cfr_ref, cfi_ref, cbr_ref, cbi_ref, sout_ref, s_ref):
    i = pl.program_id(0)

    @pl.when(i == 0)
    def _():
        s_ref[...] = s0_ref[...]

    nb = xfr_ref.shape[0]
    al_re, al_im = al_ref[0:1, :], al_ref[1:2, :]
    is_fwd = (lax.broadcasted_iota(jnp.int32, (1, al_ref.shape[1]), 1) % S5_HW) < S5_STATE

    def body(j, carry):
        re, im = carry
        rf, rb = pl.ds(j, 1), pl.ds(nb - 1 - j, 1)
        cfr_ref[rf, :] = re
        cfi_ref[rf, :] = im
        cbr_ref[rb, :] = re
        cbi_ref[rb, :] = im
        in_re = jnp.where(is_fwd, xfr_ref[rf, :], xbr_ref[rb, :])
        in_im = jnp.where(is_fwd, xfi_ref[rf, :], xbi_ref[rb, :])
        return al_re * re - al_im * im + in_re, al_re * im + al_im * re + in_im

    re, im = lax.fori_loop(0, nb, body, (s_ref[0:1, :], s_ref[1:2, :]))
    s_ref[0:1, :] = re
    s_ref[1:2, :] = im

    @pl.when(i == nblk - 1)
    def _():
        sout_ref[...] = s_ref[...]


def _s5_scan(x_re, x_im, al, s0, nb):
    n, w = x_re.shape
    nblk = n // nb
    fwd = pl.BlockSpec((nb, w), lambda i: (i, 0))
    bwd = pl.BlockSpec((nb, w), lambda i: (nblk - 1 - i, 0))
    sds = jax.ShapeDtypeStruct
    return pl.pallas_call(
        functools.partial(_s5_scan_kernel, nblk),
        grid=(nblk,),
        in_specs=[fwd, fwd, bwd, bwd, _const_spec(al.shape), _const_spec(s0.shape)],
        out_specs=[fwd, fwd, bwd, bwd, pl.BlockSpec(s0.shape, lambda i: (0, 0))],
        out_shape=[sds((n, w), F32)] * 4 + [sds(s0.shape, F32)],
        scratch_shapes=[pltpu.VMEM(s0.shape, F32)],
        compiler_params=_cparams("arbitrary"),
        name="s5_scan",
    )(x_re, x_im, x_re, x_im, al, s0)


def _s5_carry_kernel(y_ref, cfr_ref, cfi_ref, cbr_ref, cbi_ref, m_ref, o_ref):
    is_fwd = lax.broadcasted_iota(jnp.int32, (1, S5_HW), 1) < S5_STATE
    cin = jnp.concatenate([jnp.where(is_fwd, cfr_ref[...], cbr_ref[...]),
                           jnp.where(is_fwd, cfi_ref[...], cbi_ref[...])], axis=1).astype(BF16)
    o_ref[...] = y_ref[...] + jnp.dot(cin, m_ref[...], preferred_element_type=F32)


def _s5_carry(y, cins, mcat):
    n = y.shape[0]
    g = mcat.shape[0]
    blk = lambda w: pl.BlockSpec((n, w), lambda i: (0, i))
    return pl.pallas_call(
        _s5_carry_kernel,
        grid=(g,),
        in_specs=[blk(S5_CW)] + [blk(S5_HW)] * 4 + [pl.BlockSpec((None, S5_SW, S5_CW), lambda i: (i, 0, 0))],
        out_specs=blk(S5_CW),
        out_shape=jax.ShapeDtypeStruct(y.shape, F32),
        compiler_params=_cparams("parallel"),
        name="s5_carry",
    )(y, *cins, mcat)


def _s5_mixer(u, mats, s0):
    wcat, mcat, al = mats
    t = u.shape[0]
    n = t // S5_CHUNK
    ug = u.reshape(n, S5_CHUNK, S5_GROUPS, S5_GROUP_CH).transpose(2, 0, 1, 3).reshape(S5_GROUPS, n, S5_CW)
    y_loc, x_re, x_im = _s5_local(ug, wcat)
    *cins, s_fin = _s5_scan(x_re, x_im, al, s0, min(n, 128))
    y = _s5_carry(y_loc, cins, mcat)
    y = y.reshape(n, S5_GROUPS, S5_CHUNK, S5_GROUP_CH).transpose(0, 2, 1, 3).reshape(t, S5_WIDTH)
    return y, s_fin


ROUTER_E0 = MOE_GROUPS
ROUTE_SEL = 32
OUT_ROWS = 128


def _gelu_tanh(x):
    return 0.5 * x * (1.0 + jnp.tanh(math.sqrt(2.0 / math.pi) * (x + 0.044715 * (x * x * x))))


def _route(logits):
    ninf = float("-inf")
    lane = lax.broadcasted_iota(jnp.int32, (1, LANES), 1)
    lanef = lane.astype(F32)
    first = lambda hit: jnp.min(jnp.where(hit, lanef, float(LANES)), axis=-1, keepdims=True)
    gl = jnp.where(lane < MOE_GROUPS, logits, ninf)
    gmax = jnp.max(gl, axis=-1, keepdims=True)
    pg_top = 1.0 / jnp.sum(jnp.exp(gl - gmax), axis=-1, keepdims=True)
    base = ROUTER_E0 + MOE_PER_GROUP * first(gl == gmax)
    el = jnp.where((lanef >= base) & (lanef < base + MOE_PER_GROUP), logits, ninf)
    emax = jnp.max(el, axis=-1, keepdims=True)
    esum = jnp.sum(jnp.exp(el - emax), axis=-1, keepdims=True)
    i1 = first(el == emax)
    el2 = jnp.where(lanef == i1, ninf, el)
    emax2 = jnp.max(el2, axis=-1, keepdims=True)
    i2 = first(el2 == emax2)
    p1 = 1.0 / esum
    p2 = jnp.exp(emax2 - emax) / esum
    w1 = pg_top * (p1 / (p1 + p2))
    w2 = pg_top * (p2 / (p1 + p2))
    comb = jnp.where(lanef == i1, w1, 0.0) + jnp.where(lanef == i2, w2, 0.0)
    sel = (jnp.where(lane == ROUTE_SEL, i1 - ROUTER_E0, 0.0) + jnp.where(lane == ROUTE_SEL + 1, i2 - ROUTER_E0, 0.0)
           + jnp.where(lane == ROUTE_SEL + 2, w1, 0.0) + jnp.where(lane == ROUTE_SEL + 3, w2, 0.0))
    return comb + sel


def _bucket_rank(route, carry):
    tm = route.shape[0]
    lanef = lax.broadcasted_iota(jnp.int32, (1, LANES), 1).astype(F32)
    e1, e2 = route[:, ROUTE_SEL:ROUTE_SEL + 1], route[:, ROUTE_SEL + 1:ROUTE_SEL + 2]
    lo, hi = jnp.minimum(e1, e2), jnp.maximum(e1, e2)
    grp = jnp.floor(lo * (1.0 / MOE_PER_GROUP))
    a, b = lo - MOE_PER_GROUP * grp, hi - MOE_PER_GROUP * grp
    bucket = grp * MOE_PAIRS + (a * (2 * MOE_PER_GROUP - 1 - a) * 0.5 + (b - a - 1.0))
    onehot = jnp.where(lanef == bucket, 1.0, 0.0)
    r = lax.broadcasted_iota(jnp.int32, (tm, tm), 0)
    c = lax.broadcasted_iota(jnp.int32, (tm, tm), 1)
    before = jnp.where(c < r, 1.0, 0.0).astype(BF16)
    seen = jnp.dot(before, onehot.astype(BF16), preferred_element_type=F32) + carry
    rank = jnp.sum(onehot * seen, axis=-1, keepdims=True)
    lane = lax.broadcasted_iota(jnp.int32, (1, LANES), 1)
    route = route + jnp.where(lane == ROUTE_SEL + 4, bucket, 0.0) + jnp.where(lane == ROUTE_SEL + 5, rank, 0.0)
    return route, carry + jnp.sum(onehot, axis=0, keepdims=True)


def _out_kernel(routed, att_ref, of_ref, ob_ref, gate_ref, y_ref, x_ref, g1_ref, sh2_ref, sc2_ref, n2_ref, dnw_ref,
                bglu_ref, wglu_ref, woa_ref, wob_ref, woc_ref, wr_ref, br_ref, xo_ref, comb_ref, aux_ref, cnt_ref=None):
    if routed:
        @pl.when(pl.program_id(0) == 0)
        def _():
            cnt_ref[...] = jnp.zeros(cnt_ref.shape, F32)
        cnt = cnt_ref[0:1, :]
    tm = x_ref.shape[0]
    for c0 in range(0, tm, OUT_ROWS):
        rows = slice(c0, c0 + OUT_ROWS)
        o = of_ref[rows, :] + ob_ref[rows, :]
        gate = _silu(gate_ref[rows, :])
        dn = jnp.concatenate(
            [_rms_rows(o[:, h * HEAD_DIM:(h + 1) * HEAD_DIM]) * dnw_ref[...] for h in range(DN_HEADS)], axis=1) * gate
        z = _gelu_tanh(y_ref[rows, :])
        s5 = z * jax.nn.sigmoid(jnp.dot(z.astype(BF16), wglu_ref[...], preferred_element_type=F32) + bglu_ref[...])
        mix = (lax.dot_general(att_ref[:, rows], woa_ref[...], _TN, preferred_element_type=F32)
               + jnp.dot(dn.astype(BF16), wob_ref[...], preferred_element_type=F32)
               + jnp.dot(s5.astype(BF16), woc_ref[...], preferred_element_type=F32))
        x = x_ref[rows, :] + g1_ref[...] * mix
        h = _rms_rows(x) * n2_ref[...] * (1.0 + sc2_ref[...]) + sh2_ref[...]
        h_hi = h.astype(BF16)
        h_lo = (h - h_hi.astype(F32)).astype(BF16)
        logits = (jnp.dot(h_hi, wr_ref[0], preferred_element_type=F32) + jnp.dot(h_lo, wr_ref[0], preferred_element_type=F32)
                  + jnp.dot(h_hi, wr_ref[1], preferred_element_type=F32) + br_ref[...])
        route = _route(logits)
        if not routed:
            xo_ref[rows, :] = x
            aux_ref[rows, :] = h_hi
            comb_ref[rows, :] = route
            continue
        route, cnt = _bucket_rank(route, cnt)
        xo_ref[rows, :D_MODEL] = x
        xo_ref[rows, D_MODEL:] = route
        comb_ref[rows, :] = route
    if routed:
        cnt_ref[0:1, :] = cnt
        aux_ref[...] = cnt_ref[...]


def _mix_out(att_t, o_f, o_b, gate, y, x, mod, lw, tm, routed):
    t = x.shape[0]
    row = lambda w: pl.BlockSpec((tm, w), lambda i: (i, 0))
    vec = lambda a: _const_spec(a.shape)
    consts = [mod["g1"], mod["sh2"], mod["sc2"], lw["norm2_w"], lw["dnw"], lw["bglu"], lw["wglu"], lw["wo_a"],
              lw["wo_b"], lw["wo_c"], lw["wr"], lw["br"]]
    sds = jax.ShapeDtypeStruct
    if routed:
        out_specs = [row(D_MODEL + LANES), row(LANES), pl.BlockSpec((SUBLANES, LANES), lambda i: (0, 0))]
        out_shape = [sds((t, D_MODEL + LANES), F32), sds((t, LANES), F32), sds((SUBLANES, LANES), F32)]
        scratch = [pltpu.VMEM((SUBLANES, LANES), F32)]
    else:
        out_specs = [row(D_MODEL), row(LANES), row(D_MODEL)]
        out_shape = [sds((t, D_MODEL), F32), sds((t, LANES), F32), sds((t, D_MODEL), BF16)]
        scratch = []
    return pl.pallas_call(
        functools.partial(_out_kernel, routed),
        grid=(t // tm,),
        in_specs=[pl.BlockSpec((ATTN_WIDTH, tm), lambda i: (0, i)), row(DN_WIDTH), row(DN_WIDTH), row(DN_WIDTH),
                  row(S5_WIDTH), row(D_MODEL)] + [vec(a) for a in consts],
        out_specs=out_specs, out_shape=out_shape, scratch_shapes=scratch,
        compiler_params=_cparams("arbitrary" if routed else "parallel"),
        name="mix_out",
    )(att_t, o_f, o_b, gate, y, x, *consts)


def _moe_kernel(h_ref, comb_ref, x_ref, g2_ref, wg_ref, wu_ref, wd_ref, o_ref, acc_ref):
    e = pl.program_id(1)

    @pl.when(e == 0)
    def _():
        acc_ref[...] = jnp.zeros(acc_ref.shape, F32)

    h = h_ref[...]
    mid = _silu(jnp.dot(h, wg_ref[...], preferred_element_type=F32)) * jnp.dot(h, wu_ref[...], preferred_element_type=F32)
    lane = lax.broadcasted_iota(jnp.int32, (1, LANES), 1)
    cw = jnp.sum(jnp.where(lane == e + ROUTER_E0, comb_ref[...], 0.0), axis=-1, keepdims=True)
    acc_ref[...] += cw * jnp.dot(mid.astype(BF16), wd_ref[...], preferred_element_type=F32)

    @pl.when(e == MOE_EXPERTS - 1)
    def _():
        o_ref[...] = x_ref[...] + g2_ref[...] * acc_ref[...]


def _moe(h, comb, x, g2, lw, tm):
    t = x.shape[0]
    row = lambda w: pl.BlockSpec((tm, w), lambda i, e: (i, 0))
    return pl.pallas_call(
        _moe_kernel,
        grid=(t // tm, MOE_EXPERTS),
        in_specs=[row(D_MODEL), row(LANES), row(D_MODEL), _const_spec(g2.shape),
                  pl.BlockSpec((None, D_MODEL, MOE_HIDDEN), lambda i, e: (e, 0, 0)),
                  pl.BlockSpec((None, D_MODEL, MOE_HIDDEN), lambda i, e: (e, 0, 0)),
                  pl.BlockSpec((None, MOE_HIDDEN, D_MODEL), lambda i, e: (e, 0, 0))],
        out_specs=row(D_MODEL),
        out_shape=jax.ShapeDtypeStruct((t, D_MODEL), F32),
        scratch_shapes=[pltpu.VMEM((tm, D_MODEL), F32)],
        compiler_params=_cparams("parallel", "arbitrary"),
        name="moe",
    )(h, comb, x, g2, lw["w_gate"], lw["w_up"], lw["w_down"])


MOE_PAIRS = MOE_PER_GROUP * (MOE_PER_GROUP - 1) // 2
MOE_BUCKETS = MOE_GROUPS * MOE_PAIRS
MOE_TM = 128
MOE_SCATTER_GROUP = 8


def _moe_plan(route, counts, t):
    n_tiles = t // MOE_TM + MOE_BUCKETS
    bucket = route[:, ROUTE_SEL + 4].astype(jnp.int32)
    rank = route[:, ROUTE_SEL + 5].astype(jnp.int32)
    counts = counts[0, :MOE_BUCKETS].astype(jnp.int32)
    tiles = (counts + MOE_TM - 1) // MOE_TM
    tile_end = jnp.cumsum(tiles)
    tile_off = tile_end - tiles
    slot = tile_off[bucket] * MOE_TM + rank
    tok = jnp.zeros(((n_tiles + 1) * MOE_TM,), jnp.int32).at[slot].set(jnp.arange(t, dtype=jnp.int32))
    tile = jnp.arange(n_tiles)
    tb = jnp.minimum(jnp.sum((tile[:, None] >= tile_end[None, :]).astype(jnp.int32), axis=1), MOE_BUCKETS - 1)
    nvalid = jnp.clip(counts[tb] - (tile - tile_off[tb]) * MOE_TM, 0, MOE_TM)
    nvalid = jnp.where(tile < tile_end[-1], nvalid, 0).astype(jnp.int32)
    pairs = [(i, j) for i in range(MOE_PER_GROUP) for j in range(i + 1, MOE_PER_GROUP)]
    pa = jnp.array([p[0] for p in pairs], jnp.int32)
    pb = jnp.array([p[1] for p in pairs], jnp.int32)
    grp = (tb // MOE_PAIRS) * MOE_PER_GROUP
    e_lo = (grp + pa[tb % MOE_PAIRS]).astype(jnp.int32)
    e_hi = (grp + pb[tb % MOE_PAIRS]).astype(jnp.int32)
    return tok, nvalid, e_lo, e_hi


def _moe_routed_kernel(n_tiles, tok_ref, nv_ref, elo_ref, ehi_ref,
                       x_hbm, g2_ref, sh2_ref, sc2_ref, n2_ref, wgl_ref, wul_ref, wdl_ref,
                       wgh_ref, wuh_ref, wdh_ref, o_hbm, xbuf, obuf, gsem, ssem):
    t = pl.program_id(0)
    slot = lax.rem(t, 2)

    def row_copy(tile, s, r, scatter):
        tok = tok_ref[tile * MOE_TM + r]
        if scatter:
            return pltpu.make_async_copy(obuf.at[s, pl.ds(r, 1)], o_hbm.at[pl.ds(tok, 1)], ssem.at[s])
        return pltpu.make_async_copy(x_hbm.at[pl.ds(tok, 1)], xbuf.at[s, pl.ds(r, 1)], gsem.at[s])

    def start_gather(tile, s):
        for r in range(MOE_TM):
            row_copy(tile, s, r, False).start()

    def wait_gather(s):
        pltpu.make_async_copy(x_hbm.at[pl.ds(0, MOE_TM)], xbuf.at[s], gsem.at[s]).wait()

    def start_scatter(tile, s):
        n = nv_ref[tile]
        for g in range(MOE_TM // MOE_SCATTER_GROUP):
            @pl.when((g + 1) * MOE_SCATTER_GROUP <= n)
            def _():
                for r in range(g * MOE_SCATTER_GROUP, (g + 1) * MOE_SCATTER_GROUP):
                    row_copy(tile, s, r, True).start()

        def body(r, carry):
            row_copy(tile, s, r, True).start()
            return carry
        lax.fori_loop(n - lax.rem(n, MOE_SCATTER_GROUP), n, body, 0)

    def wait_scatter(tile, s):
        n = nv_ref[tile]
        for bit in range(MOE_TM.bit_length()):
            rows = 1 << bit

            @pl.when((n >> bit) & 1 == 1)
            def _():
                pltpu.make_async_copy(obuf.at[s, pl.ds(0, rows)], o_hbm.at[pl.ds(0, rows)], ssem.at[s]).wait()

    @pl.when(t == 0)
    def _():
        start_gather(0, 0)

    wait_gather(slot)

    @pl.when(t >= 2)
    def _():
        wait_scatter(t - 2, slot)

    start_gather(t + 1, 1 - slot)
    x = xbuf[slot, :, :D_MODEL]
    rec = xbuf[slot, :, D_MODEL:]
    h = (_rms_rows(x) * n2_ref[...] * (1.0 + sc2_ref[...]) + sh2_ref[...]).astype(BF16)
    top1_is_lo = rec[:, ROUTE_SEL:ROUTE_SEL + 1] == elo_ref[t].astype(F32)
    w1, w2 = rec[:, ROUTE_SEL + 2:ROUTE_SEL + 3], rec[:, ROUTE_SEL + 3:ROUTE_SEL + 4]
    w = (jnp.where(top1_is_lo, w1, w2), jnp.where(top1_is_lo, w2, w1))
    y = None
    for k, (wg_ref, wu_ref, wd_ref) in enumerate(((wgl_ref, wul_ref, wdl_ref), (wgh_ref, wuh_ref, wdh_ref))):
        mid = (_silu(jnp.dot(h, wg_ref[...], preferred_element_type=F32))
               * jnp.dot(h, wu_ref[...], preferred_element_type=F32))
        yk = w[k] * jnp.dot(mid.astype(BF16), wd_ref[...], preferred_element_type=F32)
        y = yk if y is None else y + yk
    obuf[slot] = x + g2_ref[...] * y
    start_scatter(t, slot)

    @pl.when(t == n_tiles - 1)
    def _():
        wait_gather(1 - slot)
        if n_tiles >= 2:
            wait_scatter(t - 1, 1 - slot)
        wait_scatter(t, slot)


def _moe_routed(x, route, counts, mod, lw):
    t = x.shape[0]
    tok, nvalid, e_lo, e_hi = _moe_plan(route, counts, t)
    n_tiles = nvalid.shape[0]
    vec = lambda a: pl.BlockSpec(a.shape, lambda i, *_: (0,) * a.ndim, pipeline_mode=pl.Buffered(1))
    wspec = lambda shape, which: pl.BlockSpec(
        (None,) + shape, (lambda i, tok, nv, elo, ehi: (elo[i], 0, 0)) if which == 0
        else (lambda i, tok, nv, elo, ehi: (ehi[i], 0, 0)))
    up, down = (D_MODEL, MOE_HIDDEN), (MOE_HIDDEN, D_MODEL)
    consts = [mod["g2"], mod["sh2"], mod["sc2"], lw["norm2_w"]]
    grid_spec = pltpu.PrefetchScalarGridSpec(
        num_scalar_prefetch=4,
        grid=(n_tiles,),
        in_specs=[pl.BlockSpec(memory_space=pl.ANY)] + [vec(a) for a in consts]
                 + [wspec(up, 0), wspec(up, 0), wspec(down, 0), wspec(up, 1), wspec(up, 1), wspec(down, 1)],
        out_specs=pl.BlockSpec(memory_space=pl.ANY),
        scratch_shapes=[pltpu.VMEM((2, MOE_TM, D_MODEL + LANES), F32), pltpu.VMEM((2, MOE_TM, D_MODEL), F32),
                        pltpu.SemaphoreType.DMA((2,)), pltpu.SemaphoreType.DMA((2,))],
    )
    return pl.pallas_call(
        functools.partial(_moe_routed_kernel, n_tiles),
        grid_spec=grid_spec,
        out_shape=jax.ShapeDtypeStruct((t, D_MODEL), F32),
        compiler_params=_cparams("arbitrary"),
        name="moe_routed",
    )(tok, nvalid, e_lo, e_hi, x, *consts, lw["w_gate"], lw["w_up"], lw["w_down"],
      lw["w_gate"], lw["w_up"], lw["w_down"])


def _rope_tables(n_tokens):
    rows = n_tokens // GRID_W
    half = HEAD_DIM // 2
    inv = ROPE_THETA ** (-jnp.arange(0, half, 2, dtype=F32) / half)
    nf = inv.shape[0]
    ang_r = jnp.arange(rows, dtype=F32)[:, None] * inv[None, :]
    ang_c = jnp.arange(GRID_W, dtype=F32)[:, None] * inv[None, :]

    def table(fn):
        r = jnp.broadcast_to(fn(ang_r)[:, None, :], (rows, GRID_W, nf))
        c = jnp.broadcast_to(fn(ang_c)[None, :, :], (rows, GRID_W, nf))
        return jnp.concatenate([r, r, c, c], axis=-1).reshape(rows * GRID_W, HEAD_DIM)

    cos, sin = table(jnp.cos), table(jnp.sin)
    return cos, sin, cos.T, sin.T


def _layer_weights(l, p):
    o = [0]
    for s in IN_SPLITS:
        o.append(o[-1] + s)
    w_in = p["w_in"][l]
    seg = lambda i: w_in[:, o[i]:o[i + 1]]
    ba = jnp.concatenate([seg(5), seg(6)], axis=1)
    lanes16 = lambda v: jnp.zeros((LANES,), F32).at[N_GATES:2 * N_GATES].set(v.reshape(N_GATES).astype(F32))
    alog, dtb = lanes16(p["dn_a_log"][l]), lanes16(p["dn_dt_bias"][l])
    w_out = p["w_out"][l]
    wr = jnp.zeros((D_MODEL, LANES), F32)
    wr = wr.at[:, :MOE_GROUPS].set(p["router_g_w"][l]).at[:, ROUTER_E0:ROUTER_E0 + MOE_EXPERTS].set(p["router_e_w"][l])
    br = jnp.zeros((1, LANES), F32)
    br = br.at[0, :MOE_GROUPS].set(p["router_g_b"][l]).at[0, ROUTER_E0:ROUTER_E0 + MOE_EXPERTS].set(p["router_e_b"][l])
    return dict(
        norm1_w=p["norm1_w"][l][None], norm2_w=p["norm2_w"][l][None],
        wqt=seg(0).T.astype(BF16), wk=seg(1).astype(BF16), wvt=seg(2).T.astype(BF16), wdn=seg(3).astype(BF16),
        wg=seg(4).astype(BF16), wba=jnp.pad(ba, ((0, 0), (0, LANES - 2 * N_GATES))).astype(BF16),
        wbat=ba.T.astype(BF16), ws5=seg(7).astype(BF16),
        qn_col=p["q_norm_w"][l][:, None], kn_row=p["k_norm_w"][l][None],
        conv_w=jnp.pad(p["dn_conv_w"][l], ((0, SUBLANES - DN_CONV), (0, 0))),
        alog_row=alog[None], dtb_row=dtb[None], alog_col=alog[:2 * N_GATES, None], dtb_col=dtb[:2 * N_GATES, None],
        dnw=p["dn_out_norm_w"][l][None], wglu=p["s5_w_glu"][l].astype(BF16), bglu=p["s5_b_glu"][l][None],
        wo_a=w_out[:ATTN_WIDTH].astype(BF16), wo_b=w_out[ATTN_WIDTH:ATTN_WIDTH + DN_WIDTH].astype(BF16),
        wo_c=w_out[ATTN_WIDTH + DN_WIDTH:].astype(BF16), br=br,
        wr=jnp.stack([wr.astype(BF16), (wr - wr.astype(BF16).astype(F32)).astype(BF16)]),
        w_gate=p["w_gate"][l].astype(BF16), w_up=p["w_up"][l].astype(BF16), w_down=p["w_down"][l].astype(BF16),
        s5=_s5_matrices(p["s5_a_re"][l], p["s5_a_im"][l], p["s5_log_dt"][l], p["s5_b_re"][l], p["s5_b_im"][l],
                        p["s5_c_re"][l], p["s5_c_im"][l], p["s5_d"][l]),
    )


TM_CTX = 256
TM_IN = 256
TM_SEQ = 512


def _mixers(x, mod, lw, tabs, rope, s_dn, s_s5, tm_in, tm):
    qt, k, vt, dn, gate, ba, bat, u = _in_proj(x, mod, lw, tabs, rope, tm_in)
    q_dn, k_dn, v_dn, gcol, grow = _gdn_prep(dn, ba, bat, lw, tm)
    o_f, o_b, s_dn = _gdn_scan(q_dn, k_dn, v_dn, gcol, grow, s_dn, tm)
    y, s_s5 = _s5_mixer(u, lw["s5"], s_s5)
    return dict(qt=qt, k=k, vt=vt, gate=gate, o_f=o_f, o_b=o_b, y=y), s_dn, s_s5


def kernel(x, c, ctx, c_ctx, w_ada, b_ada, norm1_w, norm2_w, w_in, q_norm_w, k_norm_w, dn_conv_w, dn_a_log, dn_dt_bias,
           dn_out_norm_w, s5_a_re, s5_a_im, s5_log_dt, s5_b_re, s5_b_im, s5_c_re, s5_c_im, s5_d, s5_w_glu, s5_b_glu,
           w_out, router_g_w, router_g_b, router_e_w, router_e_b, w_gate, w_up, w_down):
    p = dict(norm1_w=norm1_w, norm2_w=norm2_w, w_in=w_in, q_norm_w=q_norm_w, k_norm_w=k_norm_w, dn_conv_w=dn_conv_w,
             dn_a_log=dn_a_log, dn_dt_bias=dn_dt_bias, dn_out_norm_w=dn_out_norm_w, s5_a_re=s5_a_re, s5_a_im=s5_a_im,
             s5_log_dt=s5_log_dt, s5_b_re=s5_b_re, s5_b_im=s5_b_im, s5_c_re=s5_c_re, s5_c_im=s5_c_im, s5_d=s5_d,
             s5_w_glu=s5_w_glu, s5_b_glu=s5_b_glu, w_out=w_out, router_g_w=router_g_w, router_g_b=router_g_b,
             router_e_w=router_e_w, router_e_b=router_e_b, w_gate=w_gate, w_up=w_up, w_down=w_down)
    assert x.shape[0] == 1 and ctx.shape[0] == 1
    xl, xc = x[0], ctx[0]
    n_ctx = xc.shape[0]
    cvec = jnp.zeros((SUBLANES, D_MODEL), F32).at[0].set(c[0]).at[1].set(c_ctx)
    mods = _adaln_mod(cvec, w_ada, b_ada)
    tabs = _rope_tables(xl.shape[0])
    tabs_c = (tabs[0][:n_ctx], tabs[1][:n_ctx], tabs[2][:, :n_ctx], tabs[3][:, :n_ctx])
    names = ("sh1", "sc1", "g1", "sh2", "sc2", "g2")
    for l in range(DEPTH):
        lw = _layer_weights(l, p)
        mod_l = {n: mods[l, 0:1, i * D_MODEL:(i + 1) * D_MODEL] for i, n in enumerate(names)}
        mod_c = {n: mods[l, 1:2, i * D_MODEL:(i + 1) * D_MODEL] for i, n in enumerate(names)}
        s_dn0 = jnp.zeros((N_GATES, HEAD_DIM, HEAD_DIM), F32)
        s_s50 = jnp.zeros((2, S5_GROUPS * S5_HW), F32)
        mc, s_dn, s_s5 = _mixers(xc, mod_c, lw, tabs_c, False, s_dn0, s_s50, TM_CTX, TM_CTX)
        ml, _, _ = _mixers(xl, mod_l, lw, tabs, True, s_dn, s_s5, TM_IN, TM_SEQ)
        att_l = _attention(ml["qt"], (mc["k"], mc["vt"]), (ml["k"], ml["vt"]))
        xl, route, counts = _mix_out(att_l, ml["o_f"], ml["o_b"], ml["gate"], ml["y"], xl, mod_l, lw, TM_SEQ, True)
        xl = _moe_routed(xl, route, counts, mod_l, lw)
        if l < DEPTH - 1:
            att_c = _attention(mc["qt"], (mc["k"], mc["vt"]))
            xc, comb, h2 = _mix_out(att_c, mc["o_f"], mc["o_b"], mc["gate"], mc["y"], xc, mod_c, lw, TM_CTX, False)
            xc = _moe(h2, comb, xc, mod_c["g2"], lw, TM_CTX)
    return xl[None]
```

```python
import functools
import math

import jax
import jax.numpy as jnp
from jax import lax
from jax.experimental import pallas as pl
from jax.experimental.pallas import tpu as pltpu

F32 = jnp.float32
BF16 = jnp.bfloat16

D_MODEL = 2048
DEPTH = 2
GRID_W = 64
N_DIR = 2
HEAD_DIM = 128
ATTN_WIDTH = D_MODEL // 2
ATTN_Q_HEADS = ATTN_WIDTH // HEAD_DIM
ATTN_KV_HEADS = ATTN_Q_HEADS // 4
KV_WIDTH = ATTN_KV_HEADS * HEAD_DIM
ROPE_THETA = 10000.0
DN_WIDTH = D_MODEL // 4
DN_HEADS = DN_WIDTH // HEAD_DIM
DN_CONV = 5
DN_CHUNK = 64
S5_WIDTH = D_MODEL // 4
S5_GROUP_CH = 16
S5_GROUPS = S5_WIDTH // S5_GROUP_CH
S5_STATE = 64
MIX_WIDTH = ATTN_WIDTH + DN_WIDTH + S5_WIDTH
IN_SPLITS = (ATTN_WIDTH, KV_WIDTH, KV_WIDTH, 3 * DN_WIDTH, DN_WIDTH, N_DIR * DN_HEADS, N_DIR * DN_HEADS, S5_WIDTH)
MOE_GROUPS = 4
MOE_PER_GROUP = 4
MOE_EXPERTS = MOE_GROUPS * MOE_PER_GROUP
MOE_HIDDEN = D_MODEL // 4
EPS = 1e-6

LANES = 128
SUBLANES = 8
VMEM_LIMIT_BYTES = 56 * 1024 * 1024

N_GATES = N_DIR * DN_HEADS
S5_CHUNK = 16
S5_CW = S5_CHUNK * S5_GROUP_CH


def _cparams(*sem):
    return pltpu.CompilerParams(dimension_semantics=sem, vmem_limit_bytes=VMEM_LIMIT_BYTES)


def _const_spec(shape):
    nd = len(shape)
    return pl.BlockSpec(shape, lambda *_: (0,) * nd, pipeline_mode=pl.Buffered(1))


def _silu(x):
    return x * jax.nn.sigmoid(x)


MOD_TN = 1024


def _mod_kernel(cv_ref, w_ref, b_ref, o_ref):
    s = _silu(cv_ref[...]).astype(BF16)
    o_ref[...] = jnp.dot(s, w_ref[...].astype(BF16), preferred_element_type=F32) + b_ref[...]


def _adaln_mod(cvec, w_ada, b_ada):
    n6 = w_ada.shape[-1]
    return pl.pallas_call(
        _mod_kernel,
        grid=(DEPTH, n6 // MOD_TN),
        in_specs=[
            pl.BlockSpec((SUBLANES, D_MODEL), lambda l, j: (0, 0)),
            pl.BlockSpec((None, D_MODEL, MOD_TN), lambda l, j: (l, 0, j)),
            pl.BlockSpec((None, 1, MOD_TN), lambda l, j: (l, 0, j)),
        ],
        out_specs=pl.BlockSpec((None, SUBLANES, MOD_TN), lambda l, j: (l, 0, j)),
        out_shape=jax.ShapeDtypeStruct((DEPTH, SUBLANES, n6), F32),
        compiler_params=_cparams("parallel", "parallel"),
        name="adaln_mod",
    )(cvec, w_ada, b_ada.reshape(DEPTH, 1, n6))


def _rms_rows(x):
    return x * lax.rsqrt(jnp.mean(x * x, axis=-1, keepdims=True) + EPS)


def _in_kernel(rope, x_ref, nw_ref, sh_ref, sc_ref, wqt_ref, wk_ref, wvt_ref, wdn_ref, wg_ref, wba_ref, wbat_ref,
               ws5_ref, qn_ref, kn_ref, cos_ref, sin_ref, cost_ref, sint_ref,
               qt_ref, k_ref, vt_ref, dn_ref, g_ref, ba_ref, bat_ref, u_ref):
    x = x_ref[...]
    h = (_rms_rows(x) * nw_ref[...] * (1.0 + sc_ref[...]) + sh_ref[...]).astype(BF16)
    nt = (((1,), (1,)), ((), ()))

    qt = lax.dot_general(wqt_ref[...], h, nt, preferred_element_type=F32)
    q_scale = HEAD_DIM ** -0.5 * math.log2(math.e)
    for hd in range(ATTN_Q_HEADS):
        qh = qt[hd * HEAD_DIM:(hd + 1) * HEAD_DIM, :]
        qh = qh * lax.rsqrt(jnp.mean(qh * qh, axis=0, keepdims=True) + EPS) * qn_ref[...]
        if rope:
            a, b, c, d = (qh[i * 32:(i + 1) * 32, :] for i in range(4))
            rot = jnp.concatenate([-b, a, -d, c], axis=0)
            qh = qh * cost_ref[...] + rot * sint_ref[...]
        qt_ref[hd * HEAD_DIM:(hd + 1) * HEAD_DIM, :] = (qh * q_scale).astype(BF16)

    k = jnp.dot(h, wk_ref[...], preferred_element_type=F32)
    lane = lax.broadcasted_iota(jnp.int32, (1, HEAD_DIM), 1)
    first_half = (lane % 64) < 32
    for hd in range(ATTN_KV_HEADS):
        kh = _rms_rows(k[:, hd * HEAD_DIM:(hd + 1) * HEAD_DIM]) * kn_ref[...]
        if rope:
            rot = jnp.where(first_half, -pltpu.roll(kh, 96, 1), pltpu.roll(kh, 32, 1))
            kh = kh * cos_ref[...] + rot * sin_ref[...]
        k_ref[:, hd * HEAD_DIM:(hd + 1) * HEAD_DIM] = kh.astype(BF16)

    vt_ref[...] = lax.dot_general(wvt_ref[...], h, nt, preferred_element_type=F32).astype(BF16)
    dn_ref[...] = jnp.dot(h, wdn_ref[...], preferred_element_type=F32)
    g_ref[...] = jnp.dot(h, wg_ref[...], preferred_element_type=F32)
    ba_ref[...] = jnp.dot(h, wba_ref[...], preferred_element_type=F32)
    bat_ref[...] = lax.dot_general(wbat_ref[...], h, nt, preferred_element_type=F32)
    u_ref[...] = jnp.dot(h, ws5_ref[...], preferred_element_type=F32).astype(BF16)


def _in_proj(x, mod, lw, rope_tabs, rope, tm):
    t = x.shape[0]
    cos, sin, cost, sint = rope_tabs
    row = lambda w: pl.BlockSpec((tm, w), lambda i: (i, 0))
    col = lambda w: pl.BlockSpec((w, tm), lambda i: (0, i))
    vec = lambda a: _const_spec(a.shape)
    ws = [lw["wqt"], lw["wk"], lw["wvt"], lw["wdn"], lw["wg"], lw["wba"], lw["wbat"], lw["ws5"]]
    in_specs = ([row(D_MODEL), vec(lw["norm1_w"]), vec(mod["sh1"]), vec(mod["sc1"])] + [vec(w) for w in ws]
                + [vec(lw["qn_col"]), vec(lw["kn_row"]), row(HEAD_DIM), row(HEAD_DIM), col(HEAD_DIM), col(HEAD_DIM)])
    out_specs = [col(ATTN_WIDTH), row(KV_WIDTH), pl.BlockSpec((None, KV_WIDTH, tm), lambda i: (i, 0, 0)), row(3 * DN_WIDTH), row(DN_WIDTH), row(LANES),
                 col(2 * N_GATES), row(S5_WIDTH)]
    sds = jax.ShapeDtypeStruct
    out_shape = [sds((ATTN_WIDTH, t), BF16), sds((t, KV_WIDTH), BF16), sds((t // tm, KV_WIDTH, tm), BF16),
                 sds((t, 3 * DN_WIDTH), F32), sds((t, DN_WIDTH), F32), sds((t, LANES), F32),
                 sds((2 * N_GATES, t), F32), sds((t, S5_WIDTH), BF16)]
    return pl.pallas_call(
        functools.partial(_in_kernel, rope),
        grid=(t // tm,),
        in_specs=in_specs, out_specs=out_specs, out_shape=out_shape,
        compiler_params=_cparams("parallel"),
        name="in_proj",
    )(x, lw["norm1_w"], mod["sh1"], mod["sc1"], *ws, lw["qn_col"], lw["kn_row"], cos, sin, cost, sint)


ATT_TQ = 256
ATT_TK = 256
GQA = ATTN_Q_HEADS // ATTN_KV_HEADS


ATT_UNROLL = 4
ATT_ONES = 16


def _attn_kernel(n_lat, qt_ref, kc_ref, vtc_ref, *refs):
    if n_lat:
        kl_ref, vtl_ref, ot_ref, sa_ref, sb_ref, m_ref, acc_ref = refs
    else:
        ot_ref, sa_ref, sb_ref, m_ref, acc_ref = refs
    tq = qt_ref.shape[1]
    q = jnp.concatenate([qt_ref[h * HEAD_DIM:(h + 1) * HEAD_DIM, :] for h in range(GQA)], axis=1)
    ones = jnp.ones((ATT_ONES, ATT_TK), BF16)

    def scores(kb, dst_ref):
        dst_ref[...] = jnp.dot(kb, q, preferred_element_type=F32)

    def lat_keys(j):
        return kl_ref[pl.ds(pl.multiple_of(j * ATT_TK, ATT_TK), ATT_TK), :]

    def consume(vt_chunk, src_ref):
        vt = jnp.concatenate([vt_chunk, ones], axis=0)
        for h in range(GQA):
            cols = slice(h * tq, (h + 1) * tq)
            s = src_ref[:, cols]
            m_old = m_ref[:, cols]
            m_new = jnp.maximum(m_old, jnp.max(s, axis=0, keepdims=True))
            p = jnp.exp2((s - m_new).astype(BF16))
            acc_ref[:, cols] = (jnp.exp2(m_old - m_new) * acc_ref[:, cols]
                                + jnp.dot(vt, p, preferred_element_type=F32))
            m_ref[:, cols] = m_new

    m_ref[...] = jnp.full(m_ref.shape, -jnp.inf, F32)
    acc_ref[...] = jnp.zeros(acc_ref.shape, F32)
    scores(kc_ref[...], sa_ref)
    if n_lat:
        scores(lat_keys(0), sb_ref)
        consume(vtc_ref[0], sa_ref)

        def pair(i, carry):
            j = 2 * i
            scores(lat_keys(j + 1), sa_ref)
            consume(vtl_ref[j], sb_ref)
            scores(lat_keys(jnp.minimum(j + 2, n_lat - 1)), sb_ref)
            consume(vtl_ref[j + 1], sa_ref)
            return carry

        lax.fori_loop(0, n_lat // 2, pair, 0, unroll=min(ATT_UNROLL, n_lat // 2))
    else:
        consume(vtc_ref[0], sa_ref)
    acc = acc_ref[...]
    o = acc[:HEAD_DIM] * (1.0 / acc[HEAD_DIM:HEAD_DIM + 1])
    for h in range(GQA):
        ot_ref[h * HEAD_DIM:(h + 1) * HEAD_DIM, :] = o[:, h * tq:(h + 1) * tq].astype(BF16)


def _attention(qt, kv_ctx, kv_lat=None):
    t = qt.shape[1]
    tq = min(ATT_TQ, t)
    kv_specs = lambda k, vt: [pl.BlockSpec((k.shape[0], HEAD_DIM), lambda g, i: (0, g)),
                              pl.BlockSpec((vt.shape[0], HEAD_DIM, ATT_TK), lambda g, i: (0, g, 0))]
    assert kv_ctx[0].shape[0] == ATT_TK
    n_lat = 0
    kv = list(kv_ctx)
    specs = kv_specs(*kv_ctx)
    if kv_lat is not None:
        n_lat = kv_lat[0].shape[0] // ATT_TK
        assert n_lat % 2 == 0
        kv += list(kv_lat)
        specs += kv_specs(*kv_lat)
    return pl.pallas_call(
        functools.partial(_attn_kernel, n_lat),
        grid=(ATTN_KV_HEADS, t // tq),
        in_specs=[pl.BlockSpec((GQA * HEAD_DIM, tq), lambda g, i: (g, i))] + specs,
        out_specs=pl.BlockSpec((GQA * HEAD_DIM, tq), lambda g, i: (g, i)),
        out_shape=jax.ShapeDtypeStruct((ATTN_WIDTH, t), BF16),
        scratch_shapes=[pltpu.VMEM((ATT_TK, GQA * tq), F32), pltpu.VMEM((ATT_TK, GQA * tq), F32),
                        pltpu.VMEM((1, GQA * tq), F32), pltpu.VMEM((HEAD_DIM + ATT_ONES, GQA * tq), F32)],
        compiler_params=_cparams("parallel", "parallel"),
        name="attention",
    )(qt, *kv)


def _softplus(x):
    return jnp.maximum(x, 0.0) + jnp.log(1.0 + jnp.exp(-jnp.abs(x)))


def _dot_f32(a, b):
    return jnp.dot(a, b, precision=lax.Precision.HIGHEST, preferred_element_type=F32)


def _gdn_prep_kernel(nblk, cur_ref, prev_ref, next_ref, cw_ref, ba_ref, bat_ref, alog_r, dtb_r, alog_c, dtb_c,
                     q_ref, k_ref, v_ref, gcol_ref, grow_ref, xx_ref):
    i = pl.program_id(0)
    tm = cur_ref.shape[0]
    pad = (DN_CONV - 1) // 2
    xx_ref[0:SUBLANES, :] = jnp.where(i > 0, prev_ref[...], 0.0)
    xx_ref[SUBLANES:SUBLANES + tm, :] = cur_ref[...]
    xx_ref[SUBLANES + tm:2 * SUBLANES + tm, :] = jnp.where(i < nblk - 1, next_ref[...], 0.0)
    acc = None
    for j in range(DN_CONV):
        term = xx_ref[SUBLANES - pad + j:SUBLANES - pad + j + tm, :] * cw_ref[j:j + 1, :]
        acc = term if acc is None else acc + term
    y = _silu(acc)
    for hd in range(DN_HEADS):
        sl = slice(hd * HEAD_DIM, (hd + 1) * HEAD_DIM)
        qh = y[:, sl]
        q_ref[:, sl] = qh * (lax.rsqrt(jnp.sum(qh * qh, axis=-1, keepdims=True) + EPS) * HEAD_DIM ** -0.5)
        kh = y[:, DN_WIDTH + hd * HEAD_DIM:DN_WIDTH + (hd + 1) * HEAD_DIM]
        k_ref[:, sl] = kh * lax.rsqrt(jnp.sum(kh * kh, axis=-1, keepdims=True) + EPS)
    v_ref[...] = y[:, 2 * DN_WIDTH:]

    r = lax.broadcasted_iota(jnp.int32, (tm, tm), 0)
    c = lax.broadcasted_iota(jnp.int32, (tm, tm), 1)
    same = (r // DN_CHUNK) == (c // DN_CHUNK)
    tri_le = jnp.where(same & (c <= r), 1.0, 0.0)
    tri_ge = jnp.where(same & (c >= r), 1.0, 0.0)

    ba = ba_ref[...]
    lane = lax.broadcasted_iota(jnp.int32, (1, LANES), 1)
    g = -jnp.exp(alog_r[...]) * _softplus(ba + dtb_r[...])
    gc = jnp.where(lane < N_GATES + DN_HEADS, _dot_f32(tri_le, g), _dot_f32(tri_ge, g))
    gcol_ref[...] = jnp.where(lane < N_GATES, jax.nn.sigmoid(ba), gc)

    gt = -jnp.exp(alog_c[...]) * _softplus(bat_ref[...] + dtb_c[...])
    row = lax.broadcasted_iota(jnp.int32, (2 * N_GATES, 1), 0)
    gct = jnp.where(row < N_GATES + DN_HEADS, _dot_f32(gt, tri_ge), _dot_f32(gt, tri_le))
    for ch in range(tm // DN_CHUNK):
        grow_ref[ch] = gct[N_GATES:, ch * DN_CHUNK:(ch + 1) * DN_CHUNK]


def _gdn_prep(dn, ba, bat, lw, tm):
    t = dn.shape[0]
    nblk = t // tm
    nsub = tm // SUBLANES
    w3 = 3 * DN_WIDTH
    row = lambda w: pl.BlockSpec((tm, w), lambda i: (i, 0))
    vec = lambda a: _const_spec(a.shape)
    sds = jax.ShapeDtypeStruct
    return pl.pallas_call(
        functools.partial(_gdn_prep_kernel, nblk),
        grid=(nblk,),
        in_specs=[row(w3),
                  pl.BlockSpec((SUBLANES, w3), lambda i: (jnp.maximum(i * nsub - 1, 0), 0)),
                  pl.BlockSpec((SUBLANES, w3), lambda i: (jnp.minimum((i + 1) * nsub, t // SUBLANES - 1), 0)),
                  vec(lw["conv_w"]), row(LANES), pl.BlockSpec((2 * N_GATES, tm), lambda i: (0, i)),
                  vec(lw["alog_row"]), vec(lw["dtb_row"]), vec(lw["alog_col"]), vec(lw["dtb_col"])],
        out_specs=[row(DN_WIDTH), row(DN_WIDTH), row(DN_WIDTH), row(LANES),
                   pl.BlockSpec((tm // DN_CHUNK, N_GATES, DN_CHUNK), lambda i: (i, 0, 0))],
        out_shape=[sds((t, DN_WIDTH), F32), sds((t, DN_WIDTH), F32), sds((t, DN_WIDTH), F32), sds((t, LANES), F32),
                   sds((t // DN_CHUNK, N_GATES, DN_CHUNK), F32)],
        scratch_shapes=[pltpu.VMEM((tm + 2 * SUBLANES, w3), F32)],
        compiler_params=_cparams("parallel"),
        name="gdn_prep",
    )(dn, dn, dn, lw["conv_w"], ba, bat, lw["alog_row"], lw["dtb_row"], lw["alog_col"], lw["dtb_col"])


def _dot_bf(a, b, dims=(((1,), (0,)), ((), ()))):
    return lax.dot_general(a.astype(BF16), b.astype(BF16), dims, preferred_element_type=F32)


_NT = (((1,), (1,)), ((), ()))
_TN = (((0,), (0,)), ((), ()))
TRI_BASE = 8


def _gdn_local(chains):
    c = DN_CHUNK
    ri = lax.broadcasted_iota(jnp.int32, (c, c), 0)
    ci = lax.broadcasted_iota(jnp.int32, (c, c), 1)
    blk = lambda b: (ri // b) == (ci // b)
    eye = jnp.where(ri == ci, 1.0, 0.0)
    n = range(len(chains))
    qs, ks, vs, betas, gcs, grs, fwds = zip(*chains)
    incl = [(ri >= ci) if f else (ri <= ci) for f in fwds]
    strict = [(ri > ci) if f else (ri < ci) for f in fwds]
    decay = [jnp.where(incl[i], jnp.exp(jnp.where(incl[i], gcs[i] - grs[i], 0.0)), 0.0) for i in n]
    glast = [grs[i][:, c - 1:c] if fwds[i] else grs[i][:, 0:1] for i in n]
    eg = [jnp.exp(g) for g in gcs]
    kb = [ks[i] * betas[i] for i in n]
    kbf = [k.astype(BF16) for k in ks]
    a = [jnp.where(strict[i], _dot_bf(kb[i], kbf[i], _NT) * decay[i], 0.0) for i in n]
    qk = [(_dot_bf(qs[i], kbf[i], _NT) * decay[i]).astype(BF16) for i in n]
    p = [jnp.where(blk(TRI_BASE), a[i], 0.0) for i in n]
    x = [eye - p[i] for i in n]
    for _ in range(int(math.log2(TRI_BASE)) - 1):
        p = [_dot_bf(p[i], p[i]) for i in n]
        x = [x[i] + _dot_bf(x[i], p[i]) for i in n]
    b = 2 * TRI_BASE
    while b <= c:
        off = blk(b) & jnp.logical_not(blk(b // 2))
        lx = [_dot_bf(jnp.where(off, a[i], 0.0), x[i]) for i in n]
        x = [x[i] - _dot_bf(x[i], lx[i]) for i in n]
        b *= 2
    sol = [_dot_bf(x[i], jnp.concatenate([vs[i] * betas[i], kb[i] * eg[i]], axis=1)) for i in n]
    return [(sol[i][:, :HEAD_DIM],
             jnp.concatenate([sol[i][:, HEAD_DIM:], qs[i] * eg[i]], axis=0).astype(BF16),
             (ks[i] * jnp.exp(glast[i] - gcs[i])).astype(BF16),
             qk[i]) for i in n]


GDN_LOCAL_CHUNKS = 2


def _gdn_kernel(nblk, qf, kf, vf, gcf, grf, qb, kb, vb, gcb, grb, s0_ref, of_ref, ob_ref, sout_ref,
                s_ref, u_ref, wq_ref, kd_ref, qk_ref):
    i = pl.program_id(0)
    c = DN_CHUNK

    @pl.when(i == 0)
    def _():
        s_ref[...] = s0_ref[...]

    nb = qf.shape[0] // c
    dirs = ((qf, kf, vf, gcf, grf), (qb, kb, vb, gcb, grb))

    def local_body(jp, carry):
        work = []
        for cc in range(GDN_LOCAL_CHUNKS):
            ch = jp * GDN_LOCAL_CHUNKS + cc
            rows = pl.ds(pl.multiple_of(ch * c, c), c)
            for d, (q_ref, k_ref, v_ref, gc_ref, gr_ref) in enumerate(dirs):
                gcol = gc_ref[rows, :]
                grow = gr_ref[ch]
                for h in range(DN_HEADS):
                    gi = d * DN_HEADS + h
                    sl = slice(h * HEAD_DIM, (h + 1) * HEAD_DIM)
                    work.append((ch, gi, (q_ref[rows, sl], k_ref[rows, sl], v_ref[rows, sl], gcol[:, gi:gi + 1],
                                          gcol[:, N_GATES + gi:N_GATES + gi + 1], grow[gi:gi + 1, :], d == 0)))
        done = _gdn_local([args for _, _, args in work])
        for (ch, gi, _), (u, wq, kd, qk) in zip(work, done):
            u_ref[ch, gi] = u
            wq_ref[ch, gi] = wq
            kd_ref[ch, gi] = kd
            qk_ref[ch, gi] = qk
        return carry

    lax.fori_loop(0, nb // GDN_LOCAL_CHUNKS, local_body, 0)

    def seq_body(j, carry):
        work = []
        for d, (gr_ref, o_ref) in enumerate(((grf, of_ref), (grb, ob_ref))):
            jj = j if d == 0 else nb - 1 - j
            grow = gr_ref[jj]
            for h in range(DN_HEADS):
                gi = d * DN_HEADS + h
                glast = grow[gi:gi + 1, c - 1:c] if d == 0 else grow[gi:gi + 1, 0:1]
                work.append((o_ref, jj, h, gi, s_ref[gi], u_ref[jj, gi], wq_ref[jj, gi], kd_ref[jj, gi], qk_ref[jj, gi],
                             jnp.exp(glast)))
        ws = [jnp.dot(w[6], w[4].astype(BF16), preferred_element_type=F32) for w in work]
        v_new = [(w[5] - ws_[:c]).astype(BF16) for w, ws_ in zip(work, ws)]
        o = [ws_[c:] + jnp.dot(w[8], vn, preferred_element_type=F32) for w, ws_, vn in zip(work, ws, v_new)]
        s_new = [w[4] * w[9] + lax.dot_general(w[7], vn, _TN, preferred_element_type=F32) for w, vn in zip(work, v_new)]
        for (o_ref, jj, h, gi, *_), o_, s_ in zip(work, o, s_new):
            o_ref[pl.ds(pl.multiple_of(jj * c, c), c), h * HEAD_DIM:(h + 1) * HEAD_DIM] = o_
            s_ref[gi] = s_
        return carry

    lax.fori_loop(0, nb, seq_body, 0)

    @pl.when(i == nblk - 1)
    def _():
        sout_ref[...] = s_ref[...]


def _gdn_scan(q, k, v, gcol, grow, s0, tm):
    t = q.shape[0]
    nblk = t // tm
    nch = tm // DN_CHUNK
    fwd = lambda w: pl.BlockSpec((tm, w), lambda i: (i, 0))
    bwd = lambda w: pl.BlockSpec((tm, w), lambda i: (nblk - 1 - i, 0))
    sds = jax.ShapeDtypeStruct
    s_shape = (N_GATES, HEAD_DIM, HEAD_DIM)
    return pl.pallas_call(
        functools.partial(_gdn_kernel, nblk),
        grid=(nblk,),
        in_specs=[fwd(DN_WIDTH), fwd(DN_WIDTH), fwd(DN_WIDTH), fwd(LANES),
                  pl.BlockSpec((nch, N_GATES, DN_CHUNK), lambda i: (i, 0, 0)),
                  bwd(DN_WIDTH), bwd(DN_WIDTH), bwd(DN_WIDTH), bwd(LANES),
                  pl.BlockSpec((nch, N_GATES, DN_CHUNK), lambda i: (nblk - 1 - i, 0, 0)),
                  _const_spec(s_shape)],
        out_specs=[fwd(DN_WIDTH), bwd(DN_WIDTH), pl.BlockSpec(s_shape, lambda i: (0, 0, 0))],
        out_shape=[sds((t, DN_WIDTH), F32), sds((t, DN_WIDTH), F32), sds(s_shape, F32)],
        scratch_shapes=[pltpu.VMEM(s_shape, F32),
                        pltpu.VMEM((nch, N_GATES, DN_CHUNK, HEAD_DIM), F32),
                        pltpu.VMEM((nch, N_GATES, 2 * DN_CHUNK, HEAD_DIM), BF16),
                        pltpu.VMEM((nch, N_GATES, DN_CHUNK, HEAD_DIM), BF16),
                        pltpu.VMEM((nch, N_GATES, DN_CHUNK, DN_CHUNK), BF16)],
        compiler_params=_cparams("arbitrary"),
        name="gdn_scan",
    )(q, k, v, gcol, grow, q, k, v, gcol, grow, s0)


S5_SW = 4 * S5_STATE


def _s5_matrices(a_re, a_im, log_dt, b_re, b_im, c_re, c_im, d):
    L, G, P, H = S5_CHUNK, S5_GROUPS, S5_STATE, S5_GROUP_CH
    hi = lax.Precision.HIGHEST
    a_re, a_im = a_re.astype(F32), a_im.astype(F32)
    dt = jnp.exp(log_dt.astype(F32))[..., None]
    n_re = jnp.exp(a_re * dt) * jnp.cos(a_im * dt) - 1.0
    n_im = jnp.exp(a_re * dt) * jnp.sin(a_im * dt)
    den = a_re * a_re + a_im * a_im
    co_re = ((n_re * a_re + n_im * a_im) / den)[..., None]
    co_im = ((n_im * a_re - n_re * a_im) / den)[..., None]
    br, bi = b_re.astype(F32)[None], b_im.astype(F32)[None]
    bb_re, bb_im = co_re * br - co_im * bi, co_re * bi + co_im * br
    tau = jnp.arange(L + 1, dtype=F32)[:, None, None, None]
    mag = jnp.exp(a_re[None] * dt[None] * tau)
    ang = a_im[None] * dt[None] * tau
    pw_re, pw_im = mag * jnp.cos(ang), mag * jnp.sin(ang)
    cr, ci = c_re.astype(F32), c_im.astype(F32)
    e_re = cr[None] * pw_re[:, :, :, None, :] - ci[None] * pw_im[:, :, :, None, :]
    e_im = cr[None] * pw_im[:, :, :, None, :] + ci[None] * pw_re[:, :, :, None, :]
    kk = (jnp.einsum('tdghp,dgpi->tdghi', e_re[:L], bb_re, precision=hi)
          - jnp.einsum('tdghp,dgpi->tdghi', e_im[:L], bb_im, precision=hi))
    s_i = jnp.arange(L)[:, None]
    t_i = jnp.arange(L)[None, :]
    lag = t_i - s_i
    kf = jnp.where((lag >= 0)[..., None, None, None], kk[jnp.clip(lag, 0, L - 1), 0], 0.0)
    kb = jnp.where((lag <= 0)[..., None, None, None], kk[jnp.clip(-lag, 0, L - 1), 1], 0.0)
    dskip = d.astype(F32).reshape(G, H)
    eye_t = (lag == 0).astype(F32)
    skip = eye_t[:, :, None, None, None] * (dskip[:, :, None] * jnp.eye(H, dtype=F32)[None])[None, None]
    toep = (kf + kb + skip).transpose(2, 0, 4, 1, 3).reshape(G, L * H, L * H)

    def loc(pr, pi, dr):
        w_re = pr[..., None] * bb_re[dr][None] - pi[..., None] * bb_im[dr][None]
        w_im = pr[..., None] * bb_im[dr][None] + pi[..., None] * bb_re[dr][None]
        f = lambda w: w.transpose(1, 0, 3, 2).reshape(G, L * H, P)
        return f(w_re), f(w_im)
    wf_re, wf_im = loc(pw_re[:L, 0][::-1], pw_im[:L, 0][::-1], 0)
    wb_re, wb_im = loc(pw_re[:L, 1], pw_im[:L, 1], 1)
    wcat = jnp.concatenate([toep, wf_re, wb_re, wf_im, wb_im], axis=-1).astype(BF16)

    carry = lambda e: e.transpose(1, 3, 0, 2).reshape(G, P, L * H)
    mf_re, mf_im = carry(e_re[1:, 0]), carry(-e_im[1:, 0])
    mb_re, mb_im = carry(e_re[1:, 1][::-1]), carry(-e_im[1:, 1][::-1])
    mcat = jnp.concatenate([mf_re, mb_re, mf_im, mb_im], axis=1).astype(BF16)
    al_re = jnp.concatenate([pw_re[L, 0], pw_re[L, 1]], axis=-1)
    al_im = jnp.concatenate([pw_im[L, 0], pw_im[L, 1]], axis=-1)
    al = jnp.stack([al_re.reshape(-1), al_im.reshape(-1)])
    return wcat, mcat, al


S5_HW = 2 * S5_STATE


def _s5_local_kernel(u_ref, w_ref, y_ref, xre_ref, xim_ref):
    r = jnp.dot(u_ref[...], w_ref[...], preferred_element_type=F32)
    y_ref[...] = r[:, :S5_CW]
    xre_ref[...] = r[:, S5_CW:S5_CW + S5_HW]
    xim_ref[...] = r[:, S5_CW + S5_HW:]


def _s5_local(ug, wcat):
    g, n, _ = ug.shape
    out = lambda w: pl.BlockSpec((n, w), lambda i: (0, i))
    sds = jax.ShapeDtypeStruct
    return pl.pallas_call(
        _s5_local_kernel,
        grid=(g,),
        in_specs=[pl.BlockSpec((None, n, S5_CW), lambda i: (i, 0, 0)),
                  pl.BlockSpec((None, S5_CW, S5_CW + S5_SW), lambda i: (i, 0, 0))],
        out_specs=[out(S5_CW), out(S5_HW), out(S5_HW)],
        out_shape=[sds((n, g * S5_CW), F32), sds((n, g * S5_HW), F32), sds((n, g * S5_HW), F32)],
        compiler_params=_cparams("parallel"),
        name="s5_local",
    )(ug, wcat)


def _s5_scan_kernel(nblk, xfr_ref, xfi_ref, xbr_ref, xbi_ref, al_ref, s0_ref,
                    cfr_ref, cfi_ref, cbr_ref, cbi_ref, sout_ref, s_ref):
    i = pl.program_id(0)

    @pl.when(i == 0)
    def _():
        s_ref[...] = s0_ref[...]

    nb = xfr_ref.shape[0]
    al_re, al_im = al_ref[0:1, :], al_ref[1:2, :]
    is_fwd = (lax.broadcasted_iota(jnp.int32, (1, al_ref.shape[1]), 1) % S5_HW) < S5_STATE

    def body(j, carry):
        re, im = carry
        rf, rb = pl.ds(j, 1), pl.ds(nb - 1 - j, 1)
        cfr_ref[rf, :] = re
        cfi_ref[rf, :] = im
        cbr_ref[rb, :] = re
        cbi_ref[rb, :] = im
        in_re = jnp.where(is_fwd, xfr_ref[rf, :], xbr_ref[rb, :])
        in_im = jnp.where(is_fwd, xfi_ref[rf, :], xbi_ref[rb, :])
        return al_re * re - al_im * im + in_re, al_re * im + al_im * re + in_im

    re, im = lax.fori_loop(0, nb, body, (s_ref[0:1, :], s_ref[1:2, :]))
    s_ref[0:1, :] = re
    s_ref[1:2, :] = im

    @pl.when(i == nblk - 1)
    def _():
        sout_ref[...] = s_ref[...]


def _s5_scan(x_re, x_im, al, s0, nb):
    n, w = x_re.shape
    nblk = n // nb
    fwd = pl.BlockSpec((nb, w), lambda i: (i, 0))
    bwd = pl.BlockSpec((nb, w), lambda i: (nblk - 1 - i, 0))
    sds = jax.ShapeDtypeStruct
    return pl.pallas_call(
        functools.partial(_s5_scan_kernel, nblk),
        grid=(nblk,),
        in_specs=[fwd, fwd, bwd, bwd, _const_spec(al.shape), _const_spec(s0.shape)],
        out_specs=[fwd, fwd, bwd, bwd, pl.BlockSpec(s0.shape, lambda i: (0, 0))],
        out_shape=[sds((n, w), F32)] * 4 + [sds(s0.shape, F32)],
        scratch_shapes=[pltpu.VMEM(s0.shape, F32)],
        compiler_params=_cparams("arbitrary"),
        name="s5_scan",
    )(x_re, x_im, x_re, x_im, al, s0)


def _s5_carry_kernel(y_ref, cfr_ref, cfi_ref, cbr_ref, cbi_ref, m_ref, o_ref):
    is_fwd = lax.broadcasted_iota(jnp.int32, (1, S5_HW), 1) < S5_STATE
    cin = jnp.concatenate([jnp.where(is_fwd, cfr_ref[...], cbr_ref[...]),
                           jnp.where(is_fwd, cfi_ref[...], cbi_ref[...])], axis=1).astype(BF16)
    o_ref[...] = y_ref[...] + jnp.dot(cin, m_ref[...], preferred_element_type=F32)


def _s5_carry(y, cins, mcat):
    n = y.shape[0]
    g = mcat.shape[0]
    blk = lambda w: pl.BlockSpec((n, w), lambda i: (0, i))
    return pl.pallas_call(
        _s5_carry_kernel,
        grid=(g,),
        in_specs=[blk(S5_CW)] + [blk(S5_HW)] * 4 + [pl.BlockSpec((None, S5_SW, S5_CW), lambda i: (i, 0, 0))],
        out_specs=blk(S5_CW),
        out_shape=jax.ShapeDtypeStruct(y.shape, F32),
        compiler_params=_cparams("parallel"),
        name="s5_carry",
    )(y, *cins, mcat)


def _s5_mixer(u, mats, s0):
    wcat, mcat, al = mats
    t = u.shape[0]
    n = t // S5_CHUNK
    ug = u.reshape(n, S5_CHUNK, S5_GROUPS, S5_GROUP_CH).transpose(2, 0, 1, 3).reshape(S5_GROUPS, n, S5_CW)
    y_loc, x_re, x_im = _s5_local(ug, wcat)
    *cins, s_fin = _s5_scan(x_re, x_im, al, s0, min(n, 128))
    y = _s5_carry(y_loc, cins, mcat)
    y = y.reshape(n, S5_GROUPS, S5_CHUNK, S5_GROUP_CH).transpose(0, 2, 1, 3).reshape(t, S5_WIDTH)
    return y, s_fin


ROUTER_E0 = MOE_GROUPS
ROUTE_SEL = 32
OUT_ROWS = 128


def _gelu_tanh(x):
    return 0.5 * x * (1.0 + jnp.tanh(math.sqrt(2.0 / math.pi) * (x + 0.044715 * (x * x * x))))


def _route(logits):
    ninf = float("-inf")
    lane = lax.broadcasted_iota(jnp.int32, (1, LANES), 1)
    lanef = lane.astype(F32)
    first = lambda hit: jnp.min(jnp.where(hit, lanef, float(LANES)), axis=-1, keepdims=True)
    gl = jnp.where(lane < MOE_GROUPS, logits, ninf)
    gmax = jnp.max(gl, axis=-1, keepdims=True)
    pg_top = 1.0 / jnp.sum(jnp.exp(gl - gmax), axis=-1, keepdims=True)
    base = ROUTER_E0 + MOE_PER_GROUP * first(gl == gmax)
    el = jnp.where((lanef >= base) & (lanef < base + MOE_PER_GROUP), logits, ninf)
    emax = jnp.max(el, axis=-1, keepdims=True)
    esum = jnp.sum(jnp.exp(el - emax), axis=-1, keepdims=True)
    i1 = first(el == emax)
    el2 = jnp.where(lanef == i1, ninf, el)
    emax2 = jnp.max(el2, axis=-1, keepdims=True)
    i2 = first(el2 == emax2)
    p1 = 1.0 / esum
    p2 = jnp.exp(emax2 - emax) / esum
    w1 = pg_top * (p1 / (p1 + p2))
    w2 = pg_top * (p2 / (p1 + p2))
    comb = jnp.where(lanef == i1, w1, 0.0) + jnp.where(lanef == i2, w2, 0.0)
    sel = (jnp.where(lane == ROUTE_SEL, i1 - ROUTER_E0, 0.0) + jnp.where(lane == ROUTE_SEL + 1, i2 - ROUTER_E0, 0.0)
           + jnp.where(lane == ROUTE_SEL + 2, w1, 0.0) + jnp.where(lane == ROUTE_SEL + 3, w2, 0.0))
    return comb + sel


def _bucket_rank(route, carry):
    tm = route.shape[0]
    lanef = lax.broadcasted_iota(jnp.int32, (1, LANES), 1).astype(F32)
    e1, e2 = route[:, ROUTE_SEL:ROUTE_SEL + 1], route[:, ROUTE_SEL + 1:ROUTE_SEL + 2]
    lo, hi = jnp.minimum(e1, e2), jnp.maximum(e1, e2)
    grp = jnp.floor(lo * (1.0 / MOE_PER_GROUP))
    a, b = lo - MOE_PER_GROUP * grp, hi - MOE_PER_GROUP * grp
    bucket = grp * MOE_PAIRS + (a * (2 * MOE_PER_GROUP - 1 - a) * 0.5 + (b - a - 1.0))
    onehot = jnp.where(lanef == bucket, 1.0, 0.0)
    r = lax.broadcasted_iota(jnp.int32, (tm, tm), 0)
    c = lax.broadcasted_iota(jnp.int32, (tm, tm), 1)
    before = jnp.where(c < r, 1.0, 0.0).astype(BF16)
    seen = jnp.dot(before, onehot.astype(BF16), preferred_element_type=F32) + carry
    rank = jnp.sum(onehot * seen, axis=-1, keepdims=True)
    lane = lax.broadcasted_iota(jnp.int32, (1, LANES), 1)
    route = route + jnp.where(lane == ROUTE_SEL + 4, bucket, 0.0) + jnp.where(lane == ROUTE_SEL + 5, rank, 0.0)
    return route, carry + jnp.sum(onehot, axis=0, keepdims=True)


def _out_kernel(routed, att_ref, of_ref, ob_ref, gate_ref, y_ref, x_ref, g1_ref, sh2_ref, sc2_ref, n2_ref, dnw_ref,
                bglu_ref, wglu_ref, woa_ref, wob_ref, woc_ref, wr_ref, br_ref, xo_ref, comb_ref, aux_ref, cnt_ref=None):
    if routed:
        @pl.when(pl.program_id(0) == 0)
        def _():
            cnt_ref[...] = jnp.zeros(cnt_ref.shape, F32)
        cnt = cnt_ref[0:1, :]
    tm = x_ref.shape[0]
    for c0 in range(0, tm, OUT_ROWS):
        rows = slice(c0, c0 + OUT_ROWS)
        o = of_ref[rows, :] + ob_ref[rows, :]
        gate = _silu(gate_ref[rows, :])
        dn = jnp.concatenate(
            [_rms_rows(o[:, h * HEAD_DIM:(h + 1) * HEAD_DIM]) * dnw_ref[...] for h in range(DN_HEADS)], axis=1) * gate
        z = _gelu_tanh(y_ref[rows, :])
        s5 = z * jax.nn.sigmoid(jnp.dot(z.astype(BF16), wglu_ref[...], preferred_element_type=F32) + bglu_ref[...])
        mix = (lax.dot_general(att_ref[:, rows], woa_ref[...], _TN, preferred_element_type=F32)
               + jnp.dot(dn.astype(BF16), wob_ref[...], preferred_element_type=F32)
               + jnp.dot(s5.astype(BF16), woc_ref[...], preferred_element_type=F32))
        x = x_ref[rows, :] + g1_ref[...] * mix
        h = _rms_rows(x) * n2_ref[...] * (1.0 + sc2_ref[...]) + sh2_ref[...]
        h_hi = h.astype(BF16)
        h_lo = (h - h_hi.astype(F32)).astype(BF16)
        logits = (jnp.dot(h_hi, wr_ref[0], preferred_element_type=F32) + jnp.dot(h_lo, wr_ref[0], preferred_element_type=F32)
                  + jnp.dot(h_hi, wr_ref[1], preferred_element_type=F32) + br_ref[...])
        route = _route(logits)
        if not routed:
            xo_ref[rows, :] = x
            aux_ref[rows, :] = h_hi
            comb_ref[rows, :] = route
            continue
        route, cnt = _bucket_rank(route, cnt)
        xo_ref[rows, :D_MODEL] = x
        xo_ref[rows, D_MODEL:] = route
        comb_ref[rows, :] = route
    if routed:
        cnt_ref[0:1, :] = cnt
        aux_ref[...] = cnt_ref[...]


def _mix_out(att_t, o_f, o_b, gate, y, x, mod, lw, tm, routed):
    t = x.shape[0]
    row = lambda w: pl.BlockSpec((tm, w), lambda i: (i, 0))
    vec = lambda a: _const_spec(a.shape)
    consts = [mod["g1"], mod["sh2"], mod["sc2"], lw["norm2_w"], lw["dnw"], lw["bglu"], lw["wglu"], lw["wo_a"],
              lw["wo_b"], lw["wo_c"], lw["wr"], lw["br"]]
    sds = jax.ShapeDtypeStruct
    if routed:
        out_specs = [row(D_MODEL + LANES), row(LANES), pl.BlockSpec((SUBLANES, LANES), lambda i: (0, 0))]
        out_shape = [sds((t, D_MODEL + LANES), F32), sds((t, LANES), F32), sds((SUBLANES, LANES), F32)]
        scratch = [pltpu.VMEM((SUBLANES, LANES), F32)]
    else:
        out_specs = [row(D_MODEL), row(LANES), row(D_MODEL)]
        out_shape = [sds((t, D_MODEL), F32), sds((t, LANES), F32), sds((t, D_MODEL), BF16)]
        scratch = []
    return pl.pallas_call(
        functools.partial(_out_kernel, routed),
        grid=(t // tm,),
        in_specs=[pl.BlockSpec((ATTN_WIDTH, tm), lambda i: (0, i)), row(DN_WIDTH), row(DN_WIDTH), row(DN_WIDTH),
                  row(S5_WIDTH), row(D_MODEL)] + [vec(a) for a in consts],
        out_specs=out_specs, out_shape=out_shape, scratch_shapes=scratch,
        compiler_params=_cparams("arbitrary" if routed else "parallel"),
        name="mix_out",
    )(att_t, o_f, o_b, gate, y, x, *consts)


def _moe_kernel(h_ref, comb_ref, x_ref, g2_ref, wg_ref, wu_ref, wd_ref, o_ref, acc_ref):
    e = pl.program_id(1)

    @pl.when(e == 0)
    def _():
        acc_ref[...] = jnp.zeros(acc_ref.shape, F32)

    h = h_ref[...]
    mid = _silu(jnp.dot(h, wg_ref[...], preferred_element_type=F32)) * jnp.dot(h, wu_ref[...], preferred_element_type=F32)
    lane = lax.broadcasted_iota(jnp.int32, (1, LANES), 1)
    cw = jnp.sum(jnp.where(lane == e + ROUTER_E0, comb_ref[...], 0.0), axis=-1, keepdims=True)
    acc_ref[...] += cw * jnp.dot(mid.astype(BF16), wd_ref[...], preferred_element_type=F32)

    @pl.when(e == MOE_EXPERTS - 1)
    def _():
        o_ref[...] = x_ref[...] + g2_ref[...] * acc_ref[...]


def _moe(h, comb, x, g2, lw, tm):
    t = x.shape[0]
    row = lambda w: pl.BlockSpec((tm, w), lambda i, e: (i, 0))
    return pl.pallas_call(
        _moe_kernel,
        grid=(t // tm, MOE_EXPERTS),
        in_specs=[row(D_MODEL), row(LANES), row(D_MODEL), _const_spec(g2.shape),
                  pl.BlockSpec((None, D_MODEL, MOE_HIDDEN), lambda i, e: (e, 0, 0)),
                  pl.BlockSpec((None, D_MODEL, MOE_HIDDEN), lambda i, e: (e, 0, 0)),
                  pl.BlockSpec((None, MOE_HIDDEN, D_MODEL), lambda i, e: (e, 0, 0))],
        out_specs=row(D_MODEL),
        out_shape=jax.ShapeDtypeStruct((t, D_MODEL), F32),
        scratch_shapes=[pltpu.VMEM((tm, D_MODEL), F32)],
        compiler_params=_cparams("parallel", "arbitrary"),
        name="moe",
    )(h, comb, x, g2, lw["w_gate"], lw["w_up"], lw["w_down"])


MOE_PAIRS = MOE_PER_GROUP * (MOE_PER_GROUP - 1) // 2
MOE_BUCKETS = MOE_GROUPS * MOE_PAIRS
MOE_TM = 128
MOE_SCATTER_GROUP = 8


def _moe_plan(route, counts, t):
    n_tiles = t // MOE_TM + MOE_BUCKETS
    bucket = route[:, ROUTE_SEL + 4].astype(jnp.int32)
    rank = route[:, ROUTE_SEL + 5].astype(jnp.int32)
    counts = counts[0, :MOE_BUCKETS].astype(jnp.int32)
    tiles = (counts + MOE_TM - 1) // MOE_TM
    tile_end = jnp.cumsum(tiles)
    tile_off = tile_end - tiles
    slot = tile_off[bucket] * MOE_TM + rank
    tok = jnp.zeros(((n_tiles + 1) * MOE_TM,), jnp.int32).at[slot].set(jnp.arange(t, dtype=jnp.int32))
    tile = jnp.arange(n_tiles)
    tb = jnp.minimum(jnp.sum((tile[:, None] >= tile_end[None, :]).astype(jnp.int32), axis=1), MOE_BUCKETS - 1)
    nvalid = jnp.clip(counts[tb] - (tile - tile_off[tb]) * MOE_TM, 0, MOE_TM)
    nvalid = jnp.where(tile < tile_end[-1], nvalid, 0).astype(jnp.int32)
    pairs = [(i, j) for i in range(MOE_PER_GROUP) for j in range(i + 1, MOE_PER_GROUP)]
    pa = jnp.array([p[0] for p in pairs], jnp.int32)
    pb = jnp.array([p[1] for p in pairs], jnp.int32)
    grp = (tb // MOE_PAIRS) * MOE_PER_GROUP
    e_lo = (grp + pa[tb % MOE_PAIRS]).astype(jnp.int32)
    e_hi = (grp + pb[tb % MOE_PAIRS]).astype(jnp.int32)
    return tok, nvalid, e_lo, e_hi


def _moe_routed_kernel(n_tiles, tok_ref, nv_ref, elo_ref, ehi_ref,
                       x_hbm, g2_ref, sh2_ref, sc2_ref, n2_ref, wgl_ref, wul_ref, wdl_ref,
                       wgh_ref, wuh_ref, wdh_ref, o_hbm, xbuf, obuf, gsem, ssem):
    t = pl.program_id(0)
    slot = lax.rem(t, 2)

    def row_copy(tile, s, r, scatter):
        tok = tok_ref[tile * MOE_TM + r]
        if scatter:
            return pltpu.make_async_copy(obuf.at[s, pl.ds(r, 1)], o_hbm.at[pl.ds(tok, 1)], ssem.at[s])
        return pltpu.make_async_copy(x_hbm.at[pl.ds(tok, 1)], xbuf.at[s, pl.ds(r, 1)], gsem.at[s])

    def start_gather(tile, s):
        for r in range(MOE_TM):
            row_copy(tile, s, r, False).start(priority=r % 2)

    def wait_gather(s):
        pltpu.make_async_copy(x_hbm.at[pl.ds(0, MOE_TM)], xbuf.at[s], gsem.at[s]).wait()

    def start_scatter(tile, s):
        n = nv_ref[tile]
        for g in range(MOE_TM // MOE_SCATTER_GROUP):
            @pl.when((g + 1) * MOE_SCATTER_GROUP <= n)
            def _():
                for r in range(g * MOE_SCATTER_GROUP, (g + 1) * MOE_SCATTER_GROUP):
                    row_copy(tile, s, r, True).start(priority=r % 2)

        def body(r, carry):
            row_copy(tile, s, r, True).start()
            return carry
        lax.fori_loop(n - lax.rem(n, MOE_SCATTER_GROUP), n, body, 0)

    def wait_scatter(tile, s):
        n = nv_ref[tile]
        for bit in range(MOE_TM.bit_length()):
            rows = 1 << bit

            @pl.when((n >> bit) & 1 == 1)
            def _():
                pltpu.make_async_copy(obuf.at[s, pl.ds(0, rows)], o_hbm.at[pl.ds(0, rows)], ssem.at[s]).wait()

    @pl.when(t == 0)
    def _():
        start_gather(0, 0)

    wait_gather(slot)

    @pl.when(t >= 2)
    def _():
        wait_scatter(t - 2, slot)

    start_gather(t + 1, 1 - slot)
    x = xbuf[slot, :, :D_MODEL]
    rec = xbuf[slot, :, D_MODEL:]
    h = (_rms_rows(x) * n2_ref[...] * (1.0 + sc2_ref[...]) + sh2_ref[...]).astype(BF16)
    top1_is_lo = rec[:, ROUTE_SEL:ROUTE_SEL + 1] == elo_ref[t].astype(F32)
    w1, w2 = rec[:, ROUTE_SEL + 2:ROUTE_SEL + 3], rec[:, ROUTE_SEL + 3:ROUTE_SEL + 4]
    w = (jnp.where(top1_is_lo, w1, w2), jnp.where(top1_is_lo, w2, w1))
    y = None
    for k, (wg_ref, wu_ref, wd_ref) in enumerate(((wgl_ref, wul_ref, wdl_ref), (wgh_ref, wuh_ref, wdh_ref))):
        mid = (_silu(jnp.dot(h, wg_ref[...], preferred_element_type=F32))
               * jnp.dot(h, wu_ref[...], preferred_element_type=F32))
        yk = w[k] * jnp.dot(mid.astype(BF16), wd_ref[...], preferred_element_type=F32)
        y = yk if y is None else y + yk
    obuf[slot] = x + g2_ref[...] * y
    start_scatter(t, slot)

    @pl.when(t == n_tiles - 1)
    def _():
        wait_gather(1 - slot)
        if n_tiles >= 2:
            wait_scatter(t - 1, 1 - slot)
        wait_scatter(t, slot)


def _moe_routed(x, route, counts, mod, lw):
    t = x.shape[0]
    tok, nvalid, e_lo, e_hi = _moe_plan(route, counts, t)
    n_tiles = nvalid.shape[0]
    vec = lambda a: pl.BlockSpec(a.shape, lambda i, *_: (0,) * a.ndim, pipeline_mode=pl.Buffered(1))
    wspec = lambda shape, which: pl.BlockSpec(
        (None,) + shape, (lambda i, tok, nv, elo, ehi: (elo[i], 0, 0)) if which == 0
        else (lambda i, tok, nv, elo, ehi: (ehi[i], 0, 0)))
    up, down = (D_MODEL, MOE_HIDDEN), (MOE_HIDDEN, D_MODEL)
    consts = [mod["g2"], mod["sh2"], mod["sc2"], lw["norm2_w"]]
    grid_spec = pltpu.PrefetchScalarGridSpec(
        num_scalar_prefetch=4,
        grid=(n_tiles,),
        in_specs=[pl.BlockSpec(memory_space=pl.ANY)] + [vec(a) for a in consts]
                 + [wspec(up, 0), wspec(up, 0), wspec(down, 0), wspec(up, 1), wspec(up, 1), wspec(down, 1)],
        out_specs=pl.BlockSpec(memory_space=pl.ANY),
        scratch_shapes=[pltpu.VMEM((2, MOE_TM, D_MODEL + LANES), F32), pltpu.VMEM((2, MOE_TM, D_MODEL), F32),
                        pltpu.SemaphoreType.DMA((2,)), pltpu.SemaphoreType.DMA((2,))],
    )
    return pl.pallas_call(
        functools.partial(_moe_routed_kernel, n_tiles),
        grid_spec=grid_spec,
        out_shape=jax.ShapeDtypeStruct((t, D_MODEL), F32),
        compiler_params=_cparams("arbitrary"),
        name="moe_routed",
    )(tok, nvalid, e_lo, e_hi, x, *consts, lw["w_gate"], lw["w_up"], lw["w_down"],
      lw["w_gate"], lw["w_up"], lw["w_down"])


def _rope_tables(n_tokens):
    rows = n_tokens // GRID_W
    half = HEAD_DIM // 2
    inv = ROPE_THETA ** (-jnp.arange(0, half, 2, dtype=F32) / half)
    nf = inv.shape[0]
    ang_r = jnp.arange(rows, dtype=F32)[:, None] * inv[None, :]
    ang_c = jnp.arange(GRID_W, dtype=F32)[:, None] * inv[None, :]

    def table(fn):
        r = jnp.broadcast_to(fn(ang_r)[:, None, :], (rows, GRID_W, nf))
        c = jnp.broadcast_to(fn(ang_c)[None, :, :], (rows, GRID_W, nf))
        return jnp.concatenate([r, r, c, c], axis=-1).reshape(rows * GRID_W, HEAD_DIM)

    cos, sin = table(jnp.cos), table(jnp.sin)
    return cos, sin, cos.T, sin.T


def _layer_weights(l, p):
    o = [0]
    for s in IN_SPLITS:
        o.append(o[-1] + s)
    w_in = p["w_in"][l]
    seg = lambda i: w_in[:, o[i]:o[i + 1]]
    ba = jnp.concatenate([seg(5), seg(6)], axis=1)
    lanes16 = lambda v: jnp.zeros((LANES,), F32).at[N_GATES:2 * N_GATES].set(v.reshape(N_GATES).astype(F32))
    alog, dtb = lanes16(p["dn_a_log"][l]), lanes16(p["dn_dt_bias"][l])
    w_out = p["w_out"][l]
    wr = jnp.zeros((D_MODEL, LANES), F32)
    wr = wr.at[:, :MOE_GROUPS].set(p["router_g_w"][l]).at[:, ROUTER_E0:ROUTER_E0 + MOE_EXPERTS].set(p["router_e_w"][l])
    br = jnp.zeros((1, LANES), F32)
    br = br.at[0, :MOE_GROUPS].set(p["router_g_b"][l]).at[0, ROUTER_E0:ROUTER_E0 + MOE_EXPERTS].set(p["router_e_b"][l])
    return dict(
        norm1_w=p["norm1_w"][l][None], norm2_w=p["norm2_w"][l][None],
        wqt=seg(0).T.astype(BF16), wk=seg(1).astype(BF16), wvt=seg(2).T.astype(BF16), wdn=seg(3).astype(BF16),
        wg=seg(4).astype(BF16), wba=jnp.pad(ba, ((0, 0), (0, LANES - 2 * N_GATES))).astype(BF16),
        wbat=ba.T.astype(BF16), ws5=seg(7).astype(BF16),
        qn_col=p["q_norm_w"][l][:, None], kn_row=p["k_norm_w"][l][None],
        conv_w=jnp.pad(p["dn_conv_w"][l], ((0, SUBLANES - DN_CONV), (0, 0))),
        alog_row=alog[None], dtb_row=dtb[None], alog_col=alog[:2 * N_GATES, None], dtb_col=dtb[:2 * N_GATES, None],
        dnw=p["dn_out_norm_w"][l][None], wglu=p["s5_w_glu"][l].astype(BF16), bglu=p["s5_b_glu"][l][None],
        wo_a=w_out[:ATTN_WIDTH].astype(BF16), wo_b=w_out[ATTN_WIDTH:ATTN_WIDTH + DN_WIDTH].astype(BF16),
        wo_c=w_out[ATTN_WIDTH + DN_WIDTH:].astype(BF16), br=br,
        wr=jnp.stack([wr.astype(BF16), (wr - wr.astype(BF16).astype(F32)).astype(BF16)]),
        w_gate=p["w_gate"][l].astype(BF16), w_up=p["w_up"][l].astype(BF16), w_down=p["w_down"][l].astype(BF16),
        s5=_s5_matrices(p["s5_a_re"][l], p["s5_a_im"][l], p["s5_log_dt"][l], p["s5_b_re"][l], p["s5_b_im"][l],
                        p["s5_c_re"][l], p["s5_c_im"][l], p["s5_d"][l]),
    )


TM_CTX = 256
TM_IN = 256
TM_SEQ = 512


def _mixers(x, mod, lw, tabs, rope, s_dn, s_s5, tm_in, tm):
    qt, k, vt, dn, gate, ba, bat, u = _in_proj(x, mod, lw, tabs, rope, tm_in)
    q_dn, k_dn, v_dn, gcol, grow = _gdn_prep(dn, ba, bat, lw, tm)
    o_f, o_b, s_dn = _gdn_scan(q_dn, k_dn, v_dn, gcol, grow, s_dn, tm)
    y, s_s5 = _s5_mixer(u, lw["s5"], s_s5)
    return dict(qt=qt, k=k, vt=vt, gate=gate, o_f=o_f, o_b=o_b, y=y), s_dn, s_s5


def kernel(x, c, ctx, c_ctx, w_ada, b_ada, norm1_w, norm2_w, w_in, q_norm_w, k_norm_w, dn_conv_w, dn_a_log, dn_dt_bias,
           dn_out_norm_w, s5_a_re, s5_a_im, s5_log_dt, s5_b_re, s5_b_im, s5_c_re, s5_c_im, s5_d, s5_w_glu, s5_b_glu,
           w_out, router_g_w, router_g_b, router_e_w, router_e_b, w_gate, w_up, w_down):
    p = dict(norm1_w=norm1_w, norm2_w=norm2_w, w_in=w_in, q_norm_w=q_norm_w, k_norm_w=k_norm_w, dn_conv_w=dn_conv_w,
             dn_a_log=dn_a_log, dn_dt_bias=dn_dt_bias, dn_out_norm_w=dn_out_norm_w, s5_a_re=s5_a_re, s5_a_im=s5_a_im,
             s5_log_dt=s5_log_dt, s5_b_re=s5_b_re, s5_b_im=s5_b_im, s5_c_re=s5_c_re, s5_c_im=s5_c_im, s5_d=s5_d,
             s5_w_glu=s5_w_glu, s5_b_glu=s5_b_glu, w_out=w_out, router_g_w=router_g_w, router_g_b=router_g_b,
             router_e_w=router_e_w, router_e_b=router_e_b, w_gate=w_gate, w_up=w_up, w_down=w_down)
    assert x.shape[0] == 1 and ctx.shape[0] == 1
    xl, xc = x[0], ctx[0]
    n_ctx = xc.shape[0]
    cvec = jnp.zeros((SUBLANES, D_MODEL), F32).at[0].set(c[0]).at[1].set(c_ctx)
    mods = _adaln_mod(cvec, w_ada, b_ada)
    tabs = _rope_tables(xl.shape[0])
    tabs_c = (tabs[0][:n_ctx], tabs[1][:n_ctx], tabs[2][:, :n_ctx], tabs[3][:, :n_ctx])
    names = ("sh1", "sc1", "g1", "sh2", "sc2", "g2")
    for l in range(DEPTH):
        lw = _layer_weights(l, p)
        mod_l = {n: mods[l, 0:1, i * D_MODEL:(i + 1) * D_MODEL] for i, n in enumerate(names)}
        mod_c = {n: mods[l, 1:2, i * D_MODEL:(i + 1) * D_MODEL] for i, n in enumerate(names)}
        s_dn0 = jnp.zeros((N_GATES, HEAD_DIM, HEAD_DIM), F32)
        s_s50 = jnp.zeros((2, S5_GROUPS * S5_HW), F32)
        mc, s_dn, s_s5 = _mixers(xc, mod_c, lw, tabs_c, False, s_dn0, s_s50, TM_CTX, TM_CTX)
        ml, _, _ = _mixers(xl, mod_l, lw, tabs, True, s_dn, s_s5, TM_IN, TM_SEQ)
        att_l = _attention(ml["qt"], (mc["k"], mc["vt"]), (ml["k"], ml["vt"]))
        xl, route, counts = _mix_out(att_l, ml["o_f"], ml["o_b"], ml["gate"], ml["y"], xl, mod_l, lw, TM_SEQ, True)
        xl = _moe_routed(xl, route, counts, mod_l, lw)
        if l < DEPTH - 1:
            att_c = _attention(mc["qt"], (mc["k"], mc["vt"]))
            xc, comb, h2 = _mix_out(att_c, mc["o_f"], mc["o_b"], mc["gate"], mc["y"], xc, mod_c, lw, TM_CTX, False)
            xc = _moe(h2, comb, xc, mod_c["g2"], lw, TM_CTX)
    return xl[None]
```
